```python
import jax, jax.numpy as jnp
from jax import lax
import numpy as np

D_MODEL = 2048
BATCH = 4
SEQ = 4096
DEPTH = 4

CHUNK = 64
N_META = 16
Q_BLOCK = 128
ROPE_THETA = 10000.0
EPS = 1e-6

HEAD_DIM = 128
A_WIDTH = (3 * D_MODEL) // 8
B_WIDTH = (3 * D_MODEL) // 8
C_WIDTH = D_MODEL - A_WIDTH - B_WIDTH
D_MIX = A_WIDTH + B_WIDTH + C_WIDTH
A_HEADS = A_WIDTH // HEAD_DIM
IDX_HEADS = 16
IDX_DIM = 64
TOPK_MAX = 256
B_HEADS = B_WIDTH // HEAD_DIM
C_DV = 128
C_HEADS = C_WIDTH // C_DV
C_DK = C_DV // 2
C_GATE_RANK = 16
C_TAU = 16.0
N_GROUPS = 4
EXP_PER_GROUP = 8
N_EXPERTS = N_GROUPS * EXP_PER_GROUP
EXP_TOPK = 2
D_EXPERT = 512

IN_SPLITS = (A_WIDTH, HEAD_DIM, HEAD_DIM, IDX_HEADS * IDX_DIM, IDX_DIM, IDX_HEADS,
             B_WIDTH, B_WIDTH, B_WIDTH, B_HEADS,
             C_HEADS * C_DK, C_HEADS * C_DK, C_WIDTH, C_GATE_RANK, C_WIDTH)
D_IN = (A_WIDTH + 2 * HEAD_DIM + IDX_HEADS * IDX_DIM + IDX_DIM + IDX_HEADS
        + 3 * B_WIDTH + B_HEADS + 2 * C_HEADS * C_DK + 2 * C_WIDTH + C_GATE_RANK)

kernel_name = "hybrid_chunk_causal_dsa_fox_gla_hmoe"


def _rms(x, g):
    xf = x.astype(jnp.float32)
    y = xf * lax.rsqrt(jnp.mean(xf * xf, axis=-1, keepdims=True) + EPS)
    return (y * g.astype(jnp.float32)).astype(x.dtype)


def _rope_tables(T, dim):
    pos = jnp.arange(T, dtype=jnp.float32)
    inv = 1.0 / (ROPE_THETA ** (jnp.arange(0, dim, 2, dtype=jnp.float32) / dim))
    ang = pos[:, None] * inv[None, :]
    return jnp.cos(ang), jnp.sin(ang)


def _rope(x, cos, sin):
    half = x.shape[-1] // 2
    shape = (1, cos.shape[0]) + (1,) * (x.ndim - 3) + (half,)
    c = cos.reshape(shape).astype(x.dtype)
    s = sin.reshape(shape).astype(x.dtype)
    x1, x2 = x[..., :half], x[..., half:]
    return jnp.concatenate([x1 * c - x2 * s, x2 * c + x1 * s], axis=-1)


def _chunk_of(p):
    return jnp.where(p < N_META, 0, 1 + (p - N_META) // CHUNK)


def _split_cols(z, sizes):
    out, off = [], 0
    for s in sizes:
        out.append(z[..., off:off + s])
        off += s
    return out


def _to_blocks(a, nb):
    pad = nb * Q_BLOCK - a.shape[1]
    a = jnp.pad(a, [(0, 0), (0, pad)] + [(0, 0)] * (a.ndim - 2))
    return jnp.moveaxis(a.reshape((a.shape[0], nb, Q_BLOCK) + a.shape[2:]), 1, 0)


def _from_blocks(o, T):
    o = jnp.moveaxis(o, 0, 1)
    return o.reshape((o.shape[0], o.shape[1] * o.shape[2]) + o.shape[3:])[:, :T]


def _dsa_attention(q, k, v, iq, ik, iw, k_top):
    n_batch, T, H, DH = q.shape
    nb = -(-T // Q_BLOCK)
    key_chunk = _chunk_of(jnp.arange(T, dtype=jnp.int32))
    qpos = jnp.arange(nb * Q_BLOCK, dtype=jnp.int32).reshape(nb, Q_BLOCK)

    def block(args):
        qb, iqb, iwb, qp = args
        rel = jax.nn.relu(jnp.einsum('bqhd,bsd->bqhs', iqb, ik) * IDX_DIM ** -0.5)
        score = jnp.einsum('bqh,bqhs->bqs', iwb, rel).astype(jnp.float32)
        admissible = key_chunk[None, :] <= _chunk_of(qp)[:, None]
        score = jnp.where(admissible[None], score, -jnp.inf)
        top_val, top_idx = lax.top_k(score, k_top)
        valid = jnp.isfinite(top_val)
        k_sel = jax.vmap(lambda kb, ib: kb[ib])(k, top_idx)
        v_sel = jax.vmap(lambda vb, ib: vb[ib])(v, top_idx)
        logits = jnp.einsum('bqhd,bqkd->bhqk', qb, k_sel).astype(jnp.float32) * DH ** -0.5
        logits = jnp.where(valid[:, None], logits, -jnp.inf)
        p = jax.nn.softmax(logits, axis=-1).astype(v.dtype)
        return jnp.einsum('bhqk,bqkd->bqhd', p, v_sel)

    out = lax.map(block, (_to_blocks(q, nb), _to_blocks(iq, nb), _to_blocks(iw, nb), qpos))
    return _from_blocks(out, T)


def _forgetting_attention(q, k, v, log_f):
    n_batch, T, H, DH = q.shape
    nb = -(-T // Q_BLOCK)
    F = jnp.cumsum(log_f, axis=1)
    F_keys = jnp.transpose(F, (0, 2, 1))
    kpos = jnp.arange(T, dtype=jnp.int32)
    qpos = jnp.arange(nb * Q_BLOCK, dtype=jnp.int32).reshape(nb, Q_BLOCK)

    def block(args):
        qb, Fq, qp = args
        logits = jnp.einsum('bqhd,bshd->bhqs', qb, k).astype(jnp.float32) * DH ** -0.5
        logits = logits + jnp.transpose(Fq, (0, 2, 1))[..., None] - F_keys[:, :, None, :]
        logits = jnp.where(kpos[None, :] <= qp[:, None], logits, -jnp.inf)
        p = jax.nn.softmax(logits, axis=-1).astype(v.dtype)
        return jnp.einsum('bhqs,bshd->bqhd', p, v)

    out = lax.map(block, (_to_blocks(q, nb), _to_blocks(F, nb), qpos))
    return _from_blocks(out, T)


def _gla(q, k, v, log_a):
    n_batch, T, H, DK = q.shape
    DV = v.shape[-1]
    front = (-N_META) % CHUNK
    back = (-(T + front)) % CHUNK
    n_chunks = (T + front + back) // CHUNK

    def to_chunks(a):
        a = jnp.pad(a, [(0, 0), (front, back), (0, 0), (0, 0)])
        return jnp.moveaxis(a.reshape(n_batch, n_chunks, CHUNK, H, a.shape[-1]), 1, 0)

    tri = jnp.tril(jnp.ones((CHUNK, CHUNK), dtype=bool))[None, :, :, None, None]

    def step(S, inp):
        qc, kc, vc, gc = (t.astype(jnp.float32) for t in inp)
        b = jnp.cumsum(gc, axis=1)
        decay = jnp.exp(jnp.where(tri, b[:, :, None] - b[:, None, :], -jnp.inf))
        scores = jnp.einsum('bijhd,bjhd->bhij', qc[:, :, None] * decay, kc)
        o = (jnp.einsum('bhij,bjhv->bihv', scores, vc)
             + jnp.einsum('bihd,bhdv->bihv', qc * jnp.exp(b), S))
        b_last = b[:, -1]
        S = (jnp.exp(b_last)[..., None] * S
             + jnp.einsum('bjhd,bjhv->bhdv', kc * jnp.exp(b_last[:, None] - b), vc))
        return S, o

    S0 = jnp.zeros((n_batch, H, DK, DV), jnp.float32)
    _, o = lax.scan(step, S0, (to_chunks(q), to_chunks(k), to_chunks(v), to_chunks(log_a)))
    o = jnp.moveaxis(o, 0, 1).reshape(n_batch, n_chunks * CHUNK, H, DV)
    return o[:, front:front + T].astype(v.dtype)


def _hier_moe(u, wg, bg, we, be, w1, w3, w2):
    g_prob = jax.nn.softmax((u @ wg).astype(jnp.float32) + bg.astype(jnp.float32), axis=-1)
    g_p, g_i = lax.top_k(g_prob, 1)
    e_logits = ((u @ we).astype(jnp.float32) + be.astype(jnp.float32)).reshape(
        u.shape[:-1] + (N_GROUPS, EXP_PER_GROUP))
    e_logits = jnp.sum(e_logits * jax.nn.one_hot(g_i[..., 0], N_GROUPS, dtype=jnp.float32)[..., None], axis=-2)
    e_prob = jax.nn.softmax(e_logits, axis=-1)
    e_p, e_i = lax.top_k(e_prob, EXP_TOPK)
    e_p = e_p / jnp.sum(e_p, axis=-1, keepdims=True)
    expert_id = g_i * EXP_PER_GROUP + e_i
    weight = g_p * e_p
    gate = jnp.sum(jax.nn.one_hot(expert_id, N_EXPERTS, dtype=jnp.float32) * weight[..., None],
                   axis=-2).astype(u.dtype)
    y = jnp.zeros_like(u)
    for e in range(N_EXPERTS):
        hid = jax.nn.silu(u @ w1[e]) * (u @ w3[e])
        y = y + gate[..., e:e + 1] * (hid @ w2[e])
    return y


def setup_inputs(seed: int = 0) -> dict:
    key = jax.random.key(seed)
    ks = jax.random.split(key, 21)
    L, D = DEPTH, D_MODEL
    f32 = jnp.float32

    def nrm(k, shape, scale):
        return jax.random.normal(k, shape, f32) * scale

    def gain(k, shape):
        return 1.0 + 0.02 * jax.random.normal(k, shape, f32)

    res_scale = (2.0 * DEPTH) ** -0.5
    return {
        "x": nrm(ks[0], (BATCH, SEQ, D), 1.0),
        "meta": nrm(ks[1], (N_META, D), 1.0),
        "norm_mix_g": gain(ks[2], (L, D)),
        "norm_ffn_g": gain(ks[3], (L, D)),
        "w_in": nrm(ks[4], (L, D, D_IN), D ** -0.5),
        "a_qn_g": gain(ks[5], (L, HEAD_DIM)),
        "a_kn_g": gain(ks[6], (L, HEAD_DIM)),
        "b_qn_g": gain(ks[7], (L, HEAD_DIM)),
        "b_kn_g": gain(ks[8], (L, HEAD_DIM)),
        "b_f_bias": jax.random.uniform(ks[9], (L, B_HEADS), f32, 1.0, 4.0),
        "c_gate_w2": nrm(ks[10], (L, C_GATE_RANK, C_HEADS * C_DK), C_GATE_RANK ** -0.5),
        "c_gate_b": nrm(ks[11], (L, C_HEADS * C_DK), 0.1),
        "c_on_g": gain(ks[12], (L, C_DV)),
        "w_out": nrm(ks[13], (L, D_MIX, D), D_MIX ** -0.5 * res_scale),
        "r_group_w": nrm(ks[14], (L, D, N_GROUPS), D ** -0.5),
        "r_group_b": nrm(ks[15], (L, N_GROUPS), 0.01),
        "r_exp_w": nrm(ks[16], (L, D, N_EXPERTS), D ** -0.5),
        "r_exp_b": nrm(ks[17], (L, N_EXPERTS), 0.01),
        "e_w1": nrm(ks[18], (L, N_EXPERTS, D, D_EXPERT), D ** -0.5),
        "e_w3": nrm(ks[19], (L, N_EXPERTS, D, D_EXPERT), D ** -0.5),
        "e_w2": nrm(ks[20], (L, N_EXPERTS, D_EXPERT, D), D_EXPERT ** -0.5 * res_scale),
    }


def reference(x, meta, norm_mix_g, norm_ffn_g, w_in, a_qn_g, a_kn_g, b_qn_g, b_kn_g, b_f_bias,
              c_gate_w2, c_gate_b, c_on_g, w_out, r_group_w, r_group_b, r_exp_w, r_exp_b,
              e_w1, e_w3, e_w2):
    n_batch, n_seq, d = x.shape
    T = n_seq + N_META
    k_top = min(TOPK_MAX, n_seq // 4)
    cos, sin = _rope_tables(T, HEAD_DIM)
    icos, isin = _rope_tables(T, IDX_DIM)
    h = jnp.concatenate(
        [jnp.broadcast_to(meta[None].astype(x.dtype), (n_batch, N_META, d)), x], axis=1)
    for l in range(DEPTH):
        u = _rms(h, norm_mix_g[l])
        z = u @ w_in[l]
        (aq, ak, av, iq, ik, iw, bq, bk, bv, bf, cq, ck, cv, cr, cg) = _split_cols(z, IN_SPLITS)

        aq = _rope(_rms(aq.reshape(n_batch, T, A_HEADS, HEAD_DIM), a_qn_g[l]), cos, sin)
        ak = _rope(_rms(ak, a_kn_g[l]), cos, sin)
        iq = _rope(iq.reshape(n_batch, T, IDX_HEADS, IDX_DIM), icos, isin)
        ik = _rope(ik, icos, isin)
        oa = _dsa_attention(aq, ak, av, iq, ik, iw * IDX_HEADS ** -0.5, k_top)

        bq = _rms(bq.reshape(n_batch, T, B_HEADS, HEAD_DIM), b_qn_g[l])
        bk = _rms(bk.reshape(n_batch, T, B_HEADS, HEAD_DIM), b_kn_g[l])
        log_f = jax.nn.log_sigmoid(bf.astype(jnp.float32) + b_f_bias[l].astype(jnp.float32))
        ob = _forgetting_attention(bq, bk, bv.reshape(n_batch, T, B_HEADS, HEAD_DIM), log_f)

        log_a = jax.nn.log_sigmoid((cr @ c_gate_w2[l] + c_gate_b[l]).astype(jnp.float32)) / C_TAU
        oc = _gla(cq.reshape(n_batch, T, C_HEADS, C_DK) * C_DK ** -0.5,
                  ck.reshape(n_batch, T, C_HEADS, C_DK),
                  cv.reshape(n_batch, T, C_HEADS, C_DV),
                  log_a.reshape(n_batch, T, C_HEADS, C_DK))
        oc = _rms(oc, c_on_g[l]) * jax.nn.silu(cg.reshape(n_batch, T, C_HEADS, C_DV))

        mixed = jnp.concatenate([oa.reshape(n_batch, T, A_WIDTH),
                                 ob.reshape(n_batch, T, B_WIDTH),
                                 oc.reshape(n_batch, T, C_WIDTH)], axis=-1)
        h = h + mixed @ w_out[l]

        u = _rms(h, norm_ffn_g[l])
        h = h + _hier_moe(u, r_group_w[l], r_group_b[l], r_exp_w[l], r_exp_b[l],
                          e_w1[l], e_w3[l], e_w2[l])
    return h[:, N_META:]
```

```python
import functools

import jax
import jax.numpy as jnp
from jax import lax
from jax.experimental import pallas as pl
from jax.experimental.pallas import tpu as pltpu

f32 = jnp.float32
bf16 = jnp.bfloat16
i32 = jnp.int32

D_MODEL = 2048
CHUNK = 64
N_META = 16
ROPE_THETA = 10000.0
EPS = 1e-6
HEAD_DIM = 128
A_HEADS = 6
IDX_HEADS = 16
IDX_DIM = 64
TOPK_MAX = 256
B_HEADS = 6
C_HEADS = 4
C_DK = 64
C_DV = 128
C_GATE_RANK = 16
C_TAU = 16.0
N_GROUPS = 4
EXP_PER_GROUP = 8
N_EXPERTS = N_GROUPS * EXP_PER_GROUP
D_EXPERT = 512
A_WIDTH = A_HEADS * HEAD_DIM
B_WIDTH = B_HEADS * HEAD_DIM
C_WIDTH = C_HEADS * C_DV
IN_SPLITS = (A_WIDTH, HEAD_DIM, HEAD_DIM, IDX_HEADS * IDX_DIM, IDX_DIM, IDX_HEADS,
             B_WIDTH, B_WIDTH, B_WIDTH, B_HEADS,
             C_HEADS * C_DK, C_HEADS * C_DK, C_WIDTH, C_GATE_RANK, C_WIDTH)

LANES = 128
FRONT = (-N_META) % CHUNK
BLK = 128
T_AQ, T_AK, T_AV, T_IQ, T_IK = 0, 6, 7, 8, 16
T_BQ, T_BK, T_BV = 18, 24, 30
T_CQ, T_CK, T_CV, T_CG = 36, 38, 40, 44
N_CAT = 48 * LANES
S_IW, S_BF, S_CR = 0, 16, 22
R_GRP, R_EXP = 0, 32
GLA_C = 32
MOE_TM = 256
EXP_TOPK_CONST = 2
NEG = -1e30
INT_MIN = -2 ** 31
VMEM_LIMIT = 56 * 1024 * 1024


def _cparams(sem):
    return pltpu.CompilerParams(dimension_semantics=sem, vmem_limit_bytes=VMEM_LIMIT)


def _row_tile(n, cap):
    t = cap
    while n % t:
        t //= 2
    return t


def _log_sigmoid(x):
    return jnp.minimum(x, 0.0) - jnp.log(1.0 + jnp.exp(-jnp.abs(x)))


def _dot(a, b):
    return jnp.dot(a, b, preferred_element_type=f32)


def _dot_nt(a, b):
    return lax.dot_general(a, b, (((1,), (1,)), ((), ())), preferred_element_type=f32)


def _dot_tn(a, b):
    return lax.dot_general(a, b, (((0,), (0,)), ((), ())), preferred_element_type=f32)


def _split3(x):
    hi = x.astype(bf16)
    r1 = x - hi.astype(f32)
    mid = r1.astype(bf16)
    lo = (r1 - mid.astype(f32)).astype(bf16)
    return hi, mid, lo


def _rope128(x, cos, sin, lane):
    return x * cos + pltpu.roll(x, 64, 1) * jnp.where(lane < 64, -sin, sin)


def _rope64(x, cos, sin, lane):
    low = (lane & 63) < 32
    return (x * cos + pltpu.roll(x, 32, 1) * jnp.where(low, 0.0, sin)
            + pltpu.roll(x, 96, 1) * jnp.where(low, -sin, 0.0))


def _rms_gain(x, g):
    return x * lax.rsqrt(jnp.mean(x * x, axis=-1, keepdims=True) + EPS) * g


def _inproj_kernel(l_ref, hmid_ref, y2_ref, g_ref, w_ref, ws_ref, h_ref, z_ref, zs_ref, xn_ref):
    j = pl.program_id(1)

    @pl.when(j == 0)
    def _():
        d = hmid_ref.shape[1]
        h = hmid_ref[...] + y2_ref[:, :d].astype(f32) + y2_ref[:, d:].astype(f32)
        h_ref[...] = h
        xn = _rms_gain(h, g_ref[...]).astype(bf16)
        xn_ref[...] = xn
        zs_ref[...] = _dot(xn, ws_ref[...])

    z_ref[...] = _dot(xn_ref[...], w_ref[...]).astype(bf16)


def _inproj(l, hmid, y2, gain, wcat, ws):
    n, d = hmid.shape
    tm = _row_tile(n, 512)
    tn = 512
    grid = (n // tm, N_CAT // tn)
    return pl.pallas_call(
        _inproj_kernel,
        grid_spec=pltpu.PrefetchScalarGridSpec(
            num_scalar_prefetch=1, grid=grid,
            in_specs=[
                pl.BlockSpec((tm, d), lambda i, j, l: (i, 0)),
                pl.BlockSpec((tm, 2 * d), lambda i, j, l: (i, 0)),
                pl.BlockSpec((None, 1, d), lambda i, j, l: (l[0], 0, 0)),
                pl.BlockSpec((None, d, tn), lambda i, j, l: (l[0], 0, j)),
                pl.BlockSpec((None, d, LANES), lambda i, j, l: (l[0], 0, 0)),
            ],
            out_specs=[
                pl.BlockSpec((tm, d), lambda i, j, l: (i, 0)),
                pl.BlockSpec((tm, tn), lambda i, j, l: (i, j)),
                pl.BlockSpec((tm, LANES), lambda i, j, l: (i, 0)),
            ],
            scratch_shapes=[pltpu.VMEM((tm, d), bf16)]),
        out_shape=[jax.ShapeDtypeStruct((n, d), f32),
                   jax.ShapeDtypeStruct((n, N_CAT), bf16),
                   jax.ShapeDtypeStruct((n, LANES), f32)],
        compiler_params=_cparams(("parallel", "arbitrary")),
        name="inproj",
    )(l, hmid, y2, gain, wcat, ws)


def _kprep_kernel(l_ref, ak_ref, av_ref, ik_ref, bk_ref, bv_ref, zs_ref,
                  cos_ref, sin_ref, icos_ref, isin_ref, gak_ref, gbk_ref, fb_ref,
                  akr_ref, ikr_ref, avt_ref, bkn_ref, bvt_ref, fkb_ref, carry_ref, *, t_real):
    k = pl.program_id(1)
    lane = lax.broadcasted_iota(i32, (BLK, LANES), 1)
    row = lax.broadcasted_iota(i32, (BLK, LANES), 0)
    pos = k * BLK + row
    valid = (pos >= FRONT) & (pos < FRONT + t_real)

    x = ak_ref[...].astype(f32)
    akr_ref[...] = _rope128(_rms_gain(x, gak_ref[...]), cos_ref[...], sin_ref[...], lane).astype(bf16)
    ikr_ref[...] = _rope64(ik_ref[...].astype(f32), icos_ref[...], isin_ref[...], lane).astype(bf16)
    avt_ref[...] = av_ref[...].astype(f32).T.astype(bf16)
    for h in range(B_HEADS):
        sl = slice(h * LANES, (h + 1) * LANES)
        bkn_ref[h] = _rms_gain(bk_ref[:, sl].astype(f32), gbk_ref[...]).astype(bf16)
        bvt_ref[h] = bv_ref[:, sl].astype(f32).T.astype(bf16)

    @pl.when(k == 0)
    def _():
        carry_ref[...] = jnp.zeros_like(carry_ref)

    lf = jnp.where(valid, _log_sigmoid(zs_ref[...] + fb_ref[...]), 0.0)
    tri = (row >= lane).astype(bf16)
    hi, mid, lo = _split3(lf)
    fcum = _dot(tri, hi) + _dot(tri, mid) + _dot(tri, lo) + carry_ref[...]
    carry_ref[...] = fcum[BLK - 1:BLK, :]
    for h in range(B_HEADS):
        fkb_ref[h] = jnp.broadcast_to(fcum[:, S_BF + h:S_BF + h + 1], (BLK, LANES))


def _kprep(l, zcat3, zs3, tabs, gak, gbk, fbias, t_real):
    bsz, tp, _ = zcat3.shape
    nkb = tp // BLK
    cos, sin, icos, isin = tabs
    tile = lambda c: pl.BlockSpec((None, BLK, LANES), lambda b, k, l, c=c: (b, k, c))
    wide = lambda c: pl.BlockSpec((None, BLK, B_WIDTH), lambda b, k, l, c=c: (b, k, c))
    tab = pl.BlockSpec((BLK, LANES), lambda b, k, l: (k, 0))
    gain = pl.BlockSpec((None, 1, LANES), lambda b, k, l: (l[0], 0, 0))
    return pl.pallas_call(
        functools.partial(_kprep_kernel, t_real=t_real),
        grid_spec=pltpu.PrefetchScalarGridSpec(
            num_scalar_prefetch=1, grid=(bsz, nkb),
            in_specs=[tile(T_AK), tile(T_AV), tile(T_IK),
                      wide(T_BK * LANES // B_WIDTH), wide(T_BV * LANES // B_WIDTH),
                      pl.BlockSpec((None, BLK, LANES), lambda b, k, l: (b, k, 0)),
                      tab, tab, tab, tab, gain, gain, gain],
            out_specs=[
                pl.BlockSpec((None, None, BLK, LANES), lambda b, k, l: (b, k, 0, 0)),
                pl.BlockSpec((None, None, BLK, LANES), lambda b, k, l: (b, k, 0, 0)),
                pl.BlockSpec((None, None, BLK, LANES), lambda b, k, l: (b, k, 0, 0)),
                pl.BlockSpec((None, B_HEADS, None, BLK, LANES), lambda b, k, l: (b, 0, k, 0, 0)),
                pl.BlockSpec((None, B_HEADS, None, BLK, LANES), lambda b, k, l: (b, 0, k, 0, 0)),
                pl.BlockSpec((None, B_HEADS, None, BLK, LANES), lambda b, k, l: (b, 0, k, 0, 0)),
            ],
            scratch_shapes=[pltpu.VMEM((1, LANES), f32)]),
        out_shape=[jax.ShapeDtypeStruct((bsz, nkb, BLK, LANES), bf16),
                   jax.ShapeDtypeStruct((bsz, nkb, BLK, LANES), bf16),
                   jax.ShapeDtypeStruct((bsz, nkb, BLK, LANES), bf16),
                   jax.ShapeDtypeStruct((bsz, B_HEADS, nkb, BLK, LANES), bf16),
                   jax.ShapeDtypeStruct((bsz, B_HEADS, nkb, BLK, LANES), bf16),
                   jax.ShapeDtypeStruct((bsz, B_HEADS, nkb, BLK, LANES), f32)],
        compiler_params=_cparams(("parallel", "arbitrary")),
        name="kprep",
    )(l, zcat3, zcat3, zcat3, zcat3, zcat3, zs3, cos, sin, icos, isin, gak, gbk, fbias)


def _dsa_kernel(l_ref, aq_ref, iq_ref, zs_ref, akr_ref, ikr_ref, avt_ref,
                cos_ref, sin_ref, icos_ref, isin_ref, gq_ref, out_ref,
                key_ref, iqs_ref, q6_ref, iwt_ref, m_ref, s_ref, acc_ref, *, t_real, k_top):
    i = pl.program_id(1)
    nk = i + 1
    lane = lax.broadcasted_iota(i32, (BLK, LANES), 1)
    row = lax.broadcasted_iota(i32, (BLK, LANES), 0)

    for h in range(A_HEADS):
        x = aq_ref[:, h * LANES:(h + 1) * LANES].astype(f32)
        xr = _rope128(_rms_gain(x, gq_ref[...]), cos_ref[...], sin_ref[...], lane)
        q6_ref[h * BLK:(h + 1) * BLK, :] = xr.astype(bf16)
    for t in range(IDX_HEADS // 2):
        x = iq_ref[:, t * LANES:(t + 1) * LANES].astype(f32)
        xr = _rope64(x, icos_ref[...], isin_ref[...], lane)
        iqs_ref[(2 * t) * BLK:(2 * t + 1) * BLK, :] = jnp.where(lane < IDX_DIM, xr, 0.0).astype(bf16)
        iqs_ref[(2 * t + 1) * BLK:(2 * t + 2) * BLK, :] = jnp.where(
            lane < IDX_DIM, pltpu.roll(xr, 64, 1), 0.0).astype(bf16)
    iwt_ref[...] = (zs_ref[...] * (IDX_HEADS ** -0.5 * IDX_DIM ** -0.5)).T

    qpos = i * BLK + lane

    def score_body(kb, carry):
        dt = _dot_nt(ikr_ref[kb], iqs_ref[...])
        s = jnp.zeros((BLK, LANES), f32)
        for h in range(IDX_HEADS):
            s = s + iwt_ref[h:h + 1, :] * jnp.maximum(dt[:, h * LANES:(h + 1) * LANES], 0.0)
        kpos = kb * BLK + row
        adm = ((kpos >> 6) <= (qpos >> 6)) & (kpos >= FRONT) & (kpos < FRONT + t_real)
        bits = lax.bitcast_convert_type(s, i32)
        key = bits ^ ((bits >> 31) & 0x7FFFFFFF)
        key_ref[kb] = jnp.where(adm, key, INT_MIN)
        return carry

    lax.fori_loop(0, nk, score_body, 0)

    def bit_body(t, thr_u):
        bit = jnp.left_shift(jnp.int32(1), 31 - t)
        cand_u = thr_u | bit
        cand_s = cand_u ^ INT_MIN

        def cnt_body(kb, c):
            return c + (key_ref[kb] >= cand_s).astype(i32)

        cnt = lax.fori_loop(0, nk, cnt_body, jnp.zeros((BLK, LANES), i32))
        tot = jnp.sum(cnt.astype(f32), axis=0, keepdims=True)
        return jnp.where(tot >= k_top, cand_u, thr_u)

    thr_u = lax.fori_loop(0, 32, bit_body, jnp.zeros((1, LANES), i32))
    thr_s = jnp.maximum(thr_u ^ INT_MIN, INT_MIN + 1)

    m_ref[...] = jnp.full(m_ref.shape, NEG, f32)
    s_ref[...] = jnp.zeros(s_ref.shape, f32)
    acc_ref[...] = jnp.zeros(acc_ref.shape, f32)
    scale = HEAD_DIM ** -0.5

    def attn_body(kb, carry):
        st = _dot_nt(akr_ref[kb], q6_ref[...]) * scale
        msk = key_ref[kb] >= thr_s
        ps = []
        for h in range(A_HEADS):
            sl = slice(h * LANES, (h + 1) * LANES)
            sh = jnp.where(msk, st[:, sl], NEG)
            m_old = m_ref[:, sl]
            m_new = jnp.maximum(m_old, jnp.max(sh, axis=0, keepdims=True))
            alpha = jnp.exp(m_old - m_new)
            p = jnp.where(msk, jnp.exp(sh - m_new), 0.0)
            s_ref[:, sl] = s_ref[:, sl] * alpha + jnp.sum(p, axis=0, keepdims=True)
            m_ref[:, sl] = m_new
            acc_ref[:, sl] = acc_ref[:, sl] * alpha
            ps.append(p.astype(bf16))
        acc_ref[...] += _dot(avt_ref[kb], jnp.concatenate(ps, axis=1))
        return carry

    lax.fori_loop(0, nk, attn_body, 0)

    for h in range(A_HEADS):
        sl = slice(h * LANES, (h + 1) * LANES)
        o = acc_ref[:, sl] / jnp.maximum(s_ref[:, sl], 1e-30)
        out_ref[:, sl] = o.T.astype(bf16)


def _dsa(l, zcat3, zs3, akr, ikr, avt, tabs, gaq, t_real, k_top):
    bsz, tp, _ = zcat3.shape
    nkb = tp // BLK
    cos, sin, icos, isin = tabs
    tab = pl.BlockSpec((BLK, LANES), lambda b, i, l: (i, 0))
    full = pl.BlockSpec((None, nkb, BLK, LANES), lambda b, i, l: (b, 0, 0, 0))
    return pl.pallas_call(
        functools.partial(_dsa_kernel, t_real=t_real, k_top=k_top),
        grid_spec=pltpu.PrefetchScalarGridSpec(
            num_scalar_prefetch=1, grid=(bsz, nkb),
            in_specs=[
                pl.BlockSpec((None, BLK, A_WIDTH), lambda b, i, l: (b, i, 0)),
                pl.BlockSpec((None, BLK, IDX_HEADS * IDX_DIM), lambda b, i, l: (b, i, T_IQ * LANES // (IDX_HEADS * IDX_DIM))),
                pl.BlockSpec((None, BLK, LANES), lambda b, i, l: (b, i, 0)),
                full, full, full, tab, tab, tab, tab,
                pl.BlockSpec((None, 1, LANES), lambda b, i, l: (l[0], 0, 0)),
            ],
            out_specs=pl.BlockSpec((None, BLK, A_WIDTH), lambda b, i, l: (b, i, 0)),
            scratch_shapes=[
                pltpu.VMEM((nkb, BLK, LANES), i32),
                pltpu.VMEM((IDX_HEADS * BLK, LANES), bf16),
                pltpu.VMEM((A_HEADS * BLK, LANES), bf16),
                pltpu.VMEM((LANES, BLK), f32),
                pltpu.VMEM((1, A_HEADS * BLK), f32),
                pltpu.VMEM((1, A_HEADS * BLK), f32),
                pltpu.VMEM((HEAD_DIM, A_HEADS * BLK), f32),
            ]),
        out_shape=jax.ShapeDtypeStruct((bsz, tp, A_WIDTH), bf16),
        compiler_params=_cparams(("parallel", "arbitrary")),
        name="dsa",
    )(l, zcat3, zcat3, zs3, akr, ikr, avt, cos, sin, icos, isin, gaq)


def _fattn_kernel(l_ref, bq_ref, bkn_ref, bvt_ref, fkb_ref, gq_ref, out_ref, *, t_real):
    i = pl.program_id(2)
    lane = lax.broadcasted_iota(i32, (BLK, LANES), 1)
    row = lax.broadcasted_iota(i32, (BLK, LANES), 0)
    qn = _rms_gain(bq_ref[...].astype(f32), gq_ref[...]).astype(bf16)
    qpos = i * BLK + lane
    scale = HEAD_DIM ** -0.5

    def body(kb, carry):
        m_old, ssum, acc = carry
        st = _dot_nt(bkn_ref[kb], qn) * scale - fkb_ref[kb]
        kpos = kb * BLK + row
        msk = (kpos <= qpos) & (kpos >= FRONT) & (kpos < FRONT + t_real)
        sh = jnp.where(msk, st, NEG)
        m_new = jnp.maximum(m_old, jnp.max(sh, axis=0, keepdims=True))
        alpha = jnp.exp(m_old - m_new)
        p = jnp.where(msk, jnp.exp(sh - m_new), 0.0)
        ssum = ssum * alpha + jnp.sum(p, axis=0, keepdims=True)
        acc = acc * alpha + _dot(bvt_ref[kb], p.astype(bf16))
        return m_new, ssum, acc

    init = (jnp.full((1, LANES), NEG, f32), jnp.zeros((1, LANES), f32), jnp.zeros((HEAD_DIM, BLK), f32))
    _, ssum, acc = lax.fori_loop(0, i + 1, body, init)
    out_ref[...] = (acc / jnp.maximum(ssum, 1e-30)).T.astype(bf16)


def _fattn(l, zcat3, bkn, bvt, fkb, gbq, t_real):
    bsz, tp, _ = zcat3.shape
    nkb = tp // BLK
    full = pl.BlockSpec((None, None, nkb, BLK, LANES), lambda b, h, i, l: (b, h, 0, 0, 0))
    return pl.pallas_call(
        functools.partial(_fattn_kernel, t_real=t_real),
        grid_spec=pltpu.PrefetchScalarGridSpec(
            num_scalar_prefetch=1, grid=(bsz, B_HEADS, nkb),
            in_specs=[
                pl.BlockSpec((None, BLK, LANES), lambda b, h, i, l: (b, i, T_BQ + h)),
                full, full, full,
                pl.BlockSpec((None, 1, LANES), lambda b, h, i, l: (l[0], 0, 0)),
            ],
            out_specs=pl.BlockSpec((None, BLK, LANES), lambda b, h, i, l: (b, i, h))),
        out_shape=jax.ShapeDtypeStruct((bsz, tp, B_WIDTH), bf16),
        compiler_params=_cparams(("parallel", "parallel", "arbitrary")),
        name="fattn",
    )(l, zcat3, bkn, bvt, fkb, gbq)


def _gla_kernel(l_ref, cq_ref, ck_ref, cv_ref, cg_ref, zs_ref, w2_ref, gb_ref, og_ref, e_ref,
                out_ref, st_ref, oi_ref, *, t_real):
    c = pl.program_id(0)
    nb, cc, dk = cq_ref.shape

    @pl.when(c == 0)
    def _():
        st_ref[...] = jnp.zeros_like(st_ref)

    rowc = lax.broadcasted_iota(i32, (cc, 1), 0)
    pos = c * cc + rowc
    valid = (pos >= FRONT) & (pos < FRONT + t_real)
    tri = (lax.broadcasted_iota(i32, (cc, cc), 0) >= lax.broadcasted_iota(i32, (cc, cc), 1)).astype(bf16)
    lane_k = lax.broadcasted_iota(i32, (1, dk), 1)
    head_masks = [(lane_k >= h * C_DK) & (lane_k < (h + 1) * C_DK) for h in range(C_HEADS)]

    for b in range(nb):
        x = _dot(zs_ref[b].astype(bf16), w2_ref[...]) + gb_ref[...]
        la = jnp.where(valid, _log_sigmoid(x) * (1.0 / C_TAU), 0.0)
        hi, mid, lo = _split3(la)
        bc = _dot(tri, hi) + _dot(tri, mid) + _dot(tri, lo)
        q = cq_ref[b].astype(f32) * (C_DK ** -0.5)
        k = ck_ref[b].astype(f32)
        v = cv_ref[b]
        vf = v.astype(f32)
        blast = bc[cc - 1:cc, :]
        qe = q * jnp.exp(bc)
        ke = k * jnp.exp(blast - bc)
        st = st_ref[b]
        stb = st.astype(bf16)

        o_inter = jnp.concatenate(
            [_dot_nt(jnp.where(head_masks[h], qe, 0.0).astype(bf16), stb) for h in range(C_HEADS)], axis=1)

        rows = []
        for r in range(cc):
            dec = jnp.exp(jnp.minimum(bc[r:r + 1, :] - bc, 0.0))
            rows.append(jnp.where(rowc <= r, q[r:r + 1, :] * k * dec, 0.0).astype(bf16))
        rr = _dot(jnp.concatenate(rows, axis=0), e_ref[...])
        for r in range(cc):
            oi_ref[r:r + 1, :] = jnp.sum(rr[r * cc:(r + 1) * cc, :] * vf, axis=0, keepdims=True)
        o = o_inter + oi_ref[...]

        new_st = st * jnp.exp(blast)
        for h in range(C_HEADS):
            km = jnp.where(head_masks[h], ke, 0.0).astype(bf16)
            new_st = new_st + _dot_tn(v[:, h * C_DV:(h + 1) * C_DV], km)
        st_ref[b] = new_st

        g = cg_ref[b].astype(f32)
        gs = g * (1.0 / (1.0 + jnp.exp(-g)))
        for h in range(C_HEADS):
            sl = slice(h * C_DV, (h + 1) * C_DV)
            out_ref[b, :, sl] = (_rms_gain(o[:, sl], og_ref[...]) * gs[:, sl]).astype(bf16)


def _gla(l, zcat3, zs3, w2p, gb, og, emat, t_real):
    bsz, tp, _ = zcat3.shape
    cc = GLA_C
    dk = C_HEADS * C_DK
    return pl.pallas_call(
        functools.partial(_gla_kernel, t_real=t_real),
        grid_spec=pltpu.PrefetchScalarGridSpec(
            num_scalar_prefetch=1, grid=(tp // cc,),
            in_specs=[
                pl.BlockSpec((bsz, cc, dk), lambda c, l: (0, c, T_CQ * LANES // dk)),
                pl.BlockSpec((bsz, cc, dk), lambda c, l: (0, c, T_CK * LANES // dk)),
                pl.BlockSpec((bsz, cc, C_WIDTH), lambda c, l: (0, c, T_CV * LANES // C_WIDTH)),
                pl.BlockSpec((bsz, cc, C_WIDTH), lambda c, l: (0, c, T_CG * LANES // C_WIDTH)),
                pl.BlockSpec((bsz, cc, LANES), lambda c, l: (0, c, 0)),
                pl.BlockSpec((None, LANES, dk), lambda c, l: (l[0], 0, 0)),
                pl.BlockSpec((None, 1, dk), lambda c, l: (l[0], 0, 0)),
                pl.BlockSpec((None, 1, C_DV), lambda c, l: (l[0], 0, 0)),
                pl.BlockSpec((dk, C_WIDTH), lambda c, l: (0, 0)),
            ],
            out_specs=pl.BlockSpec((bsz, cc, C_WIDTH), lambda c, l: (0, c, 0)),
            scratch_shapes=[pltpu.VMEM((bsz, C_DV, dk), f32), pltpu.VMEM((cc, C_WIDTH), f32)]),
        out_shape=jax.ShapeDtypeStruct((bsz, tp, C_WIDTH), bf16),
        compiler_params=_cparams(("arbitrary",)),
        name="gla",
    )(l, zcat3, zcat3, zcat3, zcat3, zs3, w2p, gb, og, emat)


def _outproj_kernel(l_ref, oa_ref, ob_ref, oc_ref, h_ref, wo_ref, g_ref, wr_ref, br_ref,
                    hmid_ref, u_ref, eid_ref, ew_ref, *, t_real):
    tm = h_ref.shape[0]
    mix = (_dot(oa_ref[...], wo_ref[0:A_WIDTH, :])
           + _dot(ob_ref[...], wo_ref[A_WIDTH:A_WIDTH + B_WIDTH, :])
           + _dot(oc_ref[...], wo_ref[A_WIDTH + B_WIDTH:, :]))
    pos = pl.program_id(1) * tm + lax.broadcasted_iota(i32, (tm, 1), 0)
    valid = (pos >= FRONT) & (pos < FRONT + t_real)
    hm = h_ref[...] + jnp.where(valid, mix, 0.0)
    hmid_ref[...] = hm
    u = _rms_gain(hm, g_ref[...]).astype(bf16)
    u_ref[...] = u

    logits = _dot(u, wr_ref[...]) + br_ref[...]
    lane = lax.broadcasted_iota(i32, (tm, LANES), 1)
    lanef = lane.astype(f32)
    big = float(4 * LANES)
    first = lambda hit: jnp.min(jnp.where(hit, lanef, big), axis=-1, keepdims=True).astype(i32)
    gl = jnp.where(lane < R_GRP + N_GROUPS, logits, -jnp.inf)
    gmax = jnp.max(gl, axis=-1, keepdims=True)
    g_p = 1.0 / jnp.sum(jnp.exp(gl - gmax), axis=-1, keepdims=True)
    g_i = first(gl == gmax)
    e_lane = lane - R_EXP
    emask = (e_lane >= 0) & (e_lane < N_EXPERTS) & ((e_lane >> 3) == g_i)
    el = jnp.where(emask, logits, -jnp.inf)
    m1 = jnp.max(el, axis=-1, keepdims=True)
    i1 = first(el == m1)
    el2 = jnp.where(lane == i1, -jnp.inf, el)
    m2 = jnp.max(el2, axis=-1, keepdims=True)
    i2 = first(el2 == m2)
    r = jnp.exp(m2 - m1)
    w1 = g_p / (1.0 + r)
    w2 = g_p * r / (1.0 + r)
    eid_ref[...] = jnp.where(lane == 0, i1 - R_EXP, jnp.where(lane == 1, i2 - R_EXP, 0))
    ew_ref[...] = jnp.where(lane == 0, w1, jnp.where(lane == 1, w2, 0.0))


def _seq_tile(tp, cap):
    return max(t for t in range(BLK, cap + 1, BLK) if tp % t == 0)


def _outproj(l, oa, ob, oc, h, wo, gain, wr, br, t_real, tp):
    n, d = h.shape
    tm = _seq_tile(tp, 384)
    nj = tp // tm
    row = lambda w: pl.BlockSpec((tm, w), lambda b, j, l: (b * nj + j, 0))
    return pl.pallas_call(
        functools.partial(_outproj_kernel, t_real=t_real),
        grid_spec=pltpu.PrefetchScalarGridSpec(
            num_scalar_prefetch=1, grid=(n // tp, nj),
            in_specs=[row(A_WIDTH), row(B_WIDTH), row(C_WIDTH), row(d),
                      pl.BlockSpec((None, d, d), lambda b, j, l: (l[0], 0, 0)),
                      pl.BlockSpec((None, 1, d), lambda b, j, l: (l[0], 0, 0)),
                      pl.BlockSpec((None, d, LANES), lambda b, j, l: (l[0], 0, 0)),
                      pl.BlockSpec((None, 1, LANES), lambda b, j, l: (l[0], 0, 0))],
            out_specs=[row(d), row(d), row(LANES), row(LANES)]),
        out_shape=[jax.ShapeDtypeStruct((n, d), f32), jax.ShapeDtypeStruct((n, d), bf16),
                   jax.ShapeDtypeStruct((n, LANES), i32), jax.ShapeDtypeStruct((n, LANES), f32)],
        compiler_params=_cparams(("parallel", "parallel")),
        name="outproj",
    )(l, oa, ob, oc, h, wo, gain, wr, br)


def _moe_kernel(l_ref, te_ref, na_ref, xs_ref, w1_ref, w3_ref, w2_ref, rw_ref, ys_ref):
    i = pl.program_id(0)

    @pl.when(i < na_ref[0])
    def _():
        x = xs_ref[...]
        h1 = _dot(x, w1_ref[...])
        h3 = _dot(x, w3_ref[...])
        hid = (h1 * (1.0 / (1.0 + jnp.exp(-h1))) * h3).astype(bf16)
        ys_ref[...] = (_dot(hid, w2_ref[...]) * rw_ref[...]).astype(bf16)

    @pl.when(i >= na_ref[0])
    def _():
        ys_ref[...] = jnp.zeros_like(ys_ref)


def _moe(l, te, na, xs, w1, w3, w2, rw):
    p, d = xs.shape
    tm = MOE_TM
    return pl.pallas_call(
        _moe_kernel,
        grid_spec=pltpu.PrefetchScalarGridSpec(
            num_scalar_prefetch=3, grid=(p // tm,),
            in_specs=[
                pl.BlockSpec((tm, d), lambda i, l, te, na: (i, 0)),
                pl.BlockSpec((None, None, d, D_EXPERT), lambda i, l, te, na: (l[0], te[i], 0, 0)),
                pl.BlockSpec((None, None, d, D_EXPERT), lambda i, l, te, na: (l[0], te[i], 0, 0)),
                pl.BlockSpec((None, None, D_EXPERT, d), lambda i, l, te, na: (l[0], te[i], 0, 0)),
                pl.BlockSpec((tm, 1), lambda i, l, te, na: (i, 0)),
            ],
            out_specs=pl.BlockSpec((tm, d), lambda i, l, te, na: (i, 0))),
        out_shape=jax.ShapeDtypeStruct((p, d), bf16),
        compiler_params=_cparams(("arbitrary",)),
        name="moe",
    )(l, te, na, xs, w1, w3, w2, rw)


def _route(eid, ew, n_rows_pad):
    n = eid.shape[0]
    tm = MOE_TM
    e_flat = eid[:, :EXP_TOPK_CONST].reshape(-1)
    w_flat = ew[:, :EXP_TOPK_CONST].reshape(-1)
    na_all = e_flat.shape[0]
    order = jnp.argsort(e_flat, stable=True).astype(i32)
    counts = jnp.sum((e_flat[:, None] == jnp.arange(N_EXPERTS, dtype=i32)[None, :]).astype(i32), axis=0)
    pc = ((counts + tm - 1) // tm) * tm
    pend = jnp.cumsum(pc)
    po = pend - pc
    co = jnp.cumsum(counts) - counts
    r = jnp.arange(n_rows_pad, dtype=i32)
    e_r = jnp.minimum(jnp.searchsorted(pend, r, side="right").astype(i32), N_EXPERTS - 1)
    local = r - po[e_r]
    valid_r = (local < counts[e_r]) & (r < pend[-1])
    a_r = order[jnp.clip(co[e_r] + local, 0, na_all - 1)]
    tok_r = jnp.where(valid_r, a_r // EXP_TOPK_CONST, 0)
    w_r = jnp.where(valid_r, w_flat[a_r], 0.0)
    n_act = (pend[-1] // tm).astype(i32)
    tile_e = e_r[::tm]
    tidx = jnp.arange(n_rows_pad // tm, dtype=i32)
    te = jnp.where(tidx < n_act, tile_e, tile_e[jnp.maximum(n_act - 1, 0)])
    inv = jnp.argsort(order).astype(i32)
    pos_a = po[e_flat] + inv - co[e_flat]
    return tok_r, w_r, te, n_act.reshape(1), pos_a.reshape(n, EXP_TOPK_CONST)


def _final_kernel(hmid_ref, y2_ref, out_ref):
    d = hmid_ref.shape[1]
    out_ref[...] = hmid_ref[...] + y2_ref[:, :d].astype(f32) + y2_ref[:, d:].astype(f32)


def _final(hmid, y2):
    n, d = hmid.shape
    tm = _row_tile(n, 512)
    return pl.pallas_call(
        _final_kernel, grid=(n // tm,),
        in_specs=[pl.BlockSpec((tm, d), lambda i: (i, 0)), pl.BlockSpec((tm, 2 * d), lambda i: (i, 0))],
        out_specs=pl.BlockSpec((tm, d), lambda i: (i, 0)),
        out_shape=jax.ShapeDtypeStruct((n, d), f32),
        compiler_params=_cparams(("parallel",)),
        name="final",
    )(hmid, y2)


def _rope_tables(tp):
    pos = (jnp.arange(tp, dtype=f32) - FRONT)[:, None]

    def tab(dim, reps):
        inv = 1.0 / (ROPE_THETA ** (jnp.arange(0, dim, 2, dtype=f32) / dim))
        ang = pos * inv[None, :]
        return jnp.tile(jnp.cos(ang), (1, reps)), jnp.tile(jnp.sin(ang), (1, reps))

    cos, sin = tab(HEAD_DIM, 2)
    icos, isin = tab(IDX_DIM, 4)
    return cos, sin, icos, isin


def _split_w_in(w_in):
    parts, off = [], 0
    for s in IN_SPLITS:
        parts.append(w_in[..., off:off + s])
        off += s
    return parts


def kernel(x, meta, norm_mix_g, norm_ffn_g, w_in, a_qn_g, a_kn_g, b_qn_g, b_kn_g, b_f_bias,
           c_gate_w2, c_gate_b, c_on_g, w_out, r_group_w, r_group_b, r_exp_w, r_exp_b,
           e_w1, e_w3, e_w2):
    bsz, n_seq, d = x.shape
    depth = w_in.shape[0]
    t_real = n_seq + N_META
    k_top = min(TOPK_MAX, n_seq // 4)
    tp = -(-(FRONT + t_real) // BLK) * BLK
    n = bsz * tp

    (waq, wak, wav, wiq, wik, wiw, wbq, wbk, wbv, wbf, wcq, wck, wcv, wcr, wcg) = _split_w_in(w_in)
    zc = lambda w: jnp.zeros((depth, d, w), w_in.dtype)
    wcat = jnp.concatenate([waq, wak, wav, wiq, wik, zc(LANES - IDX_DIM), zc(LANES),
                            wbq, wbk, wbv, wcq, wck, wcv, wcg], axis=-1).astype(bf16)
    ws = jnp.concatenate([wiw, wbf, wcr, zc(LANES - S_CR - C_GATE_RANK)], axis=-1).astype(bf16)
    wo = w_out.astype(bf16)
    wr = jnp.concatenate([r_group_w, zc(R_EXP - N_GROUPS), r_exp_w, zc(LANES - R_EXP - N_EXPERTS)],
                         axis=-1).astype(bf16)
    zl = lambda w: jnp.zeros((depth, w), f32)
    br = jnp.concatenate([r_group_b, zl(R_EXP - N_GROUPS), r_exp_b, zl(LANES - R_EXP - N_EXPERTS)],
                         axis=-1)[:, None, :]
    fbias = jnp.concatenate([zl(S_BF), b_f_bias, zl(LANES - S_BF - B_HEADS)], axis=-1)[:, None, :]
    dk = C_HEADS * C_DK
    w2p = jnp.concatenate([jnp.zeros((depth, S_CR, dk), f32), c_gate_w2,
                           jnp.zeros((depth, LANES - S_CR - C_GATE_RANK, dk), f32)], axis=1).astype(bf16)
    gb = c_gate_b[:, None, :]
    og = c_on_g[:, None, :]
    g_mix = norm_mix_g[:, None, :]
    g_ffn = norm_ffn_g[:, None, :]
    gaq, gak, gbq, gbk = (g[:, None, :] for g in (a_qn_g, a_kn_g, b_qn_g, b_kn_g))
    ew1, ew3, ew2 = e_w1.astype(bf16), e_w3.astype(bf16), e_w2.astype(bf16)
    emat = (jnp.arange(dk, dtype=i32)[:, None] // C_DK
            == jnp.arange(C_WIDTH, dtype=i32)[None, :] // C_DV).astype(bf16)
    tabs = _rope_tables(tp)

    h0 = jnp.concatenate([
        jnp.zeros((bsz, FRONT, d), f32),
        jnp.broadcast_to(meta[None].astype(f32), (bsz, N_META, d)),
        x.astype(f32),
        jnp.zeros((bsz, tp - FRONT - t_real, d), f32)], axis=1).reshape(n, d)
    y0 = jnp.zeros((n, 2 * d), bf16)
    n_rows_pad = -(-(EXP_TOPK_CONST * n + N_EXPERTS * (MOE_TM - 1)) // MOE_TM) * MOE_TM

    def layer(li, carry):
        hmid, y2 = carry
        l = jnp.reshape(li, (1,)).astype(i32)
        h, zcat, zs = _inproj(l, hmid, y2, g_mix, wcat, ws)
        zcat3 = zcat.reshape(bsz, tp, N_CAT)
        zs3 = zs.reshape(bsz, tp, LANES)
        akr, ikr, avt, bkn, bvt, fkb = _kprep(l, zcat3, zs3, tabs, gak, gbk, fbias, t_real)
        oa = _dsa(l, zcat3, zs3, akr, ikr, avt, tabs, gaq, t_real, k_top)
        ob = _fattn(l, zcat3, bkn, bvt, fkb, gbq, t_real)
        oc = _gla(l, zcat3, zs3, w2p, gb, og, emat, t_real)
        hmid2, u, eid, ew = _outproj(l, oa.reshape(n, A_WIDTH), ob.reshape(n, B_WIDTH),
                                     oc.reshape(n, C_WIDTH), h, wo, g_ffn, wr, br, t_real, tp)
        tok_r, w_r, te, n_act, pos_a = _route(eid, ew, n_rows_pad)
        xs = jnp.take(u, tok_r, axis=0)
        ys = _moe(l, te, n_act, xs, ew1, ew3, ew2, w_r[:, None])
        y2n = jnp.take(ys, pos_a, axis=0).reshape(n, 2 * d)
        return hmid2, y2n

    hmid, y2 = lax.fori_loop(0, depth, layer, (h0, y0))
    out = _final(hmid, y2).reshape(bsz, tp, d)
    return out[:, FRONT + N_META:FRONT + t_real].astype(x.dtype)
```

```python
import functools

import jax
import jax.numpy as jnp
from jax import lax
from jax.experimental import pallas as pl
from jax.experimental.pallas import tpu as pltpu

f32 = jnp.float32
bf16 = jnp.bfloat16
i32 = jnp.int32

D_MODEL = 2048
CHUNK = 64
N_META = 16
ROPE_THETA = 10000.0
EPS = 1e-6
HEAD_DIM = 128
A_HEADS = 6
IDX_HEADS = 16
IDX_DIM = 64
TOPK_MAX = 256
B_HEADS = 6
C_HEADS = 4
C_DK = 64
C_DV = 128
C_GATE_RANK = 16
C_TAU = 16.0
N_GROUPS = 4
EXP_PER_GROUP = 8
N_EXPERTS = N_GROUPS * EXP_PER_GROUP
D_EXPERT = 512
A_WIDTH = A_HEADS * HEAD_DIM
B_WIDTH = B_HEADS * HEAD_DIM
C_WIDTH = C_HEADS * C_DV
IN_SPLITS = (A_WIDTH, HEAD_DIM, HEAD_DIM, IDX_HEADS * IDX_DIM, IDX_DIM, IDX_HEADS,
             B_WIDTH, B_WIDTH, B_WIDTH, B_HEADS,
             C_HEADS * C_DK, C_HEADS * C_DK, C_WIDTH, C_GATE_RANK, C_WIDTH)

LANES = 128
FRONT = (-N_META) % CHUNK
BLK = 128
T_AQ, T_AK, T_AV, T_IQ, T_IK = 0, 6, 7, 8, 16
T_BQ, T_BK, T_BV = 18, 24, 30
T_CQ, T_CK, T_CV, T_CG = 36, 38, 40, 44
N_CAT = 48 * LANES
S_IW, S_BF, S_CR = 0, 16, 22
R_GRP, R_EXP = 0, 32
GLA_C = 32
DSA_G = 3
MOE_TM = 256
EXP_TOPK_CONST = 2
NEG = -1e30
INT_MIN = -2 ** 31
VMEM_LIMIT = 56 * 1024 * 1024


def _cparams(sem):
    return pltpu.CompilerParams(dimension_semantics=sem, vmem_limit_bytes=VMEM_LIMIT)


def _row_tile(n, cap):
    t = cap
    while n % t:
        t //= 2
    return t


def _log_sigmoid(x):
    return jnp.minimum(x, 0.0) - jnp.log(1.0 + jnp.exp(-jnp.abs(x)))


def _dot(a, b):
    return jnp.dot(a, b, preferred_element_type=f32)


def _dot_nt(a, b):
    return lax.dot_general(a, b, (((1,), (1,)), ((), ())), preferred_element_type=f32)


def _dot_tn(a, b):
    return lax.dot_general(a, b, (((0,), (0,)), ((), ())), preferred_element_type=f32)


def _split3(x):
    hi = x.astype(bf16)
    r1 = x - hi.astype(f32)
    mid = r1.astype(bf16)
    lo = (r1 - mid.astype(f32)).astype(bf16)
    return hi, mid, lo


def _rope128(x, cos, sin, lane):
    return x * cos + pltpu.roll(x, 64, 1) * jnp.where(lane < 64, -sin, sin)


def _rope64(x, cos, sin, lane):
    low = (lane & 63) < 32
    return (x * cos + pltpu.roll(x, 32, 1) * jnp.where(low, 0.0, sin)
            + pltpu.roll(x, 96, 1) * jnp.where(low, -sin, 0.0))


def _rms_gain(x, g):
    return x * lax.rsqrt(jnp.mean(x * x, axis=-1, keepdims=True) + EPS) * g


def _inproj_kernel(l_ref, hmid_ref, y2_ref, g_ref, w_ref, ws_ref, h_ref, z_ref, zs_ref, xn_ref):
    j = pl.program_id(1)

    @pl.when(j == 0)
    def _():
        d = hmid_ref.shape[1]
        h = hmid_ref[...] + y2_ref[:, :d].astype(f32) + y2_ref[:, d:].astype(f32)
        h_ref[...] = h
        xn = _rms_gain(h, g_ref[...]).astype(bf16)
        xn_ref[...] = xn
        zs_ref[...] = _dot(xn, ws_ref[...])

    z_ref[...] = _dot(xn_ref[...], w_ref[...]).astype(bf16)


def _inproj(l, hmid, y2, gain, wcat, ws):
    n, d = hmid.shape
    tm = _row_tile(n, 512)
    tn = 512
    grid = (n // tm, N_CAT // tn)
    return pl.pallas_call(
        _inproj_kernel,
        grid_spec=pltpu.PrefetchScalarGridSpec(
            num_scalar_prefetch=1, grid=grid,
            in_specs=[
                pl.BlockSpec((tm, d), lambda i, j, l: (i, 0)),
                pl.BlockSpec((tm, 2 * d), lambda i, j, l: (i, 0)),
                pl.BlockSpec((None, 1, d), lambda i, j, l: (l[0], 0, 0)),
                pl.BlockSpec((None, d, tn), lambda i, j, l: (l[0], 0, j)),
                pl.BlockSpec((None, d, LANES), lambda i, j, l: (l[0], 0, 0)),
            ],
            out_specs=[
                pl.BlockSpec((tm, d), lambda i, j, l: (i, 0)),
                pl.BlockSpec((tm, tn), lambda i, j, l: (i, j)),
                pl.BlockSpec((tm, LANES), lambda i, j, l: (i, 0)),
            ],
            scratch_shapes=[pltpu.VMEM((tm, d), bf16)]),
        out_shape=[jax.ShapeDtypeStruct((n, d), f32),
                   jax.ShapeDtypeStruct((n, N_CAT), bf16),
                   jax.ShapeDtypeStruct((n, LANES), f32)],
        compiler_params=_cparams(("parallel", "arbitrary")),
        name="inproj",
    )(l, hmid, y2, gain, wcat, ws)


def _kprep_kernel(l_ref, ak_ref, ik_ref, bk_ref, zs_ref,
                  cos_ref, sin_ref, icos_ref, isin_ref, gak_ref, gbk_ref, fb_ref,
                  akr_ref, ikr_ref, bka_ref, carry_ref, *, t_real):
    k = pl.program_id(1)
    lane = lax.broadcasted_iota(i32, (BLK, LANES), 1)
    row = lax.broadcasted_iota(i32, (BLK, LANES), 0)
    pos = k * BLK + row
    valid = (pos >= FRONT) & (pos < FRONT + t_real)

    x = ak_ref[...].astype(f32)
    akr_ref[...] = _rope128(_rms_gain(x, gak_ref[...]), cos_ref[...], sin_ref[...], lane).astype(bf16)
    ikr_ref[...] = _rope64(ik_ref[...].astype(f32), icos_ref[...], isin_ref[...], lane).astype(bf16)

    @pl.when(k == 0)
    def _():
        carry_ref[...] = jnp.zeros_like(carry_ref)

    lf = jnp.where(valid, _log_sigmoid(zs_ref[...] + fb_ref[...]), 0.0)
    tri = (row >= lane).astype(bf16)
    hi, mid, lo = _split3(lf)
    fcum = _dot(tri, hi) + _dot(tri, mid) + _dot(tri, lo) + carry_ref[...]
    carry_ref[...] = fcum[BLK - 1:BLK, :]

    fs = jnp.where(valid, fcum * (-(HEAD_DIM ** 0.5)), NEG)
    for h in range(B_HEADS):
        sl = slice(h * LANES, (h + 1) * LANES)
        bka_ref[h, :, 0:LANES] = _rms_gain(bk_ref[:, sl].astype(f32), gbk_ref[...]).astype(bf16)
        p0, p1, p2 = _split3(fs[:, S_BF + h:S_BF + h + 1])
        aug = jnp.where(lane == 0, p0.astype(f32),
                        jnp.where(lane == 1, p1.astype(f32), jnp.where(lane == 2, p2.astype(f32), 0.0)))
        bka_ref[h, :, LANES:2 * LANES] = aug.astype(bf16)


def _kprep(l, zcat3, zs3, tabs, gak, gbk, fbias, t_real):
    bsz, tp, _ = zcat3.shape
    nkb = tp // BLK
    cos, sin, icos, isin = tabs
    tile = lambda c: pl.BlockSpec((None, BLK, LANES), lambda b, k, l, c=c: (b, k, c))
    wide = lambda c: pl.BlockSpec((None, BLK, B_WIDTH), lambda b, k, l, c=c: (b, k, c))
    tab = pl.BlockSpec((BLK, LANES), lambda b, k, l: (k, 0))
    gain = pl.BlockSpec((None, 1, LANES), lambda b, k, l: (l[0], 0, 0))
    return pl.pallas_call(
        functools.partial(_kprep_kernel, t_real=t_real),
        grid_spec=pltpu.PrefetchScalarGridSpec(
            num_scalar_prefetch=1, grid=(bsz, nkb),
            in_specs=[tile(T_AK), tile(T_IK),
                      wide(T_BK * LANES // B_WIDTH),
                      pl.BlockSpec((None, BLK, LANES), lambda b, k, l: (b, k, 0)),
                      tab, tab, tab, tab, gain, gain, gain],
            out_specs=[
                pl.BlockSpec((None, BLK, LANES), lambda b, k, l: (b, k, 0)),
                pl.BlockSpec((None, BLK, LANES), lambda b, k, l: (b, k, 0)),
                pl.BlockSpec((None, B_HEADS, BLK, 2 * LANES), lambda b, k, l: (b, 0, k, 0)),
            ],
            scratch_shapes=[pltpu.VMEM((1, LANES), f32)]),
        out_shape=[jax.ShapeDtypeStruct((bsz, tp, LANES), bf16),
                   jax.ShapeDtypeStruct((bsz, tp, LANES), bf16),
                   jax.ShapeDtypeStruct((bsz, B_HEADS, tp, 2 * LANES), bf16)],
        compiler_params=_cparams(("parallel", "arbitrary")),
        name="kprep",
    )(l, zcat3, zcat3, zcat3, zs3, cos, sin, icos, isin, gak, gbk, fbias)


def _dsa_kernel(l_ref, aq_ref, iq_ref, zs_ref, akr_ref, ikr_ref, av_ref,
                cos_ref, sin_ref, icos_ref, isin_ref, gq_ref, out_ref,
                key_ref, iqs_ref, q6_ref, iwt_ref, m_ref, s_ref, acc_ref, *, t_real, k_top):
    i = pl.program_id(1)
    nk = i + 1
    lane = lax.broadcasted_iota(i32, (BLK, LANES), 1)

    for h in range(A_HEADS):
        x = aq_ref[:, h * LANES:(h + 1) * LANES].astype(f32)
        xr = _rope128(_rms_gain(x, gq_ref[...]), cos_ref[...], sin_ref[...], lane)
        q6_ref[h * BLK:(h + 1) * BLK, :] = xr.astype(bf16)
    for t in range(IDX_HEADS // 2):
        x = iq_ref[:, t * LANES:(t + 1) * LANES].astype(f32)
        xr = _rope64(x, icos_ref[...], isin_ref[...], lane)
        iqs_ref[(2 * t) * BLK:(2 * t + 1) * BLK, :] = jnp.where(lane < IDX_DIM, xr, 0.0).astype(bf16)
        iqs_ref[(2 * t + 1) * BLK:(2 * t + 2) * BLK, :] = jnp.where(
            lane < IDX_DIM, pltpu.roll(xr, 64, 1), 0.0).astype(bf16)
    iwt_ref[...] = (zs_ref[...] * (IDX_HEADS ** -0.5 * IDX_DIM ** -0.5)).T

    gb = DSA_G * BLK
    ng = lax.div(nk, jnp.int32(DSA_G))
    nr = nk - ng * DSA_G

    def over_keys(fn, init):
        c = lax.fori_loop(0, ng, lambda g, c: fn(pl.multiple_of(g * gb, gb), gb, c), init)
        return lax.fori_loop(0, nr, lambda r, c: fn(pl.multiple_of((ng * DSA_G + r) * BLK, BLK), BLK, c), c)

    def score_rows(k0, nrows, carry):
        dt = _dot_nt(ikr_ref[pl.ds(k0, nrows), :], iqs_ref[...])
        s = jnp.zeros((nrows, LANES), f32)
        for h in range(IDX_HEADS):
            s = s + iwt_ref[h:h + 1, :] * jnp.maximum(dt[:, h * LANES:(h + 1) * LANES], 0.0)
        kpos = k0 + lax.broadcasted_iota(i32, (nrows, LANES), 0)
        qpos = i * BLK + lax.broadcasted_iota(i32, (nrows, LANES), 1)
        adm = ((kpos >> 6) <= (qpos >> 6)) & (kpos >= FRONT) & (kpos < FRONT + t_real)
        bits = lax.bitcast_convert_type(s, i32)
        key = bits ^ ((bits >> 31) & 0x7FFFFFFF)
        key_ref[pl.ds(k0, nrows), :] = jnp.where(adm, key, INT_MIN)
        return carry

    over_keys(score_rows, 0)

    def bit_body(t, thr_u):
        bit = jnp.left_shift(jnp.int32(1), 31 - t)
        cand_u = thr_u | bit
        cand_s = cand_u ^ INT_MIN

        def count_rows(k0, nrows, c):
            hit = (key_ref[pl.ds(k0, nrows), :] >= cand_s).astype(i32)
            for j in range(nrows // BLK):
                c = c + hit[j * BLK:(j + 1) * BLK, :]
            return c

        cnt = over_keys(count_rows, jnp.zeros((BLK, LANES), i32))
        tot = jnp.sum(cnt.astype(f32), axis=0, keepdims=True)
        return jnp.where(tot >= k_top, cand_u, thr_u)

    thr_u = lax.fori_loop(0, 32, bit_body, jnp.zeros((1, LANES), i32))
    thr_s = jnp.maximum(thr_u ^ INT_MIN, INT_MIN + 1)

    m_ref[...] = jnp.full(m_ref.shape, NEG, f32)
    s_ref[...] = jnp.zeros(s_ref.shape, f32)
    acc_ref[...] = jnp.zeros(acc_ref.shape, f32)
    scale = HEAD_DIM ** -0.5

    def attn_rows(k0, nrows, carry):
        st = _dot_nt(akr_ref[pl.ds(k0, nrows), :], q6_ref[...]) * scale
        msk = key_ref[pl.ds(k0, nrows), :] >= thr_s
        ps = []
        for h in range(A_HEADS):
            sl = slice(h * LANES, (h + 1) * LANES)
            sh = jnp.where(msk, st[:, sl], NEG)
            m_old = m_ref[:, sl]
            m_new = jnp.maximum(m_old, jnp.max(sh, axis=0, keepdims=True))
            alpha = jnp.exp(m_old - m_new)
            p = jnp.where(msk, jnp.exp(sh - m_new), 0.0)
            s_ref[:, sl] = s_ref[:, sl] * alpha + jnp.sum(p, axis=0, keepdims=True)
            m_ref[:, sl] = m_new
            acc_ref[:, sl] = acc_ref[:, sl] * alpha
            ps.append(p.astype(bf16))
        acc_ref[...] += _dot_tn(av_ref[pl.ds(k0, nrows), :], jnp.concatenate(ps, axis=1))
        return carry

    over_keys(attn_rows, 0)

    for h in range(A_HEADS):
        sl = slice(h * LANES, (h + 1) * LANES)
        o = acc_ref[:, sl] / jnp.maximum(s_ref[:, sl], 1e-30)
        out_ref[:, sl] = o.T.astype(bf16)


def _dsa(l, zcat3, zs3, akr, ikr, tabs, gaq, t_real, k_top):
    bsz, tp, _ = zcat3.shape
    nkb = tp // BLK
    cos, sin, icos, isin = tabs
    tab = pl.BlockSpec((BLK, LANES), lambda b, i, l: (i, 0))
    full = pl.BlockSpec((None, tp, LANES), lambda b, i, l: (b, 0, 0))
    return pl.pallas_call(
        functools.partial(_dsa_kernel, t_real=t_real, k_top=k_top),
        grid_spec=pltpu.PrefetchScalarGridSpec(
            num_scalar_prefetch=1, grid=(bsz, nkb),
            in_specs=[
                pl.BlockSpec((None, BLK, A_WIDTH), lambda b, i, l: (b, i, 0)),
                pl.BlockSpec((None, BLK, IDX_HEADS * IDX_DIM), lambda b, i, l: (b, i, T_IQ * LANES // (IDX_HEADS * IDX_DIM))),
                pl.BlockSpec((None, BLK, LANES), lambda b, i, l: (b, i, 0)),
                full, full,
                pl.BlockSpec((None, tp, LANES), lambda b, i, l: (b, 0, T_AV)),
                tab, tab, tab, tab,
                pl.BlockSpec((None, 1, LANES), lambda b, i, l: (l[0], 0, 0)),
            ],
            out_specs=pl.BlockSpec((None, BLK, A_WIDTH), lambda b, i, l: (b, i, 0)),
            scratch_shapes=[
                pltpu.VMEM((tp, LANES), i32),
                pltpu.VMEM((IDX_HEADS * BLK, LANES), bf16),
                pltpu.VMEM((A_HEADS * BLK, LANES), bf16),
                pltpu.VMEM((LANES, BLK), f32),
                pltpu.VMEM((1, A_HEADS * BLK), f32),
                pltpu.VMEM((1, A_HEADS * BLK), f32),
                pltpu.VMEM((HEAD_DIM, A_HEADS * BLK), f32),
            ]),
        out_shape=jax.ShapeDtypeStruct((bsz, tp, A_WIDTH), bf16),
        compiler_params=_cparams(("parallel", "arbitrary")),
        name="dsa",
    )(l, zcat3, zcat3, zs3, akr, ikr, zcat3, cos, sin, icos, isin, gaq)


def _fattn_kernel(l_ref, bq_ref, bka_ref, bv_ref, gq_ref, out_ref, qa_ref, m_ref, s_ref, acc_ref):
    i = pl.program_id(1)
    fb = bq_ref.shape[0]
    ones3 = jnp.where(lax.broadcasted_iota(i32, (fb, LANES), 1) < 3, 1.0, 0.0).astype(bf16)
    for h in range(B_HEADS):
        sl = slice(h * LANES, (h + 1) * LANES)
        qa_ref[h, :, 0:LANES] = _rms_gain(bq_ref[:, sl].astype(f32), gq_ref[...]).astype(bf16)
        qa_ref[h, :, LANES:2 * LANES] = ones3
    m_ref[...] = jnp.full(m_ref.shape, NEG, f32)
    s_ref[...] = jnp.zeros(s_ref.shape, f32)
    acc_ref[...] = jnp.zeros(acc_ref.shape, f32)
    scale = HEAD_DIM ** -0.5
    causal = lax.broadcasted_iota(i32, (fb, fb), 0) <= lax.broadcasted_iota(i32, (fb, fb), 1)

    def step(kb, diagonal):
        k0 = pl.multiple_of(kb * fb, fb)
        for h in range(B_HEADS):
            sl = slice(h * LANES, (h + 1) * LANES)
            st = _dot_nt(bka_ref[h, pl.ds(k0, fb), :], qa_ref[h]) * scale
            if diagonal:
                st = jnp.where(causal, st, NEG)
            m_old = m_ref[h]
            m_new = jnp.maximum(m_old, jnp.max(st, axis=0, keepdims=True))
            alpha = jnp.exp(m_old - m_new)
            p = jnp.exp(st - m_new)
            s_ref[h] = s_ref[h] * alpha + jnp.sum(p, axis=0, keepdims=True)
            m_ref[h] = m_new
            acc_ref[h] = acc_ref[h] * alpha + _dot_tn(bv_ref[pl.ds(k0, fb), sl], p.astype(bf16))

    def body(kb, carry):
        step(kb, False)
        return carry

    lax.fori_loop(0, i, body, 0)
    step(i, True)
    for h in range(B_HEADS):
        sl = slice(h * LANES, (h + 1) * LANES)
        out_ref[:, sl] = (acc_ref[h] / s_ref[h]).T.astype(bf16)


def _fattn(l, zcat3, bka, gbq):
    bsz, tp, _ = zcat3.shape
    fb = _seq_tile(tp, 384)
    once = pl.Buffered(1)
    return pl.pallas_call(
        _fattn_kernel,
        grid_spec=pltpu.PrefetchScalarGridSpec(
            num_scalar_prefetch=1, grid=(bsz, tp // fb),
            in_specs=[
                pl.BlockSpec((None, fb, B_WIDTH), lambda b, i, l: (b, i, T_BQ * LANES // B_WIDTH)),
                pl.BlockSpec((None, B_HEADS, tp, 2 * LANES), lambda b, i, l: (b, 0, 0, 0), pipeline_mode=once),
                pl.BlockSpec((None, tp, B_WIDTH), lambda b, i, l: (b, 0, T_BV * LANES // B_WIDTH),
                             pipeline_mode=once),
                pl.BlockSpec((None, 1, LANES), lambda b, i, l: (l[0], 0, 0)),
            ],
            out_specs=pl.BlockSpec((None, fb, B_WIDTH), lambda b, i, l: (b, i, 0)),
            scratch_shapes=[
                pltpu.VMEM((B_HEADS, fb, 2 * LANES), bf16),
                pltpu.VMEM((B_HEADS, 1, fb), f32),
                pltpu.VMEM((B_HEADS, 1, fb), f32),
                pltpu.VMEM((B_HEADS, HEAD_DIM, fb), f32),
            ]),
        out_shape=jax.ShapeDtypeStruct((bsz, tp, B_WIDTH), bf16),
        compiler_params=_cparams(("parallel", "arbitrary")),
        name="fattn",
    )(l, zcat3, bka, zcat3, gbq)


def _gla_kernel(l_ref, cq_ref, ck_ref, cv_ref, cg_ref, zs_ref, w2_ref, gb_ref, og_ref, e_ref,
                out_ref, st_ref, oi_ref, *, t_real):
    c = pl.program_id(0)
    nb, cc, dk = cq_ref.shape

    @pl.when(c == 0)
    def _():
        st_ref[...] = jnp.zeros_like(st_ref)

    rowc = lax.broadcasted_iota(i32, (cc, 1), 0)
    pos = c * cc + rowc
    valid = (pos >= FRONT) & (pos < FRONT + t_real)
    tri = (lax.broadcasted_iota(i32, (cc, cc), 0) >= lax.broadcasted_iota(i32, (cc, cc), 1)).astype(bf16)
    lane_k = lax.broadcasted_iota(i32, (1, dk), 1)
    head_masks = [(lane_k >= h * C_DK) & (lane_k < (h + 1) * C_DK) for h in range(C_HEADS)]

    for b in range(nb):
        x = _dot(zs_ref[b].astype(bf16), w2_ref[...]) + gb_ref[...]
        la = jnp.where(valid, _log_sigmoid(x) * (1.0 / C_TAU), 0.0)
        hi, mid, lo = _split3(la)
        bc = _dot(tri, hi) + _dot(tri, mid) + _dot(tri, lo)
        q = cq_ref[b].astype(f32) * (C_DK ** -0.5)
        k = ck_ref[b].astype(f32)
        v = cv_ref[b]
        vf = v.astype(f32)
        blast = bc[cc - 1:cc, :]
        qe = q * jnp.exp(bc)
        ke = k * jnp.exp(blast - bc)
        st = st_ref[b]
        stb = st.astype(bf16)

        o_inter = jnp.concatenate(
            [_dot_nt(jnp.where(head_masks[h], qe, 0.0).astype(bf16), stb) for h in range(C_HEADS)], axis=1)

        rows = []
        for r in range(cc):
            dec = jnp.exp(jnp.minimum(bc[r:r + 1, :] - bc, 0.0))
            rows.append(jnp.where(rowc <= r, q[r:r + 1, :] * k * dec, 0.0).astype(bf16))
        rr = _dot(jnp.concatenate(rows, axis=0), e_ref[...])
        for r in range(cc):
            oi_ref[r:r + 1, :] = jnp.sum(rr[r * cc:(r + 1) * cc, :] * vf, axis=0, keepdims=True)
        o = o_inter + oi_ref[...]

        new_st = st * jnp.exp(blast)
        for h in range(C_HEADS):
            km = jnp.where(head_masks[h], ke, 0.0).astype(bf16)
            new_st = new_st + _dot_tn(v[:, h * C_DV:(h + 1) * C_DV], km)
        st_ref[b] = new_st

        g = cg_ref[b].astype(f32)
        gs = g * (1.0 / (1.0 + jnp.exp(-g)))
        for h in range(C_HEADS):
            sl = slice(h * C_DV, (h + 1) * C_DV)
            out_ref[b, :, sl] = (_rms_gain(o[:, sl], og_ref[...]) * gs[:, sl]).astype(bf16)


def _gla(l, zcat3, zs3, w2p, gb, og, emat, t_real):
    bsz, tp, _ = zcat3.shape
    cc = GLA_C
    dk = C_HEADS * C_DK
    return pl.pallas_call(
        functools.partial(_gla_kernel, t_real=t_real),
        grid_spec=pltpu.PrefetchScalarGridSpec(
            num_scalar_prefetch=1, grid=(tp // cc,),
            in_specs=[
                pl.BlockSpec((bsz, cc, dk), lambda c, l: (0, c, T_CQ * LANES // dk)),
                pl.BlockSpec((bsz, cc, dk), lambda c, l: (0, c, T_CK * LANES // dk)),
                pl.BlockSpec((bsz, cc, C_WIDTH), lambda c, l: (0, c, T_CV * LANES // C_WIDTH)),
                pl.BlockSpec((bsz, cc, C_WIDTH), lambda c, l: (0, c, T_CG * LANES // C_WIDTH)),
                pl.BlockSpec((bsz, cc, LANES), lambda c, l: (0, c, 0)),
                pl.BlockSpec((None, LANES, dk), lambda c, l: (l[0], 0, 0)),
                pl.BlockSpec((None, 1, dk), lambda c, l: (l[0], 0, 0)),
                pl.BlockSpec((None, 1, C_DV), lambda c, l: (l[0], 0, 0)),
                pl.BlockSpec((dk, C_WIDTH), lambda c, l: (0, 0)),
            ],
            out_specs=pl.BlockSpec((bsz, cc, C_WIDTH), lambda c, l: (0, c, 0)),
            scratch_shapes=[pltpu.VMEM((bsz, C_DV, dk), f32), pltpu.VMEM((cc, C_WIDTH), f32)]),
        out_shape=jax.ShapeDtypeStruct((bsz, tp, C_WIDTH), bf16),
        compiler_params=_cparams(("arbitrary",)),
        name="gla",
    )(l, zcat3, zcat3, zcat3, zcat3, zs3, w2p, gb, og, emat)


def _outproj_kernel(l_ref, oa_ref, ob_ref, oc_ref, h_ref, wo_ref, g_ref, wr_ref, br_ref,
                    hmid_ref, u_ref, eid_ref, ew_ref, *, t_real):
    tm = h_ref.shape[0]
    mix = (_dot(oa_ref[...], wo_ref[0:A_WIDTH, :])
           + _dot(ob_ref[...], wo_ref[A_WIDTH:A_WIDTH + B_WIDTH, :])
           + _dot(oc_ref[...], wo_ref[A_WIDTH + B_WIDTH:, :]))
    pos = pl.program_id(1) * tm + lax.broadcasted_iota(i32, (tm, 1), 0)
    valid = (pos >= FRONT) & (pos < FRONT + t_real)
    hm = h_ref[...] + jnp.where(valid, mix, 0.0)
    hmid_ref[...] = hm
    u = _rms_gain(hm, g_ref[...]).astype(bf16)
    u_ref[...] = u

    logits = _dot(u, wr_ref[...]) + br_ref[...]
    lane = lax.broadcasted_iota(i32, (tm, LANES), 1)
    lanef = lane.astype(f32)
    big = float(4 * LANES)
    first = lambda hit: jnp.min(jnp.where(hit, lanef, big), axis=-1, keepdims=True).astype(i32)
    gl = jnp.where(lane < R_GRP + N_GROUPS, logits, -jnp.inf)
    gmax = jnp.max(gl, axis=-1, keepdims=True)
    g_p = 1.0 / jnp.sum(jnp.exp(gl - gmax), axis=-1, keepdims=True)
    g_i = first(gl == gmax)
    e_lane = lane - R_EXP
    emask = (e_lane >= 0) & (e_lane < N_EXPERTS) & ((e_lane >> 3) == g_i)
    el = jnp.where(emask, logits, -jnp.inf)
    m1 = jnp.max(el, axis=-1, keepdims=True)
    i1 = first(el == m1)
    el2 = jnp.where(lane == i1, -jnp.inf, el)
    m2 = jnp.max(el2, axis=-1, keepdims=True)
    i2 = first(el2 == m2)
    r = jnp.exp(m2 - m1)
    w1 = g_p / (1.0 + r)
    w2 = g_p * r / (1.0 + r)
    eid_ref[...] = jnp.where(lane == 0, i1 - R_EXP, jnp.where(lane == 1, i2 - R_EXP, 0))
    ew_ref[...] = jnp.where(lane == 0, w1, jnp.where(lane == 1, w2, 0.0))


def _seq_tile(tp, cap):
    return max(t for t in range(BLK, cap + 1, BLK) if tp % t == 0)


def _outproj(l, oa, ob, oc, h, wo, gain, wr, br, t_real, tp):
    n, d = h.shape
    tm = _seq_tile(tp, 384)
    nj = tp // tm
    row = lambda w: pl.BlockSpec((tm, w), lambda b, j, l: (b * nj + j, 0))
    return pl.pallas_call(
        functools.partial(_outproj_kernel, t_real=t_real),
        grid_spec=pltpu.PrefetchScalarGridSpec(
            num_scalar_prefetch=1, grid=(n // tp, nj),
            in_specs=[row(A_WIDTH), row(B_WIDTH), row(C_WIDTH), row(d),
                      pl.BlockSpec((None, d, d), lambda b, j, l: (l[0], 0, 0)),
                      pl.BlockSpec((None, 1, d), lambda b, j, l: (l[0], 0, 0)),
                      pl.BlockSpec((None, d, LANES), lambda b, j, l: (l[0], 0, 0)),
                      pl.BlockSpec((None, 1, LANES), lambda b, j, l: (l[0], 0, 0))],
            out_specs=[row(d), row(d), row(LANES), row(LANES)]),
        out_shape=[jax.ShapeDtypeStruct((n, d), f32), jax.ShapeDtypeStruct((n, d), bf16),
                   jax.ShapeDtypeStruct((n, LANES), i32), jax.ShapeDtypeStruct((n, LANES), f32)],
        compiler_params=_cparams(("parallel", "parallel")),
        name="outproj",
    )(l, oa, ob, oc, h, wo, gain, wr, br)


def _moe_kernel(l_ref, te_ref, na_ref, xs_ref, w1_ref, w3_ref, w2_ref, rw_ref, ys_ref):
    i = pl.program_id(0)

    @pl.when(i < na_ref[0])
    def _():
        x = xs_ref[...]
        h1 = _dot(x, w1_ref[...])
        h3 = _dot(x, w3_ref[...])
        hid = (h1 * (1.0 / (1.0 + jnp.exp(-h1))) * h3).astype(bf16)
        ys_ref[...] = (_dot(hid, w2_ref[...]) * rw_ref[...]).astype(bf16)

    @pl.when(i >= na_ref[0])
    def _():
        ys_ref[...] = jnp.zeros_like(ys_ref)


def _moe(l, te, na, xs, w1, w3, w2, rw):
    p, d = xs.shape
    tm = MOE_TM
    return pl.pallas_call(
        _moe_kernel,
        grid_spec=pltpu.PrefetchScalarGridSpec(
            num_scalar_prefetch=3, grid=(p // tm,),
            in_specs=[
                pl.BlockSpec((tm, d), lambda i, l, te, na: (i, 0)),
                pl.BlockSpec((None, None, d, D_EXPERT), lambda i, l, te, na: (l[0], te[i], 0, 0)),
                pl.BlockSpec((None, None, d, D_EXPERT), lambda i, l, te, na: (l[0], te[i], 0, 0)),
                pl.BlockSpec((None, None, D_EXPERT, d), lambda i, l, te, na: (l[0], te[i], 0, 0)),
                pl.BlockSpec((tm, 1), lambda i, l, te, na: (i, 0)),
            ],
            out_specs=pl.BlockSpec((tm, d), lambda i, l, te, na: (i, 0))),
        out_shape=jax.ShapeDtypeStruct((p, d), bf16),
        compiler_params=_cparams(("arbitrary",)),
        name="moe",
    )(l, te, na, xs, w1, w3, w2, rw)


def _route(eid, ew, n_rows_pad):
    n = eid.shape[0]
    tm = MOE_TM
    e_flat = eid[:, :EXP_TOPK_CONST].reshape(-1)
    w_flat = ew[:, :EXP_TOPK_CONST].reshape(-1)
    na_all = e_flat.shape[0]
    order = jnp.argsort(e_flat, stable=True).astype(i32)
    counts = jnp.sum((e_flat[:, None] == jnp.arange(N_EXPERTS, dtype=i32)[None, :]).astype(i32), axis=0)
    pc = ((counts + tm - 1) // tm) * tm
    pend = jnp.cumsum(pc)
    po = pend - pc
    co = jnp.cumsum(counts) - counts
    r = jnp.arange(n_rows_pad, dtype=i32)
    e_r = jnp.minimum(jnp.sum((r[:, None] >= pend[None, :]).astype(i32), axis=1), N_EXPERTS - 1)
    local = r - po[e_r]
    valid_r = (local < counts[e_r]) & (r < pend[-1])
    a_r = order[jnp.clip(co[e_r] + local, 0, na_all - 1)]
    tok_r = jnp.where(valid_r, a_r // EXP_TOPK_CONST, 0)
    w_r = jnp.where(valid_r, w_flat[a_r], 0.0)
    n_act = (pend[-1] // tm).astype(i32)
    tile_e = e_r[::tm]
    tidx = jnp.arange(n_rows_pad // tm, dtype=i32)
    te = jnp.where(tidx < n_act, tile_e, tile_e[jnp.maximum(n_act - 1, 0)])
    inv = jnp.argsort(order).astype(i32)
    pos_a = po[e_flat] + inv - co[e_flat]
    return tok_r, w_r, te, n_act.reshape(1), pos_a.reshape(n, EXP_TOPK_CONST)


def _final_kernel(hmid_ref, y2_ref, out_ref):
    d = hmid_ref.shape[1]
    out_ref[...] = hmid_ref[...] + y2_ref[:, :d].astype(f32) + y2_ref[:, d:].astype(f32)


def _final(hmid, y2):
    n, d = hmid.shape
    tm = _row_tile(n, 512)
    return pl.pallas_call(
        _final_kernel, grid=(n // tm,),
        in_specs=[pl.BlockSpec((tm, d), lambda i: (i, 0)), pl.BlockSpec((tm, 2 * d), lambda i: (i, 0))],
        out_specs=pl.BlockSpec((tm, d), lambda i: (i, 0)),
        out_shape=jax.ShapeDtypeStruct((n, d), f32),
        compiler_params=_cparams(("parallel",)),
        name="final",
    )(hmid, y2)


def _rope_tables(tp):
    pos = (jnp.arange(tp, dtype=f32) - FRONT)[:, None]

    def tab(dim, reps):
        inv = 1.0 / (ROPE_THETA ** (jnp.arange(0, dim, 2, dtype=f32) / dim))
        ang = pos * inv[None, :]
        return jnp.tile(jnp.cos(ang), (1, reps)), jnp.tile(jnp.sin(ang), (1, reps))

    cos, sin = tab(HEAD_DIM, 2)
    icos, isin = tab(IDX_DIM, 4)
    return cos, sin, icos, isin


def _split_w_in(w_in):
    parts, off = [], 0
    for s in IN_SPLITS:
        parts.append(w_in[..., off:off + s])
        off += s
    return parts


def kernel(x, meta, norm_mix_g, norm_ffn_g, w_in, a_qn_g, a_kn_g, b_qn_g, b_kn_g, b_f_bias,
           c_gate_w2, c_gate_b, c_on_g, w_out, r_group_w, r_group_b, r_exp_w, r_exp_b,
           e_w1, e_w3, e_w2):
    bsz, n_seq, d = x.shape
    depth = w_in.shape[0]
    t_real = n_seq + N_META
    k_top = min(TOPK_MAX, n_seq // 4)
    tp = -(-(FRONT + t_real) // BLK) * BLK
    n = bsz * tp

    (waq, wak, wav, wiq, wik, wiw, wbq, wbk, wbv, wbf, wcq, wck, wcv, wcr, wcg) = _split_w_in(w_in)
    zc = lambda w: jnp.zeros((depth, d, w), w_in.dtype)
    wcat = jnp.concatenate([waq, wak, wav, wiq, wik, zc(LANES - IDX_DIM), zc(LANES),
                            wbq, wbk, wbv, wcq, wck, wcv, wcg], axis=-1).astype(bf16)
    ws = jnp.concatenate([wiw, wbf, wcr, zc(LANES - S_CR - C_GATE_RANK)], axis=-1).astype(bf16)
    wo = w_out.astype(bf16)
    wr = jnp.concatenate([r_group_w, zc(R_EXP - N_GROUPS), r_exp_w, zc(LANES - R_EXP - N_EXPERTS)],
                         axis=-1).astype(bf16)
    zl = lambda w: jnp.zeros((depth, w), f32)
    br = jnp.concatenate([r_group_b, zl(R_EXP - N_GROUPS), r_exp_b, zl(LANES - R_EXP - N_EXPERTS)],
                         axis=-1)[:, None, :]
    fbias = jnp.concatenate([zl(S_BF), b_f_bias, zl(LANES - S_BF - B_HEADS)], axis=-1)[:, None, :]
    dk = C_HEADS * C_DK
    w2p = jnp.concatenate([jnp.zeros((depth, S_CR, dk), f32), c_gate_w2,
                           jnp.zeros((depth, LANES - S_CR - C_GATE_RANK, dk), f32)], axis=1).astype(bf16)
    gb = c_gate_b[:, None, :]
    og = c_on_g[:, None, :]
    g_mix = norm_mix_g[:, None, :]
    g_ffn = norm_ffn_g[:, None, :]
    gaq, gak, gbq, gbk = (g[:, None, :] for g in (a_qn_g, a_kn_g, b_qn_g, b_kn_g))
    ew1, ew3, ew2 = e_w1.astype(bf16), e_w3.astype(bf16), e_w2.astype(bf16)
    emat = (jnp.arange(dk, dtype=i32)[:, None] // C_DK
            == jnp.arange(C_WIDTH, dtype=i32)[None, :] // C_DV).astype(bf16)
    tabs = _rope_tables(tp)

    h0 = jnp.concatenate([
        jnp.zeros((bsz, FRONT, d), f32),
        jnp.broadcast_to(meta[None].astype(f32), (bsz, N_META, d)),
        x.astype(f32),
        jnp.zeros((bsz, tp - FRONT - t_real, d), f32)], axis=1).reshape(n, d)
    y0 = jnp.zeros((n, 2 * d), bf16)
    n_rows_pad = -(-(EXP_TOPK_CONST * n + N_EXPERTS * (MOE_TM - 1)) // MOE_TM) * MOE_TM

    def layer(li, carry):
        hmid, y2 = carry
        l = jnp.reshape(li, (1,)).astype(i32)
        h, zcat, zs = _inproj(l, hmid, y2, g_mix, wcat, ws)
        zcat3 = zcat.reshape(bsz, tp, N_CAT)
        zs3 = zs.reshape(bsz, tp, LANES)
        akr, ikr, bka = _kprep(l, zcat3, zs3, tabs, gak, gbk, fbias, t_real)
        oa = _dsa(l, zcat3, zs3, akr, ikr, tabs, gaq, t_real, k_top)
        ob = _fattn(l, zcat3, bka, gbq)
        oc = _gla(l, zcat3, zs3, w2p, gb, og, emat, t_real)
        hmid2, u, eid, ew = _outproj(l, oa.reshape(n, A_WIDTH), ob.reshape(n, B_WIDTH),
                                     oc.reshape(n, C_WIDTH), h, wo, g_ffn, wr, br, t_real, tp)
        tok_r, w_r, te, n_act, pos_a = _route(eid, ew, n_rows_pad)
        xs = jnp.take(u, tok_r, axis=0)
        ys = _moe(l, te, n_act, xs, ew1, ew3, ew2, w_r[:, None])
        y2n = jnp.take(ys, pos_a, axis=0).reshape(n, 2 * d)
        return hmid2, y2n

    hmid, y2 = lax.fori_loop(0, depth, layer, (h0, y0))
    out = _final(hmid, y2).reshape(bsz, tp, d)
    return out[:, FRONT + N_META:FRONT + t_real].astype(x.dtype)
```

```python
import functools

import jax
import jax.numpy as jnp
from jax import lax
from jax.experimental import pallas as pl
from jax.experimental.pallas import tpu as pltpu

f32 = jnp.float32
bf16 = jnp.bfloat16
i32 = jnp.int32

D_MODEL = 2048
CHUNK = 64
N_META = 16
ROPE_THETA = 10000.0
EPS = 1e-6
HEAD_DIM = 128
A_HEADS = 6
IDX_HEADS = 16
IDX_DIM = 64
TOPK_MAX = 256
B_HEADS = 6
C_HEADS = 4
C_DK = 64
C_DV = 128
C_GATE_RANK = 16
C_TAU = 16.0
N_GROUPS = 4
EXP_PER_GROUP = 8
N_EXPERTS = N_GROUPS * EXP_PER_GROUP
D_EXPERT = 512
A_WIDTH = A_HEADS * HEAD_DIM
B_WIDTH = B_HEADS * HEAD_DIM
C_WIDTH = C_HEADS * C_DV
IN_SPLITS = (A_WIDTH, HEAD_DIM, HEAD_DIM, IDX_HEADS * IDX_DIM, IDX_DIM, IDX_HEADS,
             B_WIDTH, B_WIDTH, B_WIDTH, B_HEADS,
             C_HEADS * C_DK, C_HEADS * C_DK, C_WIDTH, C_GATE_RANK, C_WIDTH)

LANES = 128
FRONT = (-N_META) % CHUNK
BLK = 128
T_AQ, T_AK, T_AV, T_IQ, T_IK = 0, 6, 7, 8, 16
T_BQ, T_BK, T_BV = 18, 24, 30
T_CQ, T_CK, T_CV, T_CG = 36, 38, 40, 44
N_CAT = 48 * LANES
S_IW, S_BF, S_CR = 0, 16, 22
R_GRP, R_EXP = 0, 32
GLA_C = 32
DSA_G = 3
MOE_TM = 256
EXP_TOPK_CONST = 2
NEG = -1e30
LOG2E = 1.4426950408889634
INT_MIN = -2 ** 31
VMEM_LIMIT = 56 * 1024 * 1024


def _cparams(sem):
    return pltpu.CompilerParams(dimension_semantics=sem, vmem_limit_bytes=VMEM_LIMIT)


def _row_tile(n, cap):
    t = cap
    while n % t:
        t //= 2
    return t


def _log_sigmoid(x):
    return jnp.minimum(x, 0.0) - jnp.log(1.0 + jnp.exp(-jnp.abs(x)))


def _dot(a, b):
    return jnp.dot(a, b, preferred_element_type=f32)


def _dot_nt(a, b):
    return lax.dot_general(a, b, (((1,), (1,)), ((), ())), preferred_element_type=f32)


def _dot_tn(a, b):
    return lax.dot_general(a, b, (((0,), (0,)), ((), ())), preferred_element_type=f32)


def _split3(x):
    hi = x.astype(bf16)
    r1 = x - hi.astype(f32)
    mid = r1.astype(bf16)
    lo = (r1 - mid.astype(f32)).astype(bf16)
    return hi, mid, lo


def _rope128(x, cos, sin, lane):
    return x * cos + pltpu.roll(x, 64, 1) * jnp.where(lane < 64, -sin, sin)


def _rope64(x, cos, sin, lane):
    low = (lane & 63) < 32
    return (x * cos + pltpu.roll(x, 32, 1) * jnp.where(low, 0.0, sin)
            + pltpu.roll(x, 96, 1) * jnp.where(low, -sin, 0.0))


def _rms_gain(x, g):
    return x * lax.rsqrt(jnp.mean(x * x, axis=-1, keepdims=True) + EPS) * g


def _inproj_kernel(l_ref, hmid_ref, y2_ref, g_ref, w_ref, ws_ref, h_ref, z_ref, zs_ref, xn_ref):
    j = pl.program_id(1)

    @pl.when(j == 0)
    def _():
        d = hmid_ref.shape[1]
        h = hmid_ref[...] + y2_ref[:, :d].astype(f32) + y2_ref[:, d:].astype(f32)
        h_ref[...] = h
        xn = _rms_gain(h, g_ref[...]).astype(bf16)
        xn_ref[...] = xn
        zs_ref[...] = _dot(xn, ws_ref[...])

    z_ref[...] = _dot(xn_ref[...], w_ref[...]).astype(bf16)


def _inproj(l, hmid, y2, gain, wcat, ws):
    n, d = hmid.shape
    tm = _row_tile(n, 512)
    tn = 512
    grid = (n // tm, N_CAT // tn)
    return pl.pallas_call(
        _inproj_kernel,
        grid_spec=pltpu.PrefetchScalarGridSpec(
            num_scalar_prefetch=1, grid=grid,
            in_specs=[
                pl.BlockSpec((tm, d), lambda i, j, l: (i, 0)),
                pl.BlockSpec((tm, 2 * d), lambda i, j, l: (i, 0)),
                pl.BlockSpec((None, 1, d), lambda i, j, l: (l[0], 0, 0)),
                pl.BlockSpec((None, d, tn), lambda i, j, l: (l[0], 0, j)),
                pl.BlockSpec((None, d, LANES), lambda i, j, l: (l[0], 0, 0)),
            ],
            out_specs=[
                pl.BlockSpec((tm, d), lambda i, j, l: (i, 0)),
                pl.BlockSpec((tm, tn), lambda i, j, l: (i, j)),
                pl.BlockSpec((tm, LANES), lambda i, j, l: (i, 0)),
            ],
            scratch_shapes=[pltpu.VMEM((tm, d), bf16)]),
        out_shape=[jax.ShapeDtypeStruct((n, d), f32),
                   jax.ShapeDtypeStruct((n, N_CAT), bf16),
                   jax.ShapeDtypeStruct((n, LANES), f32)],
        compiler_params=_cparams(("parallel", "arbitrary")),
        name="inproj",
    )(l, hmid, y2, gain, wcat, ws)


def _kprep_kernel(l_ref, ak_ref, ik_ref, bk_ref, zs_ref,
                  cos_ref, sin_ref, icos_ref, isin_ref, gak_ref, gbk_ref, fb_ref,
                  akr_ref, ikr_ref, bka_ref, carry_ref, *, t_real):
    k = pl.program_id(1)
    lane = lax.broadcasted_iota(i32, (BLK, LANES), 1)
    row = lax.broadcasted_iota(i32, (BLK, LANES), 0)
    pos = k * BLK + row
    valid = (pos >= FRONT) & (pos < FRONT + t_real)

    x = ak_ref[...].astype(f32)
    akr_ref[...] = _rope128(_rms_gain(x, gak_ref[...]), cos_ref[...], sin_ref[...], lane).astype(bf16)
    ikr_ref[...] = _rope64(ik_ref[...].astype(f32), icos_ref[...], isin_ref[...], lane).astype(bf16)

    @pl.when(k == 0)
    def _():
        carry_ref[...] = jnp.zeros_like(carry_ref)

    lf = jnp.where(valid, _log_sigmoid(zs_ref[...] + fb_ref[...]), 0.0)
    tri = (row >= lane).astype(bf16)
    hi, mid, lo = _split3(lf)
    fcum = _dot(tri, hi) + _dot(tri, mid) + _dot(tri, lo) + carry_ref[...]
    carry_ref[...] = fcum[BLK - 1:BLK, :]

    fs = jnp.where(valid, fcum * (-(HEAD_DIM ** 0.5)), NEG)
    for h in range(B_HEADS):
        sl = slice(h * LANES, (h + 1) * LANES)
        bka_ref[h, :, 0:LANES] = _rms_gain(bk_ref[:, sl].astype(f32), gbk_ref[...]).astype(bf16)
        p0, p1, p2 = _split3(fs[:, S_BF + h:S_BF + h + 1])
        aug = jnp.where(lane == 0, p0.astype(f32),
                        jnp.where(lane == 1, p1.astype(f32), jnp.where(lane == 2, p2.astype(f32), 0.0)))
        bka_ref[h, :, LANES:2 * LANES] = aug.astype(bf16)


def _kprep(l, zcat3, zs3, tabs, gak, gbk, fbias, t_real):
    bsz, tp, _ = zcat3.shape
    nkb = tp // BLK
    cos, sin, icos, isin = tabs
    tile = lambda c: pl.BlockSpec((None, BLK, LANES), lambda b, k, l, c=c: (b, k, c))
    wide = lambda c: pl.BlockSpec((None, BLK, B_WIDTH), lambda b, k, l, c=c: (b, k, c))
    tab = pl.BlockSpec((BLK, LANES), lambda b, k, l: (k, 0))
    gain = pl.BlockSpec((None, 1, LANES), lambda b, k, l: (l[0], 0, 0))
    return pl.pallas_call(
        functools.partial(_kprep_kernel, t_real=t_real),
        grid_spec=pltpu.PrefetchScalarGridSpec(
            num_scalar_prefetch=1, grid=(bsz, nkb),
            in_specs=[tile(T_AK), tile(T_IK),
                      wide(T_BK * LANES // B_WIDTH),
                      pl.BlockSpec((None, BLK, LANES), lambda b, k, l: (b, k, 0)),
                      tab, tab, tab, tab, gain, gain, gain],
            out_specs=[
                pl.BlockSpec((None, BLK, LANES), lambda b, k, l: (b, k, 0)),
                pl.BlockSpec((None, BLK, LANES), lambda b, k, l: (b, k, 0)),
                pl.BlockSpec((None, B_HEADS, BLK, 2 * LANES), lambda b, k, l: (b, 0, k, 0)),
            ],
            scratch_shapes=[pltpu.VMEM((1, LANES), f32)]),
        out_shape=[jax.ShapeDtypeStruct((bsz, tp, LANES), bf16),
                   jax.ShapeDtypeStruct((bsz, tp, LANES), bf16),
                   jax.ShapeDtypeStruct((bsz, B_HEADS, tp, 2 * LANES), bf16)],
        compiler_params=_cparams(("parallel", "arbitrary")),
        name="kprep",
    )(l, zcat3, zcat3, zcat3, zs3, cos, sin, icos, isin, gak, gbk, fbias)


def _dsa_kernel(l_ref, aq_ref, iq_ref, zs_ref, akr_ref, ikr_ref, av_ref,
                cos_ref, sin_ref, icos_ref, isin_ref, gq_ref, out_ref,
                key_ref, iqs_ref, q6_ref, iwt_ref, m_ref, s_ref, acc_ref, *, t_real, k_top):
    i = pl.program_id(1)
    nk = i + 1
    lane = lax.broadcasted_iota(i32, (BLK, LANES), 1)

    for h in range(A_HEADS):
        x = aq_ref[:, h * LANES:(h + 1) * LANES].astype(f32)
        xr = _rope128(_rms_gain(x, gq_ref[...]), cos_ref[...], sin_ref[...], lane)
        q6_ref[h * BLK:(h + 1) * BLK, :] = xr.astype(bf16)
    for t in range(IDX_HEADS // 2):
        x = iq_ref[:, t * LANES:(t + 1) * LANES].astype(f32)
        xr = _rope64(x, icos_ref[...], isin_ref[...], lane)
        iqs_ref[(2 * t) * BLK:(2 * t + 1) * BLK, :] = jnp.where(lane < IDX_DIM, xr, 0.0).astype(bf16)
        iqs_ref[(2 * t + 1) * BLK:(2 * t + 2) * BLK, :] = jnp.where(
            lane < IDX_DIM, pltpu.roll(xr, 64, 1), 0.0).astype(bf16)
    iwt_ref[...] = (zs_ref[...] * (IDX_HEADS ** -0.5 * IDX_DIM ** -0.5)).T

    gb = DSA_G * BLK
    ng = lax.div(nk, jnp.int32(DSA_G))
    nr = nk - ng * DSA_G

    def over_keys(fn, init):
        c = lax.fori_loop(0, ng, lambda g, c: fn(pl.multiple_of(g * gb, gb), gb, c), init)
        return lax.fori_loop(0, nr, lambda r, c: fn(pl.multiple_of((ng * DSA_G + r) * BLK, BLK), BLK, c), c)

    def score_rows(k0, nrows, carry):
        dt = _dot_nt(ikr_ref[pl.ds(k0, nrows), :], iqs_ref[...])
        s = jnp.zeros((nrows, LANES), f32)
        for h in range(IDX_HEADS):
            s = s + iwt_ref[h:h + 1, :] * jnp.maximum(dt[:, h * LANES:(h + 1) * LANES], 0.0)
        kpos = k0 + lax.broadcasted_iota(i32, (nrows, LANES), 0)
        qpos = i * BLK + lax.broadcasted_iota(i32, (nrows, LANES), 1)
        adm = ((kpos >> 6) <= (qpos >> 6)) & (kpos >= FRONT) & (kpos < FRONT + t_real)
        bits = lax.bitcast_convert_type(s, i32)
        key = bits ^ ((bits >> 31) & 0x7FFFFFFF)
        key_ref[pl.ds(k0, nrows), :] = jnp.where(adm, key, INT_MIN)
        return carry

    over_keys(score_rows, 0)

    def bit_body(t, thr_u):
        bit = jnp.left_shift(jnp.int32(1), 31 - t)
        cand_u = thr_u | bit
        cand_s = cand_u ^ INT_MIN

        def count_rows(k0, nrows, c):
            hit = (key_ref[pl.ds(k0, nrows), :] >= cand_s).astype(i32)
            for j in range(nrows // BLK):
                c = c + hit[j * BLK:(j + 1) * BLK, :]
            return c

        cnt = over_keys(count_rows, jnp.zeros((BLK, LANES), i32))
        tot = jnp.sum(cnt.astype(f32), axis=0, keepdims=True)
        return jnp.where(tot >= k_top, cand_u, thr_u)

    thr_u = lax.fori_loop(0, 32, bit_body, jnp.zeros((1, LANES), i32))
    thr_s = jnp.maximum(thr_u ^ INT_MIN, INT_MIN + 1)

    m_ref[...] = jnp.full(m_ref.shape, NEG, f32)
    s_ref[...] = jnp.zeros(s_ref.shape, f32)
    acc_ref[...] = jnp.zeros(acc_ref.shape, f32)
    c2 = HEAD_DIM ** -0.5 * LOG2E

    def attn_rows(k0, nrows, carry):
        st = _dot_nt(akr_ref[pl.ds(k0, nrows), :], q6_ref[...])
        bias = jnp.where(key_ref[pl.ds(k0, nrows), :] >= thr_s, 0.0, NEG)
        ps, alphas = [], []
        for h in range(A_HEADS):
            sl = slice(h * LANES, (h + 1) * LANES)
            sh = st[:, sl] + bias
            m_old = m_ref[:, sl]
            m_new = jnp.maximum(m_old, jnp.max(sh, axis=0, keepdims=True))
            alpha = jnp.exp2((m_old - m_new) * c2)
            p = jnp.exp2((sh - m_new) * c2)
            s_ref[:, sl] = s_ref[:, sl] * alpha + jnp.sum(p, axis=0, keepdims=True)
            m_ref[:, sl] = m_new
            alphas.append(alpha)
            ps.append(p.astype(bf16))
        pv = _dot_tn(av_ref[pl.ds(k0, nrows), :], jnp.concatenate(ps, axis=1))
        acc_ref[...] = acc_ref[...] * jnp.concatenate(alphas, axis=1) + pv
        return carry

    over_keys(attn_rows, 0)

    for h in range(A_HEADS):
        sl = slice(h * LANES, (h + 1) * LANES)
        o = acc_ref[:, sl] / jnp.maximum(s_ref[:, sl], 1e-30)
        out_ref[:, sl] = o.T.astype(bf16)


def _dsa(l, zcat3, zs3, akr, ikr, tabs, gaq, t_real, k_top):
    bsz, tp, _ = zcat3.shape
    nkb = tp // BLK
    cos, sin, icos, isin = tabs
    tab = pl.BlockSpec((BLK, LANES), lambda b, i, l: (i, 0))
    full = pl.BlockSpec((None, tp, LANES), lambda b, i, l: (b, 0, 0))
    return pl.pallas_call(
        functools.partial(_dsa_kernel, t_real=t_real, k_top=k_top),
        grid_spec=pltpu.PrefetchScalarGridSpec(
            num_scalar_prefetch=1, grid=(bsz, nkb),
            in_specs=[
                pl.BlockSpec((None, BLK, A_WIDTH), lambda b, i, l: (b, i, 0)),
                pl.BlockSpec((None, BLK, IDX_HEADS * IDX_DIM), lambda b, i, l: (b, i, T_IQ * LANES // (IDX_HEADS * IDX_DIM))),
                pl.BlockSpec((None, BLK, LANES), lambda b, i, l: (b, i, 0)),
                full, full,
                pl.BlockSpec((None, tp, LANES), lambda b, i, l: (b, 0, T_AV)),
                tab, tab, tab, tab,
                pl.BlockSpec((None, 1, LANES), lambda b, i, l: (l[0], 0, 0)),
            ],
            out_specs=pl.BlockSpec((None, BLK, A_WIDTH), lambda b, i, l: (b, i, 0)),
            scratch_shapes=[
                pltpu.VMEM((tp, LANES), i32),
                pltpu.VMEM((IDX_HEADS * BLK, LANES), bf16),
                pltpu.VMEM((A_HEADS * BLK, LANES), bf16),
                pltpu.VMEM((LANES, BLK), f32),
                pltpu.VMEM((1, A_HEADS * BLK), f32),
                pltpu.VMEM((1, A_HEADS * BLK), f32),
                pltpu.VMEM((HEAD_DIM, A_HEADS * BLK), f32),
            ]),
        out_shape=jax.ShapeDtypeStruct((bsz, tp, A_WIDTH), bf16),
        compiler_params=_cparams(("parallel", "arbitrary")),
        name="dsa",
    )(l, zcat3, zcat3, zs3, akr, ikr, zcat3, cos, sin, icos, isin, gaq)


def _fattn_kernel(l_ref, bq_ref, bka_ref, bv_ref, gq_ref, out_ref, qa_ref, m_ref, s_ref, acc_ref):
    i = pl.program_id(1)
    fb = bq_ref.shape[0]
    ones3 = jnp.where(lax.broadcasted_iota(i32, (fb, LANES), 1) < 3, 1.0, 0.0).astype(bf16)
    for h in range(B_HEADS):
        sl = slice(h * LANES, (h + 1) * LANES)
        qa_ref[h, :, 0:LANES] = _rms_gain(bq_ref[:, sl].astype(f32), gq_ref[...]).astype(bf16)
        qa_ref[h, :, LANES:2 * LANES] = ones3
    m_ref[...] = jnp.full(m_ref.shape, NEG, f32)
    s_ref[...] = jnp.zeros(s_ref.shape, f32)
    acc_ref[...] = jnp.zeros(acc_ref.shape, f32)
    c2 = HEAD_DIM ** -0.5 * LOG2E
    causal = lax.broadcasted_iota(i32, (fb, fb), 0) <= lax.broadcasted_iota(i32, (fb, fb), 1)

    def step(kb, diagonal):
        k0 = pl.multiple_of(kb * fb, fb)
        for h in range(B_HEADS):
            sl = slice(h * LANES, (h + 1) * LANES)
            st = _dot_nt(bka_ref[h, pl.ds(k0, fb), :], qa_ref[h])
            if diagonal:
                st = jnp.where(causal, st, NEG)
            m_old = m_ref[h]
            m_new = jnp.maximum(m_old, jnp.max(st, axis=0, keepdims=True))
            alpha = jnp.exp2((m_old - m_new) * c2)
            p = jnp.exp2((st - m_new) * c2)
            s_ref[h] = s_ref[h] * alpha + jnp.sum(p, axis=0, keepdims=True)
            m_ref[h] = m_new
            acc_ref[h] = acc_ref[h] * alpha + _dot_tn(bv_ref[pl.ds(k0, fb), sl], p.astype(bf16))

    def body(kb, carry):
        step(kb, False)
        return carry

    lax.fori_loop(0, i, body, 0)
    step(i, True)
    for h in range(B_HEADS):
        sl = slice(h * LANES, (h + 1) * LANES)
        out_ref[:, sl] = (acc_ref[h] / s_ref[h]).T.astype(bf16)


def _fattn(l, zcat3, bka, gbq):
    bsz, tp, _ = zcat3.shape
    fb = _seq_tile(tp, 384)
    once = pl.Buffered(1)
    return pl.pallas_call(
        _fattn_kernel,
        grid_spec=pltpu.PrefetchScalarGridSpec(
            num_scalar_prefetch=1, grid=(bsz, tp // fb),
            in_specs=[
                pl.BlockSpec((None, fb, B_WIDTH), lambda b, i, l: (b, i, T_BQ * LANES // B_WIDTH)),
                pl.BlockSpec((None, B_HEADS, tp, 2 * LANES), lambda b, i, l: (b, 0, 0, 0), pipeline_mode=once),
                pl.BlockSpec((None, tp, B_WIDTH), lambda b, i, l: (b, 0, T_BV * LANES // B_WIDTH),
                             pipeline_mode=once),
                pl.BlockSpec((None, 1, LANES), lambda b, i, l: (l[0], 0, 0)),
            ],
            out_specs=pl.BlockSpec((None, fb, B_WIDTH), lambda b, i, l: (b, i, 0)),
            scratch_shapes=[
                pltpu.VMEM((B_HEADS, fb, 2 * LANES), bf16),
                pltpu.VMEM((B_HEADS, 1, fb), f32),
                pltpu.VMEM((B_HEADS, 1, fb), f32),
                pltpu.VMEM((B_HEADS, HEAD_DIM, fb), f32),
            ]),
        out_shape=jax.ShapeDtypeStruct((bsz, tp, B_WIDTH), bf16),
        compiler_params=_cparams(("parallel", "arbitrary")),
        name="fattn",
    )(l, zcat3, bka, zcat3, gbq)


def _gla_kernel(l_ref, cq_ref, ck_ref, cv_ref, cg_ref, zs_ref, w2_ref, gb_ref, og_ref, e_ref,
                out_ref, st_ref, oi_ref, *, t_real):
    c = pl.program_id(0)
    nb, cc, dk = cq_ref.shape

    @pl.when(c == 0)
    def _():
        st_ref[...] = jnp.zeros_like(st_ref)

    rowc = lax.broadcasted_iota(i32, (cc, 1), 0)
    pos = c * cc + rowc
    valid = (pos >= FRONT) & (pos < FRONT + t_real)
    tri = (lax.broadcasted_iota(i32, (cc, cc), 0) >= lax.broadcasted_iota(i32, (cc, cc), 1)).astype(bf16)
    lane_k = lax.broadcasted_iota(i32, (1, dk), 1)
    head_masks = [(lane_k >= h * C_DK) & (lane_k < (h + 1) * C_DK) for h in range(C_HEADS)]

    for b in range(nb):
        x = _dot(zs_ref[b].astype(bf16), w2_ref[...]) + gb_ref[...]
        la = jnp.where(valid, _log_sigmoid(x) * (1.0 / C_TAU), 0.0)
        hi, mid, lo = _split3(la)
        bc = _dot(tri, hi) + _dot(tri, mid) + _dot(tri, lo)
        q = cq_ref[b].astype(f32) * (C_DK ** -0.5)
        k = ck_ref[b].astype(f32)
        v = cv_ref[b]
        vf = v.astype(f32)
        blast = bc[cc - 1:cc, :]
        qe = q * jnp.exp(bc)
        ke = k * jnp.exp(blast - bc)
        st = st_ref[b]
        stb = st.astype(bf16)

        o_inter = jnp.concatenate(
            [_dot_nt(jnp.where(head_masks[h], qe, 0.0).astype(bf16), stb) for h in range(C_HEADS)], axis=1)

        rows = []
        for r in range(cc):
            dec = jnp.exp(jnp.minimum(bc[r:r + 1, :] - bc, 0.0))
            rows.append(jnp.where(rowc <= r, q[r:r + 1, :] * k * dec, 0.0).astype(bf16))
        rr = _dot(jnp.concatenate(rows, axis=0), e_ref[...])
        for r in range(cc):
            oi_ref[r:r + 1, :] = jnp.sum(rr[r * cc:(r + 1) * cc, :] * vf, axis=0, keepdims=True)
        o = o_inter + oi_ref[...]

        new_st = st * jnp.exp(blast)
        for h in range(C_HEADS):
            km = jnp.where(head_masks[h], ke, 0.0).astype(bf16)
            new_st = new_st + _dot_tn(v[:, h * C_DV:(h + 1) * C_DV], km)
        st_ref[b] = new_st

        g = cg_ref[b].astype(f32)
        gs = g * (1.0 / (1.0 + jnp.exp(-g)))
        for h in range(C_HEADS):
            sl = slice(h * C_DV, (h + 1) * C_DV)
            out_ref[b, :, sl] = (_rms_gain(o[:, sl], og_ref[...]) * gs[:, sl]).astype(bf16)


def _gla(l, zcat3, zs3, w2p, gb, og, emat, t_real):
    bsz, tp, _ = zcat3.shape
    cc = GLA_C
    dk = C_HEADS * C_DK
    return pl.pallas_call(
        functools.partial(_gla_kernel, t_real=t_real),
        grid_spec=pltpu.PrefetchScalarGridSpec(
            num_scalar_prefetch=1, grid=(tp // cc,),
            in_specs=[
                pl.BlockSpec((bsz, cc, dk), lambda c, l: (0, c, T_CQ * LANES // dk)),
                pl.BlockSpec((bsz, cc, dk), lambda c, l: (0, c, T_CK * LANES // dk)),
                pl.BlockSpec((bsz, cc, C_WIDTH), lambda c, l: (0, c, T_CV * LANES // C_WIDTH)),
                pl.BlockSpec((bsz, cc, C_WIDTH), lambda c, l: (0, c, T_CG * LANES // C_WIDTH)),
                pl.BlockSpec((bsz, cc, LANES), lambda c, l: (0, c, 0)),
                pl.BlockSpec((None, LANES, dk), lambda c, l: (l[0], 0, 0)),
                pl.BlockSpec((None, 1, dk), lambda c, l: (l[0], 0, 0)),
                pl.BlockSpec((None, 1, C_DV), lambda c, l: (l[0], 0, 0)),
                pl.BlockSpec((dk, C_WIDTH), lambda c, l: (0, 0)),
            ],
            out_specs=pl.BlockSpec((bsz, cc, C_WIDTH), lambda c, l: (0, c, 0)),
            scratch_shapes=[pltpu.VMEM((bsz, C_DV, dk), f32), pltpu.VMEM((cc, C_WIDTH), f32)]),
        out_shape=jax.ShapeDtypeStruct((bsz, tp, C_WIDTH), bf16),
        compiler_params=_cparams(("arbitrary",)),
        name="gla",
    )(l, zcat3, zcat3, zcat3, zcat3, zs3, w2p, gb, og, emat)


def _outproj_kernel(l_ref, oa_ref, ob_ref, oc_ref, h_ref, wo_ref, g_ref, wr_ref, br_ref,
                    hmid_ref, u_ref, eid_ref, ew_ref, *, t_real):
    tm = h_ref.shape[0]
    mix = (_dot(oa_ref[...], wo_ref[0:A_WIDTH, :])
           + _dot(ob_ref[...], wo_ref[A_WIDTH:A_WIDTH + B_WIDTH, :])
           + _dot(oc_ref[...], wo_ref[A_WIDTH + B_WIDTH:, :]))
    pos = pl.program_id(1) * tm + lax.broadcasted_iota(i32, (tm, 1), 0)
    valid = (pos >= FRONT) & (pos < FRONT + t_real)
    hm = h_ref[...] + jnp.where(valid, mix, 0.0)
    hmid_ref[...] = hm
    u = _rms_gain(hm, g_ref[...]).astype(bf16)
    u_ref[...] = u

    logits = _dot(u, wr_ref[...]) + br_ref[...]
    lane = lax.broadcasted_iota(i32, (tm, LANES), 1)
    lanef = lane.astype(f32)
    big = float(4 * LANES)
    first = lambda hit: jnp.min(jnp.where(hit, lanef, big), axis=-1, keepdims=True).astype(i32)
    gl = jnp.where(lane < R_GRP + N_GROUPS, logits, -jnp.inf)
    gmax = jnp.max(gl, axis=-1, keepdims=True)
    g_p = 1.0 / jnp.sum(jnp.exp(gl - gmax), axis=-1, keepdims=True)
    g_i = first(gl == gmax)
    e_lane = lane - R_EXP
    emask = (e_lane >= 0) & (e_lane < N_EXPERTS) & ((e_lane >> 3) == g_i)
    el = jnp.where(emask, logits, -jnp.inf)
    m1 = jnp.max(el, axis=-1, keepdims=True)
    i1 = first(el == m1)
    el2 = jnp.where(lane == i1, -jnp.inf, el)
    m2 = jnp.max(el2, axis=-1, keepdims=True)
    i2 = first(el2 == m2)
    r = jnp.exp(m2 - m1)
    w1 = g_p / (1.0 + r)
    w2 = g_p * r / (1.0 + r)
    eid_ref[...] = jnp.where(lane == 0, i1 - R_EXP, jnp.where(lane == 1, i2 - R_EXP, 0))
    ew_ref[...] = jnp.where(lane == 0, w1, jnp.where(lane == 1, w2, 0.0))


def _seq_tile(tp, cap):
    return max(t for t in range(BLK, cap + 1, BLK) if tp % t == 0)


def _outproj(l, oa, ob, oc, h, wo, gain, wr, br, t_real, tp):
    n, d = h.shape
    tm = _seq_tile(tp, 384)
    nj = tp // tm
    row = lambda w: pl.BlockSpec((tm, w), lambda b, j, l: (b * nj + j, 0))
    return pl.pallas_call(
        functools.partial(_outproj_kernel, t_real=t_real),
        grid_spec=pltpu.PrefetchScalarGridSpec(
            num_scalar_prefetch=1, grid=(n // tp, nj),
            in_specs=[row(A_WIDTH), row(B_WIDTH), row(C_WIDTH), row(d),
                      pl.BlockSpec((None, d, d), lambda b, j, l: (l[0], 0, 0)),
                      pl.BlockSpec((None, 1, d), lambda b, j, l: (l[0], 0, 0)),
                      pl.BlockSpec((None, d, LANES), lambda b, j, l: (l[0], 0, 0)),
                      pl.BlockSpec((None, 1, LANES), lambda b, j, l: (l[0], 0, 0))],
            out_specs=[row(d), row(d), row(LANES), row(LANES)]),
        out_shape=[jax.ShapeDtypeStruct((n, d), f32), jax.ShapeDtypeStruct((n, d), bf16),
                   jax.ShapeDtypeStruct((n, LANES), i32), jax.ShapeDtypeStruct((n, LANES), f32)],
        compiler_params=_cparams(("parallel", "parallel")),
        name="outproj",
    )(l, oa, ob, oc, h, wo, gain, wr, br)


def _moe_kernel(l_ref, te_ref, na_ref, xs_ref, w1_ref, w3_ref, w2_ref, rw_ref, ys_ref):
    i = pl.program_id(0)

    @pl.when(i < na_ref[0])
    def _():
        x = xs_ref[...]
        h1 = _dot(x, w1_ref[...])
        h3 = _dot(x, w3_ref[...])
        hid = (h1 * (1.0 / (1.0 + jnp.exp(-h1))) * h3).astype(bf16)
        ys_ref[...] = (_dot(hid, w2_ref[...]) * rw_ref[...]).astype(bf16)

    @pl.when(i >= na_ref[0])
    def _():
        ys_ref[...] = jnp.zeros_like(ys_ref)


def _moe(l, te, na, xs, w1, w3, w2, rw):
    p, d = xs.shape
    tm = MOE_TM
    return pl.pallas_call(
        _moe_kernel,
        grid_spec=pltpu.PrefetchScalarGridSpec(
            num_scalar_prefetch=3, grid=(p // tm,),
            in_specs=[
                pl.BlockSpec((tm, d), lambda i, l, te, na: (i, 0)),
                pl.BlockSpec((None, None, d, D_EXPERT), lambda i, l, te, na: (l[0], te[i], 0, 0)),
                pl.BlockSpec((None, None, d, D_EXPERT), lambda i, l, te, na: (l[0], te[i], 0, 0)),
                pl.BlockSpec((None, None, D_EXPERT, d), lambda i, l, te, na: (l[0], te[i], 0, 0)),
                pl.BlockSpec((tm, 1), lambda i, l, te, na: (i, 0)),
            ],
            out_specs=pl.BlockSpec((tm, d), lambda i, l, te, na: (i, 0))),
        out_shape=jax.ShapeDtypeStruct((p, d), bf16),
        compiler_params=_cparams(("arbitrary",)),
        name="moe",
    )(l, te, na, xs, w1, w3, w2, rw)


def _route(eid, ew, n_tiles):
    tm = MOE_TM
    p = n_tiles * tm
    e_flat = eid[:, :EXP_TOPK_CONST].reshape(-1)
    w_flat = ew[:, :EXP_TOPK_CONST].reshape(-1)
    na_all = e_flat.shape[0]
    ex = jnp.arange(N_EXPERTS, dtype=i32)[:, None]
    onehot = ex == e_flat[None, :]
    counts = jnp.sum(onehot.astype(i32), axis=1)
    pc = ((counts + tm - 1) // tm) * tm
    pend = jnp.cumsum(pc)
    po = pend - pc
    co = jnp.cumsum(counts) - counts
    order = jnp.argsort(e_flat, stable=True).astype(i32)
    r = jnp.arange(p, dtype=i32)
    e_r = jnp.minimum(jnp.sum((r[None, :] >= pend[:, None]).astype(i32), axis=0), N_EXPERTS - 1)
    sel = ex == e_r[None, :]
    pick = lambda v: jnp.sum(jnp.where(sel, v[:, None], 0), axis=0)
    local = r - pick(po)
    valid_r = (local < pick(counts)) & (r < pend[-1])
    a_r = order[jnp.clip(pick(co) + local, 0, na_all - 1)]
    tok_r = jnp.where(valid_r, a_r // EXP_TOPK_CONST, 0)
    w_r = jnp.where(valid_r, w_flat[a_r], 0.0)
    n_act = (pend[-1] // tm).astype(i32)
    tile_e = e_r[::tm]
    te = jnp.where(jnp.arange(n_tiles, dtype=i32) < n_act, tile_e, tile_e[jnp.maximum(n_act - 1, 0)])
    inv = jnp.argsort(order).astype(i32)
    pos_a = jnp.sum(jnp.where(onehot, (po - co)[:, None], 0), axis=0) + inv
    return tok_r, w_r[:, None], te, n_act.reshape(1), pos_a


def _final_kernel(hmid_ref, y2_ref, out_ref):
    d = hmid_ref.shape[1]
    out_ref[...] = hmid_ref[...] + y2_ref[:, :d].astype(f32) + y2_ref[:, d:].astype(f32)


def _final(hmid, y2):
    n, d = hmid.shape
    tm = _row_tile(n, 512)
    return pl.pallas_call(
        _final_kernel, grid=(n // tm,),
        in_specs=[pl.BlockSpec((tm, d), lambda i: (i, 0)), pl.BlockSpec((tm, 2 * d), lambda i: (i, 0))],
        out_specs=pl.BlockSpec((tm, d), lambda i: (i, 0)),
        out_shape=jax.ShapeDtypeStruct((n, d), f32),
        compiler_params=_cparams(("parallel",)),
        name="final",
    )(hmid, y2)


def _rope_tables(tp):
    pos = (jnp.arange(tp, dtype=f32) - FRONT)[:, None]

    def tab(dim, reps):
        inv = 1.0 / (ROPE_THETA ** (jnp.arange(0, dim, 2, dtype=f32) / dim))
        ang = pos * inv[None, :]
        return jnp.tile(jnp.cos(ang), (1, reps)), jnp.tile(jnp.sin(ang), (1, reps))

    cos, sin = tab(HEAD_DIM, 2)
    icos, isin = tab(IDX_DIM, 4)
    return cos, sin, icos, isin


def _split_w_in(w_in):
    parts, off = [], 0
    for s in IN_SPLITS:
        parts.append(w_in[..., off:off + s])
        off += s
    return parts


def kernel(x, meta, norm_mix_g, norm_ffn_g, w_in, a_qn_g, a_kn_g, b_qn_g, b_kn_g, b_f_bias,
           c_gate_w2, c_gate_b, c_on_g, w_out, r_group_w, r_group_b, r_exp_w, r_exp_b,
           e_w1, e_w3, e_w2):
    bsz, n_seq, d = x.shape
    depth = w_in.shape[0]
    t_real = n_seq + N_META
    k_top = min(TOPK_MAX, n_seq // 4)
    tp = -(-(FRONT + t_real) // BLK) * BLK
    n = bsz * tp

    (waq, wak, wav, wiq, wik, wiw, wbq, wbk, wbv, wbf, wcq, wck, wcv, wcr, wcg) = _split_w_in(w_in)
    zc = lambda w: jnp.zeros((depth, d, w), w_in.dtype)
    wcat = jnp.concatenate([waq, wak, wav, wiq, wik, zc(LANES - IDX_DIM), zc(LANES),
                            wbq, wbk, wbv, wcq, wck, wcv, wcg], axis=-1).astype(bf16)
    ws = jnp.concatenate([wiw, wbf, wcr, zc(LANES - S_CR - C_GATE_RANK)], axis=-1).astype(bf16)
    wo = w_out.astype(bf16)
    wr = jnp.concatenate([r_group_w, zc(R_EXP - N_GROUPS), r_exp_w, zc(LANES - R_EXP - N_EXPERTS)],
                         axis=-1).astype(bf16)
    zl = lambda w: jnp.zeros((depth, w), f32)
    br = jnp.concatenate([r_group_b, zl(R_EXP - N_GROUPS), r_exp_b, zl(LANES - R_EXP - N_EXPERTS)],
                         axis=-1)[:, None, :]
    fbias = jnp.concatenate([zl(S_BF), b_f_bias, zl(LANES - S_BF - B_HEADS)], axis=-1)[:, None, :]
    dk = C_HEADS * C_DK
    w2p = jnp.concatenate([jnp.zeros((depth, S_CR, dk), f32), c_gate_w2,
                           jnp.zeros((depth, LANES - S_CR - C_GATE_RANK, dk), f32)], axis=1).astype(bf16)
    gb = c_gate_b[:, None, :]
    og = c_on_g[:, None, :]
    g_mix = norm_mix_g[:, None, :]
    g_ffn = norm_ffn_g[:, None, :]
    gaq, gak, gbq, gbk = (g[:, None, :] for g in (a_qn_g, a_kn_g, b_qn_g, b_kn_g))
    ew1, ew3, ew2 = e_w1.astype(bf16), e_w3.astype(bf16), e_w2.astype(bf16)
    emat = (jnp.arange(dk, dtype=i32)[:, None] // C_DK
            == jnp.arange(C_WIDTH, dtype=i32)[None, :] // C_DV).astype(bf16)
    tabs = _rope_tables(tp)

    h0 = jnp.concatenate([
        jnp.zeros((bsz, FRONT, d), f32),
        jnp.broadcast_to(meta[None].astype(f32), (bsz, N_META, d)),
        x.astype(f32),
        jnp.zeros((bsz, tp - FRONT - t_real, d), f32)], axis=1).reshape(n, d)
    y0 = jnp.zeros((n, 2 * d), bf16)
    n_tiles = -(-(EXP_TOPK_CONST * n + N_EXPERTS * (MOE_TM - 1)) // MOE_TM)

    def layer(li, carry):
        hmid, y2 = carry
        l = jnp.reshape(li, (1,)).astype(i32)
        h, zcat, zs = _inproj(l, hmid, y2, g_mix, wcat, ws)
        zcat3 = zcat.reshape(bsz, tp, N_CAT)
        zs3 = zs.reshape(bsz, tp, LANES)
        akr, ikr, bka = _kprep(l, zcat3, zs3, tabs, gak, gbk, fbias, t_real)
        oa = _dsa(l, zcat3, zs3, akr, ikr, tabs, gaq, t_real, k_top)
        ob = _fattn(l, zcat3, bka, gbq)
        oc = _gla(l, zcat3, zs3, w2p, gb, og, emat, t_real)
        hmid2, u, eid, ew = _outproj(l, oa.reshape(n, A_WIDTH), ob.reshape(n, B_WIDTH),
                                     oc.reshape(n, C_WIDTH), h, wo, g_ffn, wr, br, t_real, tp)
        tok_r, w_r, te, n_act, pos_a = _route(eid, ew, n_tiles)
        ys = _moe(l, te, n_act, jnp.take(u, tok_r, axis=0), ew1, ew3, ew2, w_r)
        return hmid2, jnp.take(ys, pos_a, axis=0).reshape(n, 2 * d)

    hmid, y2 = lax.fori_loop(0, depth, layer, (h0, y0))
    out = _final(hmid, y2).reshape(bsz, tp, d)
    return out[:, FRONT + N_META:FRONT + t_real].astype(x.dtype)
```

```python
import functools

import jax
import jax.numpy as jnp
from jax import lax
from jax.experimental import pallas as pl
from jax.experimental.pallas import tpu as pltpu

f32 = jnp.float32
bf16 = jnp.bfloat16
i32 = jnp.int32

D_MODEL = 2048
CHUNK = 64
N_META = 16
ROPE_THETA = 10000.0
EPS = 1e-6
HEAD_DIM = 128
A_HEADS = 6
IDX_HEADS = 16
IDX_DIM = 64
TOPK_MAX = 256
B_HEADS = 6
C_HEADS = 4
C_DK = 64
C_DV = 128
C_GATE_RANK = 16
C_TAU = 16.0
N_GROUPS = 4
EXP_PER_GROUP = 8
N_EXPERTS = N_GROUPS * EXP_PER_GROUP
D_EXPERT = 512
A_WIDTH = A_HEADS * HEAD_DIM
B_WIDTH = B_HEADS * HEAD_DIM
C_WIDTH = C_HEADS * C_DV
IN_SPLITS = (A_WIDTH, HEAD_DIM, HEAD_DIM, IDX_HEADS * IDX_DIM, IDX_DIM, IDX_HEADS,
             B_WIDTH, B_WIDTH, B_WIDTH, B_HEADS,
             C_HEADS * C_DK, C_HEADS * C_DK, C_WIDTH, C_GATE_RANK, C_WIDTH)

LANES = 128
FRONT = (-N_META) % CHUNK
BLK = 128
T_AQ, T_AK, T_AV, T_IQ, T_IK = 0, 6, 7, 8, 16
T_BQ, T_BK, T_BV = 18, 24, 30
T_CQ, T_CK, T_CV, T_CG = 36, 38, 40, 44
N_CAT = 48 * LANES
S_IW, S_BF, S_CR = 0, 16, 22
R_GRP, R_EXP = 0, 32
GLA_C = 32
DSA_G = 3
MOE_TM = 256
EXP_TOPK_CONST = 2
NEG = -1e30
LOG2E = 1.4426950408889634
INT_MIN = -2 ** 31
VMEM_LIMIT = 56 * 1024 * 1024


def _cparams(sem):
    return pltpu.CompilerParams(dimension_semantics=sem, vmem_limit_bytes=VMEM_LIMIT)


def _row_tile(n, cap):
    t = cap
    while n % t:
        t //= 2
    return t


def _log_sigmoid(x):
    return jnp.minimum(x, 0.0) - jnp.log(1.0 + jnp.exp(-jnp.abs(x)))


def _dot(a, b):
    return jnp.dot(a, b, preferred_element_type=f32)


def _dot_nt(a, b):
    return lax.dot_general(a, b, (((1,), (1,)), ((), ())), preferred_element_type=f32)


def _dot_tn(a, b):
    return lax.dot_general(a, b, (((0,), (0,)), ((), ())), preferred_element_type=f32)


def _split3(x):
    hi = x.astype(bf16)
    r1 = x - hi.astype(f32)
    mid = r1.astype(bf16)
    lo = (r1 - mid.astype(f32)).astype(bf16)
    return hi, mid, lo


def _rope128(x, cos, sin, lane):
    return x * cos + pltpu.roll(x, 64, 1) * jnp.where(lane < 64, -sin, sin)


def _rope64(x, cos, sin, lane):
    low = (lane & 63) < 32
    return (x * cos + pltpu.roll(x, 32, 1) * jnp.where(low, 0.0, sin)
            + pltpu.roll(x, 96, 1) * jnp.where(low, -sin, 0.0))


def _rms_gain(x, g):
    return x * lax.rsqrt(jnp.mean(x * x, axis=-1, keepdims=True) + EPS) * g


def _inproj_kernel(l_ref, hmid_ref, ya_ref, yb_ref, g_ref, w_ref, ws_ref, h_ref, z_ref, zs_ref, xn_ref):
    j = pl.program_id(1)

    @pl.when(j == 0)
    def _():
        h = hmid_ref[...] + ya_ref[...].astype(f32) + yb_ref[...].astype(f32)
        h_ref[...] = h
        xn = _rms_gain(h, g_ref[...]).astype(bf16)
        xn_ref[...] = xn
        zs_ref[...] = _dot(xn, ws_ref[...])

    z_ref[...] = _dot(xn_ref[...], w_ref[...]).astype(bf16)


def _inproj(l, hmid, y2, gain, wcat, ws):
    n, d = hmid.shape
    tm = _row_tile(n, 512)
    tn = 512
    grid = (n // tm, N_CAT // tn)
    nt = n // tm
    return pl.pallas_call(
        _inproj_kernel,
        grid_spec=pltpu.PrefetchScalarGridSpec(
            num_scalar_prefetch=1, grid=grid,
            in_specs=[
                pl.BlockSpec((tm, d), lambda i, j, l: (i, 0)),
                pl.BlockSpec((tm, d), lambda i, j, l: (i, 0)),
                pl.BlockSpec((tm, d), lambda i, j, l: (i + nt, 0)),
                pl.BlockSpec((None, 1, d), lambda i, j, l: (l[0], 0, 0)),
                pl.BlockSpec((None, d, tn), lambda i, j, l: (l[0], 0, j)),
                pl.BlockSpec((None, d, LANES), lambda i, j, l: (l[0], 0, 0)),
            ],
            out_specs=[
                pl.BlockSpec((tm, d), lambda i, j, l: (i, 0)),
                pl.BlockSpec((tm, tn), lambda i, j, l: (i, j)),
                pl.BlockSpec((tm, LANES), lambda i, j, l: (i, 0)),
            ],
            scratch_shapes=[pltpu.VMEM((tm, d), bf16)]),
        out_shape=[jax.ShapeDtypeStruct((n, d), f32),
                   jax.ShapeDtypeStruct((n, N_CAT), bf16),
                   jax.ShapeDtypeStruct((n, LANES), f32)],
        compiler_params=_cparams(("parallel", "arbitrary")),
        name="inproj",
    )(l, hmid, y2, y2, gain, wcat, ws)


def _kprep_kernel(l_ref, ak_ref, ik_ref, bk_ref, zs_ref,
                  cos_ref, sin_ref, icos_ref, isin_ref, gak_ref, gbk_ref, fb_ref,
                  akr_ref, ikr_ref, bka_ref, carry_ref, *, t_real):
    k = pl.program_id(1)
    lane = lax.broadcasted_iota(i32, (BLK, LANES), 1)
    row = lax.broadcasted_iota(i32, (BLK, LANES), 0)
    pos = k * BLK + row
    valid = (pos >= FRONT) & (pos < FRONT + t_real)

    x = ak_ref[...].astype(f32)
    akr_ref[...] = _rope128(_rms_gain(x, gak_ref[...]), cos_ref[...], sin_ref[...], lane).astype(bf16)
    ikr_ref[...] = _rope64(ik_ref[...].astype(f32), icos_ref[...], isin_ref[...], lane).astype(bf16)

    @pl.when(k == 0)
    def _():
        carry_ref[...] = jnp.zeros_like(carry_ref)

    lf = jnp.where(valid, _log_sigmoid(zs_ref[...] + fb_ref[...]), 0.0)
    tri = (row >= lane).astype(bf16)
    hi, mid, lo = _split3(lf)
    fcum = _dot(tri, hi) + _dot(tri, mid) + _dot(tri, lo) + carry_ref[...]
    carry_ref[...] = fcum[BLK - 1:BLK, :]

    fs = jnp.where(valid, fcum * (-(HEAD_DIM ** 0.5)), NEG)
    for h in range(B_HEADS):
        sl = slice(h * LANES, (h + 1) * LANES)
        bka_ref[h, :, 0:LANES] = _rms_gain(bk_ref[:, sl].astype(f32), gbk_ref[...]).astype(bf16)
        p0, p1, p2 = _split3(fs[:, S_BF + h:S_BF + h + 1])
        aug = jnp.where(lane == 0, p0.astype(f32),
                        jnp.where(lane == 1, p1.astype(f32), jnp.where(lane == 2, p2.astype(f32), 0.0)))
        bka_ref[h, :, LANES:2 * LANES] = aug.astype(bf16)


def _kprep(l, zcat3, zs3, tabs, gak, gbk, fbias, t_real):
    bsz, tp, _ = zcat3.shape
    nkb = tp // BLK
    cos, sin, icos, isin = tabs
    tile = lambda c: pl.BlockSpec((None, BLK, LANES), lambda b, k, l, c=c: (b, k, c))
    wide = lambda c: pl.BlockSpec((None, BLK, B_WIDTH), lambda b, k, l, c=c: (b, k, c))
    tab = pl.BlockSpec((BLK, LANES), lambda b, k, l: (k, 0))
    gain = pl.BlockSpec((None, 1, LANES), lambda b, k, l: (l[0], 0, 0))
    return pl.pallas_call(
        functools.partial(_kprep_kernel, t_real=t_real),
        grid_spec=pltpu.PrefetchScalarGridSpec(
            num_scalar_prefetch=1, grid=(bsz, nkb),
            in_specs=[tile(T_AK), tile(T_IK),
                      wide(T_BK * LANES // B_WIDTH),
                      pl.BlockSpec((None, BLK, LANES), lambda b, k, l: (b, k, 0)),
                      tab, tab, tab, tab, gain, gain, gain],
            out_specs=[
                pl.BlockSpec((None, BLK, LANES), lambda b, k, l: (b, k, 0)),
                pl.BlockSpec((None, BLK, LANES), lambda b, k, l: (b, k, 0)),
                pl.BlockSpec((None, B_HEADS, BLK, 2 * LANES), lambda b, k, l: (b, 0, k, 0)),
            ],
            scratch_shapes=[pltpu.VMEM((1, LANES), f32)]),
        out_shape=[jax.ShapeDtypeStruct((bsz, tp, LANES), bf16),
                   jax.ShapeDtypeStruct((bsz, tp, LANES), bf16),
                   jax.ShapeDtypeStruct((bsz, B_HEADS, tp, 2 * LANES), bf16)],
        compiler_params=_cparams(("parallel", "arbitrary")),
        name="kprep",
    )(l, zcat3, zcat3, zcat3, zs3, cos, sin, icos, isin, gak, gbk, fbias)


def _dsa_kernel(l_ref, aq_ref, iq_ref, zs_ref, akr_ref, ikr_ref, av_ref,
                cos_ref, sin_ref, icos_ref, isin_ref, gq_ref, out_ref,
                key_ref, iqs_ref, q6_ref, iwt_ref, m_ref, s_ref, acc_ref, *, t_real, k_top):
    i = pl.program_id(1)
    nk = i + 1
    lane = lax.broadcasted_iota(i32, (BLK, LANES), 1)

    for h in range(A_HEADS):
        x = aq_ref[:, h * LANES:(h + 1) * LANES].astype(f32)
        xr = _rope128(_rms_gain(x, gq_ref[...]), cos_ref[...], sin_ref[...], lane)
        q6_ref[h * BLK:(h + 1) * BLK, :] = xr.astype(bf16)
    for t in range(IDX_HEADS // 2):
        x = iq_ref[:, t * LANES:(t + 1) * LANES].astype(f32)
        xr = _rope64(x, icos_ref[...], isin_ref[...], lane)
        iqs_ref[(2 * t) * BLK:(2 * t + 1) * BLK, :] = jnp.where(lane < IDX_DIM, xr, 0.0).astype(bf16)
        iqs_ref[(2 * t + 1) * BLK:(2 * t + 2) * BLK, :] = jnp.where(
            lane < IDX_DIM, pltpu.roll(xr, 64, 1), 0.0).astype(bf16)
    iwt_ref[...] = (zs_ref[...] * (IDX_HEADS ** -0.5 * IDX_DIM ** -0.5)).T

    gb = DSA_G * BLK
    ng = lax.div(nk, jnp.int32(DSA_G))
    nr = nk - ng * DSA_G

    def over_keys(fn, init):
        c = lax.fori_loop(0, ng, lambda g, c: fn(pl.multiple_of(g * gb, gb), gb, c), init)
        return lax.fori_loop(0, nr, lambda r, c: fn(pl.multiple_of((ng * DSA_G + r) * BLK, BLK), BLK, c), c)

    def score_rows(k0, nrows, carry):
        dt = _dot_nt(ikr_ref[pl.ds(k0, nrows), :], iqs_ref[...])
        s = jnp.zeros((nrows, LANES), f32)
        for h in range(IDX_HEADS):
            s = s + iwt_ref[h:h + 1, :] * jnp.maximum(dt[:, h * LANES:(h + 1) * LANES], 0.0)
        kpos = k0 + lax.broadcasted_iota(i32, (nrows, LANES), 0)
        qpos = i * BLK + lax.broadcasted_iota(i32, (nrows, LANES), 1)
        adm = ((kpos >> 6) <= (qpos >> 6)) & (kpos >= FRONT) & (kpos < FRONT + t_real)
        bits = lax.bitcast_convert_type(s, i32)
        key = bits ^ ((bits >> 31) & 0x7FFFFFFF)
        key_ref[pl.ds(k0, nrows), :] = jnp.where(adm, key, INT_MIN)
        return carry

    over_keys(score_rows, 0)

    def bit_body(t, thr_u):
        bit = jnp.left_shift(jnp.int32(1), 31 - t)
        cand_u = thr_u | bit
        cand_s = cand_u ^ INT_MIN

        def count_rows(k0, nrows, c):
            hit = (key_ref[pl.ds(k0, nrows), :] >= cand_s).astype(i32)
            for j in range(nrows // BLK):
                c = c + hit[j * BLK:(j + 1) * BLK, :]
            return c

        cnt = over_keys(count_rows, jnp.zeros((BLK, LANES), i32))
        tot = jnp.sum(cnt.astype(f32), axis=0, keepdims=True)
        return jnp.where(tot >= k_top, cand_u, thr_u)

    thr_u = lax.fori_loop(0, 32, bit_body, jnp.zeros((1, LANES), i32))
    thr_s = jnp.maximum(thr_u ^ INT_MIN, INT_MIN + 1)

    m_ref[...] = jnp.full(m_ref.shape, NEG, f32)
    s_ref[...] = jnp.zeros(s_ref.shape, f32)
    acc_ref[...] = jnp.zeros(acc_ref.shape, f32)
    c2 = HEAD_DIM ** -0.5 * LOG2E

    def attn_rows(k0, nrows, carry):
        st = _dot_nt(akr_ref[pl.ds(k0, nrows), :], q6_ref[...])
        bias = jnp.where(key_ref[pl.ds(k0, nrows), :] >= thr_s, 0.0, NEG)
        ps, alphas = [], []
        for h in range(A_HEADS):
            sl = slice(h * LANES, (h + 1) * LANES)
            sh = st[:, sl] + bias
            m_old = m_ref[:, sl]
            m_new = jnp.maximum(m_old, jnp.max(sh, axis=0, keepdims=True))
            alpha = jnp.exp2((m_old - m_new) * c2)
            p = jnp.exp2((sh - m_new) * c2)
            s_ref[:, sl] = s_ref[:, sl] * alpha + jnp.sum(p, axis=0, keepdims=True)
            m_ref[:, sl] = m_new
            alphas.append(alpha)
            ps.append(p.astype(bf16))
        pv = _dot_tn(av_ref[pl.ds(k0, nrows), :], jnp.concatenate(ps, axis=1))
        acc_ref[...] = acc_ref[...] * jnp.concatenate(alphas, axis=1) + pv
        return carry

    over_keys(attn_rows, 0)

    for h in range(A_HEADS):
        sl = slice(h * LANES, (h + 1) * LANES)
        o = acc_ref[:, sl] / jnp.maximum(s_ref[:, sl], 1e-30)
        out_ref[:, sl] = o.T.astype(bf16)


def _dsa(l, zcat3, zs3, akr, ikr, tabs, gaq, t_real, k_top):
    bsz, tp, _ = zcat3.shape
    nkb = tp // BLK
    cos, sin, icos, isin = tabs
    tab = pl.BlockSpec((BLK, LANES), lambda b, i, l: (i, 0))
    full = pl.BlockSpec((None, tp, LANES), lambda b, i, l: (b, 0, 0))
    return pl.pallas_call(
        functools.partial(_dsa_kernel, t_real=t_real, k_top=k_top),
        grid_spec=pltpu.PrefetchScalarGridSpec(
            num_scalar_prefetch=1, grid=(bsz, nkb),
            in_specs=[
                pl.BlockSpec((None, BLK, A_WIDTH), lambda b, i, l: (b, i, 0)),
                pl.BlockSpec((None, BLK, IDX_HEADS * IDX_DIM), lambda b, i, l: (b, i, T_IQ * LANES // (IDX_HEADS * IDX_DIM))),
                pl.BlockSpec((None, BLK, LANES), lambda b, i, l: (b, i, 0)),
                full, full,
                pl.BlockSpec((None, tp, LANES), lambda b, i, l: (b, 0, T_AV)),
                tab, tab, tab, tab,
                pl.BlockSpec((None, 1, LANES), lambda b, i, l: (l[0], 0, 0)),
            ],
            out_specs=pl.BlockSpec((None, BLK, A_WIDTH), lambda b, i, l: (b, i, 0)),
            scratch_shapes=[
                pltpu.VMEM((tp, LANES), i32),
                pltpu.VMEM((IDX_HEADS * BLK, LANES), bf16),
                pltpu.VMEM((A_HEADS * BLK, LANES), bf16),
                pltpu.VMEM((LANES, BLK), f32),
                pltpu.VMEM((1, A_HEADS * BLK), f32),
                pltpu.VMEM((1, A_HEADS * BLK), f32),
                pltpu.VMEM((HEAD_DIM, A_HEADS * BLK), f32),
            ]),
        out_shape=jax.ShapeDtypeStruct((bsz, tp, A_WIDTH), bf16),
        compiler_params=_cparams(("parallel", "arbitrary")),
        name="dsa",
    )(l, zcat3, zcat3, zs3, akr, ikr, zcat3, cos, sin, icos, isin, gaq)


def _fattn_kernel(l_ref, bq_ref, bka_ref, bv_ref, gq_ref, out_ref, qa_ref, m_ref, s_ref, acc_ref):
    i = pl.program_id(1)
    fb = bq_ref.shape[0]
    ones3 = jnp.where(lax.broadcasted_iota(i32, (fb, LANES), 1) < 3, 1.0, 0.0).astype(bf16)
    for h in range(B_HEADS):
        sl = slice(h * LANES, (h + 1) * LANES)
        qa_ref[h, :, 0:LANES] = _rms_gain(bq_ref[:, sl].astype(f32), gq_ref[...]).astype(bf16)
        qa_ref[h, :, LANES:2 * LANES] = ones3
    m_ref[...] = jnp.full(m_ref.shape, NEG, f32)
    s_ref[...] = jnp.zeros(s_ref.shape, f32)
    acc_ref[...] = jnp.zeros(acc_ref.shape, f32)
    c2 = HEAD_DIM ** -0.5 * LOG2E
    causal = lax.broadcasted_iota(i32, (fb, fb), 0) <= lax.broadcasted_iota(i32, (fb, fb), 1)

    def step(kb, diagonal):
        k0 = pl.multiple_of(kb * fb, fb)
        for h in range(B_HEADS):
            sl = slice(h * LANES, (h + 1) * LANES)
            st = _dot_nt(bka_ref[h, pl.ds(k0, fb), :], qa_ref[h])
            if diagonal:
                st = jnp.where(causal, st, NEG)
            m_old = m_ref[h]
            m_new = jnp.maximum(m_old, jnp.max(st, axis=0, keepdims=True))
            alpha = jnp.exp2((m_old - m_new) * c2)
            p = jnp.exp2((st - m_new) * c2)
            s_ref[h] = s_ref[h] * alpha + jnp.sum(p, axis=0, keepdims=True)
            m_ref[h] = m_new
            acc_ref[h] = acc_ref[h] * alpha + _dot_tn(bv_ref[pl.ds(k0, fb), sl], p.astype(bf16))

    def body(kb, carry):
        step(kb, False)
        return carry

    lax.fori_loop(0, i, body, 0)
    step(i, True)
    for h in range(B_HEADS):
        sl = slice(h * LANES, (h + 1) * LANES)
        out_ref[:, sl] = (acc_ref[h] / s_ref[h]).T.astype(bf16)


def _fattn(l, zcat3, bka, gbq):
    bsz, tp, _ = zcat3.shape
    fb = _seq_tile(tp, 384)
    once = pl.Buffered(1)
    return pl.pallas_call(
        _fattn_kernel,
        grid_spec=pltpu.PrefetchScalarGridSpec(
            num_scalar_prefetch=1, grid=(bsz, tp // fb),
            in_specs=[
                pl.BlockSpec((None, fb, B_WIDTH), lambda b, i, l: (b, i, T_BQ * LANES // B_WIDTH)),
                pl.BlockSpec((None, B_HEADS, tp, 2 * LANES), lambda b, i, l: (b, 0, 0, 0), pipeline_mode=once),
                pl.BlockSpec((None, tp, B_WIDTH), lambda b, i, l: (b, 0, T_BV * LANES // B_WIDTH),
                             pipeline_mode=once),
                pl.BlockSpec((None, 1, LANES), lambda b, i, l: (l[0], 0, 0)),
            ],
            out_specs=pl.BlockSpec((None, fb, B_WIDTH), lambda b, i, l: (b, i, 0)),
            scratch_shapes=[
                pltpu.VMEM((B_HEADS, fb, 2 * LANES), bf16),
                pltpu.VMEM((B_HEADS, 1, fb), f32),
                pltpu.VMEM((B_HEADS, 1, fb), f32),
                pltpu.VMEM((B_HEADS, HEAD_DIM, fb), f32),
            ]),
        out_shape=jax.ShapeDtypeStruct((bsz, tp, B_WIDTH), bf16),
        compiler_params=_cparams(("parallel", "arbitrary")),
        name="fattn",
    )(l, zcat3, bka, zcat3, gbq)


def _gla_kernel(l_ref, cq_ref, ck_ref, cv_ref, cg_ref, zs_ref, w2_ref, gb_ref, og_ref, e_ref,
                out_ref, st_ref, oi_ref, *, t_real):
    c = pl.program_id(0)
    nb, cc, dk = cq_ref.shape

    @pl.when(c == 0)
    def _():
        st_ref[...] = jnp.zeros_like(st_ref)

    rowc = lax.broadcasted_iota(i32, (cc, 1), 0)
    pos = c * cc + rowc
    valid = (pos >= FRONT) & (pos < FRONT + t_real)
    tri = (lax.broadcasted_iota(i32, (cc, cc), 0) >= lax.broadcasted_iota(i32, (cc, cc), 1)).astype(bf16)
    lane_k = lax.broadcasted_iota(i32, (1, dk), 1)
    head_masks = [(lane_k >= h * C_DK) & (lane_k < (h + 1) * C_DK) for h in range(C_HEADS)]

    for b in range(nb):
        x = _dot(zs_ref[b].astype(bf16), w2_ref[...]) + gb_ref[...]
        la = jnp.where(valid, _log_sigmoid(x) * (1.0 / C_TAU), 0.0)
        hi, mid, lo = _split3(la)
        bc = _dot(tri, hi) + _dot(tri, mid) + _dot(tri, lo)
        q = cq_ref[b].astype(f32) * (C_DK ** -0.5)
        k = ck_ref[b].astype(f32)
        v = cv_ref[b]
        vf = v.astype(f32)
        blast = bc[cc - 1:cc, :]
        qe = q * jnp.exp(bc)
        ke = k * jnp.exp(blast - bc)
        st = st_ref[b]
        stb = st.astype(bf16)

        o_inter = jnp.concatenate(
            [_dot_nt(jnp.where(head_masks[h], qe, 0.0).astype(bf16), stb) for h in range(C_HEADS)], axis=1)

        rows = []
        for r in range(cc):
            dec = jnp.exp(jnp.minimum(bc[r:r + 1, :] - bc, 0.0))
            rows.append(jnp.where(rowc <= r, q[r:r + 1, :] * k * dec, 0.0).astype(bf16))
        rr = _dot(jnp.concatenate(rows, axis=0), e_ref[...])
        for r in range(cc):
            oi_ref[r:r + 1, :] = jnp.sum(rr[r * cc:(r + 1) * cc, :] * vf, axis=0, keepdims=True)
        o = o_inter + oi_ref[...]

        new_st = st * jnp.exp(blast)
        for h in range(C_HEADS):
            km = jnp.where(head_masks[h], ke, 0.0).astype(bf16)
            new_st = new_st + _dot_tn(v[:, h * C_DV:(h + 1) * C_DV], km)
        st_ref[b] = new_st

        g = cg_ref[b].astype(f32)
        gs = g * (1.0 / (1.0 + jnp.exp(-g)))
        for h in range(C_HEADS):
            sl = slice(h * C_DV, (h + 1) * C_DV)
            out_ref[b, :, sl] = (_rms_gain(o[:, sl], og_ref[...]) * gs[:, sl]).astype(bf16)


def _gla(l, zcat3, zs3, w2p, gb, og, emat, t_real):
    bsz, tp, _ = zcat3.shape
    cc = GLA_C
    dk = C_HEADS * C_DK
    return pl.pallas_call(
        functools.partial(_gla_kernel, t_real=t_real),
        grid_spec=pltpu.PrefetchScalarGridSpec(
            num_scalar_prefetch=1, grid=(tp // cc,),
            in_specs=[
                pl.BlockSpec((bsz, cc, dk), lambda c, l: (0, c, T_CQ * LANES // dk)),
                pl.BlockSpec((bsz, cc, dk), lambda c, l: (0, c, T_CK * LANES // dk)),
                pl.BlockSpec((bsz, cc, C_WIDTH), lambda c, l: (0, c, T_CV * LANES // C_WIDTH)),
                pl.BlockSpec((bsz, cc, C_WIDTH), lambda c, l: (0, c, T_CG * LANES // C_WIDTH)),
                pl.BlockSpec((bsz, cc, LANES), lambda c, l: (0, c, 0)),
                pl.BlockSpec((None, LANES, dk), lambda c, l: (l[0], 0, 0)),
                pl.BlockSpec((None, 1, dk), lambda c, l: (l[0], 0, 0)),
                pl.BlockSpec((None, 1, C_DV), lambda c, l: (l[0], 0, 0)),
                pl.BlockSpec((dk, C_WIDTH), lambda c, l: (0, 0)),
            ],
            out_specs=pl.BlockSpec((bsz, cc, C_WIDTH), lambda c, l: (0, c, 0)),
            scratch_shapes=[pltpu.VMEM((bsz, C_DV, dk), f32), pltpu.VMEM((cc, C_WIDTH), f32)]),
        out_shape=jax.ShapeDtypeStruct((bsz, tp, C_WIDTH), bf16),
        compiler_params=_cparams(("arbitrary",)),
        name="gla",
    )(l, zcat3, zcat3, zcat3, zcat3, zs3, w2p, gb, og, emat)


def _outproj_kernel(l_ref, oa_ref, ob_ref, oc_ref, h_ref, wo_ref, g_ref, wr_ref, br_ref,
                    hmid_ref, u_ref, eid_ref, ew_ref, *, t_real):
    tm = h_ref.shape[0]
    mix = (_dot(oa_ref[...], wo_ref[0:A_WIDTH, :])
           + _dot(ob_ref[...], wo_ref[A_WIDTH:A_WIDTH + B_WIDTH, :])
           + _dot(oc_ref[...], wo_ref[A_WIDTH + B_WIDTH:, :]))
    pos = pl.program_id(1) * tm + lax.broadcasted_iota(i32, (tm, 1), 0)
    valid = (pos >= FRONT) & (pos < FRONT + t_real)
    hm = h_ref[...] + jnp.where(valid, mix, 0.0)
    hmid_ref[...] = hm
    u = _rms_gain(hm, g_ref[...]).astype(bf16)
    u_ref[...] = u

    logits = _dot(u, wr_ref[...]) + br_ref[...]
    lane = lax.broadcasted_iota(i32, (tm, LANES), 1)
    lanef = lane.astype(f32)
    big = float(4 * LANES)
    first = lambda hit: jnp.min(jnp.where(hit, lanef, big), axis=-1, keepdims=True).astype(i32)
    gl = jnp.where(lane < R_GRP + N_GROUPS, logits, -jnp.inf)
    gmax = jnp.max(gl, axis=-1, keepdims=True)
    g_p = 1.0 / jnp.sum(jnp.exp(gl - gmax), axis=-1, keepdims=True)
    g_i = first(gl == gmax)
    e_lane = lane - R_EXP
    emask = (e_lane >= 0) & (e_lane < N_EXPERTS) & ((e_lane >> 3) == g_i)
    el = jnp.where(emask, logits, -jnp.inf)
    m1 = jnp.max(el, axis=-1, keepdims=True)
    i1 = first(el == m1)
    el2 = jnp.where(lane == i1, -jnp.inf, el)
    m2 = jnp.max(el2, axis=-1, keepdims=True)
    i2 = first(el2 == m2)
    r = jnp.exp(m2 - m1)
    w1 = g_p / (1.0 + r)
    w2 = g_p * r / (1.0 + r)
    eid_ref[...] = jnp.where(lane == 0, i1 - R_EXP, jnp.where(lane == 1, i2 - R_EXP, 0))
    ew_ref[...] = jnp.where(lane == 0, w1, jnp.where(lane == 1, w2, 0.0))


def _seq_tile(tp, cap):
    return max(t for t in range(BLK, cap + 1, BLK) if tp % t == 0)


def _outproj(l, oa, ob, oc, h, wo, gain, wr, br, t_real, tp):
    n, d = h.shape
    tm = _seq_tile(tp, 384)
    nj = tp // tm
    row = lambda w: pl.BlockSpec((tm, w), lambda b, j, l: (b * nj + j, 0))
    return pl.pallas_call(
        functools.partial(_outproj_kernel, t_real=t_real),
        grid_spec=pltpu.PrefetchScalarGridSpec(
            num_scalar_prefetch=1, grid=(n // tp, nj),
            in_specs=[row(A_WIDTH), row(B_WIDTH), row(C_WIDTH), row(d),
                      pl.BlockSpec((None, d, d), lambda b, j, l: (l[0], 0, 0)),
                      pl.BlockSpec((None, 1, d), lambda b, j, l: (l[0], 0, 0)),
                      pl.BlockSpec((None, d, LANES), lambda b, j, l: (l[0], 0, 0)),
                      pl.BlockSpec((None, 1, LANES), lambda b, j, l: (l[0], 0, 0))],
            out_specs=[row(d), row(d), row(LANES), row(LANES)]),
        out_shape=[jax.ShapeDtypeStruct((n, d), f32), jax.ShapeDtypeStruct((n, d), bf16),
                   jax.ShapeDtypeStruct((n, LANES), i32), jax.ShapeDtypeStruct((n, LANES), f32)],
        compiler_params=_cparams(("parallel", "parallel")),
        name="outproj",
    )(l, oa, ob, oc, h, wo, gain, wr, br)


def _moe_kernel(l_ref, te_ref, na_ref, xs_ref, w1_ref, w3_ref, w2_ref, rw_ref, ys_ref,
                w1b_ref, w3b_ref, w2b_ref):
    i = pl.program_id(0)

    @pl.when((i == 0) | (te_ref[i] != te_ref[jnp.maximum(i - 1, 0)]))
    def _():
        w1b_ref[...] = w1_ref[...].astype(bf16)
        w3b_ref[...] = w3_ref[...].astype(bf16)
        w2b_ref[...] = w2_ref[...].astype(bf16)

    @pl.when(i < na_ref[0])
    def _():
        x = xs_ref[...]
        h1 = _dot(x, w1b_ref[...])
        h3 = _dot(x, w3b_ref[...])
        hid = (h1 * (1.0 / (1.0 + jnp.exp(-h1))) * h3).astype(bf16)
        ys_ref[...] = (_dot(hid, w2b_ref[...]) * rw_ref[...]).astype(bf16)

    @pl.when(i >= na_ref[0])
    def _():
        ys_ref[...] = jnp.zeros_like(ys_ref)


def _moe(l, te, na, xs, w1, w3, w2, rw):
    p, d = xs.shape
    tm = MOE_TM
    return pl.pallas_call(
        _moe_kernel,
        grid_spec=pltpu.PrefetchScalarGridSpec(
            num_scalar_prefetch=3, grid=(p // tm,),
            in_specs=[
                pl.BlockSpec((tm, d), lambda i, l, te, na: (i, 0)),
                pl.BlockSpec((None, None, d, D_EXPERT), lambda i, l, te, na: (l[0], te[i], 0, 0)),
                pl.BlockSpec((None, None, d, D_EXPERT), lambda i, l, te, na: (l[0], te[i], 0, 0)),
                pl.BlockSpec((None, None, D_EXPERT, d), lambda i, l, te, na: (l[0], te[i], 0, 0)),
                pl.BlockSpec((tm, 1), lambda i, l, te, na: (i, 0)),
            ],
            out_specs=pl.BlockSpec((tm, d), lambda i, l, te, na: (i, 0)),
            scratch_shapes=[pltpu.VMEM((d, D_EXPERT), bf16), pltpu.VMEM((d, D_EXPERT), bf16),
                            pltpu.VMEM((D_EXPERT, d), bf16)]),
        out_shape=jax.ShapeDtypeStruct((p, d), bf16),
        compiler_params=_cparams(("arbitrary",)),
        name="moe",
    )(l, te, na, xs, w1, w3, w2, rw)


def _route(eid, ew, n_tiles):
    tm = MOE_TM
    p = n_tiles * tm
    n = eid.shape[0]
    e_flat = eid[:, :EXP_TOPK_CONST].T.reshape(-1)
    w_flat = ew[:, :EXP_TOPK_CONST].T.reshape(-1)
    na_all = e_flat.shape[0]
    hp = lax.Precision.HIGHEST
    ex = jnp.arange(N_EXPERTS, dtype=i32)[:, None]
    onehot = (ex == e_flat[None, :]).astype(f32)
    counts = jnp.sum(onehot, axis=1).astype(i32)
    pc = ((counts + tm - 1) // tm) * tm
    pend = jnp.cumsum(pc)
    po = pend - pc
    co = jnp.cumsum(counts) - counts
    order = jnp.argsort(e_flat, stable=True).astype(i32)
    r = jnp.arange(p, dtype=i32)
    step = (r[None, :] >= pend[:, None]).astype(f32)
    dlt = lambda v: jnp.concatenate([v[1:] - v[:-1], jnp.zeros((1,), v.dtype)]).astype(f32)
    tabs = jnp.stack([jnp.ones((N_EXPERTS,), f32), dlt(po), dlt(counts), dlt(co)])
    picked = jnp.dot(tabs, step, precision=hp).astype(i32)
    e_r = jnp.minimum(picked[0], N_EXPERTS - 1)
    local = r - (po[0] + picked[1])
    valid_r = (local < counts[0] + picked[2]) & (r < pend[-1])
    a_r = order[jnp.clip(co[0] + picked[3] + local, 0, na_all - 1)]
    tok_r = jnp.where(valid_r, jnp.where(a_r >= n, a_r - n, a_r), r % n)
    w_r = jnp.where(valid_r, w_flat[a_r], 0.0)
    n_act = (pend[-1] // tm).astype(i32)
    tile_e = e_r[::tm]
    te = jnp.where(jnp.arange(n_tiles, dtype=i32) < n_act, tile_e, tile_e[jnp.maximum(n_act - 1, 0)])
    inv = jnp.argsort(order).astype(i32)
    pos_a = jnp.dot((po - co).astype(f32)[None, :], onehot, precision=hp)[0].astype(i32) + inv
    return tok_r, w_r[:, None], te, n_act.reshape(1), pos_a


def _final_kernel(hmid_ref, ya_ref, yb_ref, out_ref):
    out_ref[...] = hmid_ref[...] + ya_ref[...].astype(f32) + yb_ref[...].astype(f32)


def _final(hmid, y2):
    n, d = hmid.shape
    tm = _row_tile(n, 512)
    nt = n // tm
    return pl.pallas_call(
        _final_kernel, grid=(nt,),
        in_specs=[pl.BlockSpec((tm, d), lambda i: (i, 0)), pl.BlockSpec((tm, d), lambda i: (i, 0)),
                  pl.BlockSpec((tm, d), lambda i: (i + nt, 0))],
        out_specs=pl.BlockSpec((tm, d), lambda i: (i, 0)),
        out_shape=jax.ShapeDtypeStruct((n, d), f32),
        compiler_params=_cparams(("parallel",)),
        name="final",
    )(hmid, y2, y2)


def _rope_tables(tp):
    pos = (jnp.arange(tp, dtype=f32) - FRONT)[:, None]

    def tab(dim, reps):
        inv = 1.0 / (ROPE_THETA ** (jnp.arange(0, dim, 2, dtype=f32) / dim))
        ang = pos * inv[None, :]
        return jnp.tile(jnp.cos(ang), (1, reps)), jnp.tile(jnp.sin(ang), (1, reps))

    cos, sin = tab(HEAD_DIM, 2)
    icos, isin = tab(IDX_DIM, 4)
    return cos, sin, icos, isin


def _split_w_in(w_in):
    parts, off = [], 0
    for s in IN_SPLITS:
        parts.append(w_in[..., off:off + s])
        off += s
    return parts


def kernel(x, meta, norm_mix_g, norm_ffn_g, w_in, a_qn_g, a_kn_g, b_qn_g, b_kn_g, b_f_bias,
           c_gate_w2, c_gate_b, c_on_g, w_out, r_group_w, r_group_b, r_exp_w, r_exp_b,
           e_w1, e_w3, e_w2):
    bsz, n_seq, d = x.shape
    depth = w_in.shape[0]
    t_real = n_seq + N_META
    k_top = min(TOPK_MAX, n_seq // 4)
    tp = -(-(FRONT + t_real) // BLK) * BLK
    n = bsz * tp

    (waq, wak, wav, wiq, wik, wiw, wbq, wbk, wbv, wbf, wcq, wck, wcv, wcr, wcg) = _split_w_in(w_in)
    zc = lambda w: jnp.zeros((depth, d, w), w_in.dtype)
    wcat = jnp.concatenate([waq, wak, wav, wiq, wik, zc(LANES - IDX_DIM), zc(LANES),
                            wbq, wbk, wbv, wcq, wck, wcv, wcg], axis=-1).astype(bf16)
    ws = jnp.concatenate([wiw, wbf, wcr, zc(LANES - S_CR - C_GATE_RANK)], axis=-1).astype(bf16)
    wo = w_out.astype(bf16)
    wr = jnp.concatenate([r_group_w, zc(R_EXP - N_GROUPS), r_exp_w, zc(LANES - R_EXP - N_EXPERTS)],
                         axis=-1).astype(bf16)
    zl = lambda w: jnp.zeros((depth, w), f32)
    br = jnp.concatenate([r_group_b, zl(R_EXP - N_GROUPS), r_exp_b, zl(LANES - R_EXP - N_EXPERTS)],
                         axis=-1)[:, None, :]
    fbias = jnp.concatenate([zl(S_BF), b_f_bias, zl(LANES - S_BF - B_HEADS)], axis=-1)[:, None, :]
    dk = C_HEADS * C_DK
    w2p = jnp.concatenate([jnp.zeros((depth, S_CR, dk), f32), c_gate_w2,
                           jnp.zeros((depth, LANES - S_CR - C_GATE_RANK, dk), f32)], axis=1).astype(bf16)
    gb = c_gate_b[:, None, :]
    og = c_on_g[:, None, :]
    g_mix = norm_mix_g[:, None, :]
    g_ffn = norm_ffn_g[:, None, :]
    gaq, gak, gbq, gbk = (g[:, None, :] for g in (a_qn_g, a_kn_g, b_qn_g, b_kn_g))
    emat = (jnp.arange(dk, dtype=i32)[:, None] // C_DK
            == jnp.arange(C_WIDTH, dtype=i32)[None, :] // C_DV).astype(bf16)
    tabs = _rope_tables(tp)

    h0 = jnp.concatenate([
        jnp.zeros((bsz, FRONT, d), f32),
        jnp.broadcast_to(meta[None].astype(f32), (bsz, N_META, d)),
        x.astype(f32),
        jnp.zeros((bsz, tp - FRONT - t_real, d), f32)], axis=1).reshape(n, d)
    y0 = jnp.zeros((EXP_TOPK_CONST * n, d), bf16)
    n_tiles = -(-(EXP_TOPK_CONST * n + N_EXPERTS * (MOE_TM - 1)) // MOE_TM)

    def layer(li, carry):
        hmid, y2 = carry
        l = jnp.reshape(li, (1,)).astype(i32)
        h, zcat, zs = _inproj(l, hmid, y2, g_mix, wcat, ws)
        zcat3 = zcat.reshape(bsz, tp, N_CAT)
        zs3 = zs.reshape(bsz, tp, LANES)
        akr, ikr, bka = _kprep(l, zcat3, zs3, tabs, gak, gbk, fbias, t_real)
        oa = _dsa(l, zcat3, zs3, akr, ikr, tabs, gaq, t_real, k_top)
        ob = _fattn(l, zcat3, bka, gbq)
        oc = _gla(l, zcat3, zs3, w2p, gb, og, emat, t_real)
        hmid2, u, eid, ew = _outproj(l, oa.reshape(n, A_WIDTH), ob.reshape(n, B_WIDTH),
                                     oc.reshape(n, C_WIDTH), h, wo, g_ffn, wr, br, t_real, tp)
        tok_r, w_r, te, n_act, pos_a = _route(eid, ew, n_tiles)
        ys = _moe(l, te, n_act, jnp.take(u, tok_r, axis=0), e_w1, e_w3, e_w2, w_r)
        return hmid2, jnp.take(ys, pos_a, axis=0)

    hmid, y2 = lax.fori_loop(0, depth, layer, (h0, y0))
    out = _final(hmid, y2).reshape(bsz, tp, d)
    return out[:, FRONT + N_META:FRONT + t_real].astype(x.dtype)
```

```python
import functools

import jax
import jax.numpy as jnp
from jax import lax
from jax.experimental import pallas as pl
from jax.experimental.pallas import tpu as pltpu

f32 = jnp.float32
bf16 = jnp.bfloat16
i32 = jnp.int32

D_MODEL = 2048
CHUNK = 64
N_META = 16
ROPE_THETA = 10000.0
EPS = 1e-6
HEAD_DIM = 128
A_HEADS = 6
IDX_HEADS = 16
IDX_DIM = 64
TOPK_MAX = 256
B_HEADS = 6
C_HEADS = 4
C_DK = 64
C_DV = 128
C_GATE_RANK = 16
C_TAU = 16.0
N_GROUPS = 4
EXP_PER_GROUP = 8
N_EXPERTS = N_GROUPS * EXP_PER_GROUP
D_EXPERT = 512
A_WIDTH = A_HEADS * HEAD_DIM
B_WIDTH = B_HEADS * HEAD_DIM
C_WIDTH = C_HEADS * C_DV
IN_SPLITS = (A_WIDTH, HEAD_DIM, HEAD_DIM, IDX_HEADS * IDX_DIM, IDX_DIM, IDX_HEADS,
             B_WIDTH, B_WIDTH, B_WIDTH, B_HEADS,
             C_HEADS * C_DK, C_HEADS * C_DK, C_WIDTH, C_GATE_RANK, C_WIDTH)

LANES = 128
FRONT = (-N_META) % CHUNK
BLK = 128
T_AQ, T_AK, T_AV, T_IQ, T_IK = 0, 6, 7, 8, 16
T_BQ, T_BK, T_BV = 18, 24, 30
T_CQ, T_CK, T_CV, T_CG = 36, 38, 40, 44
N_CAT = 48 * LANES
S_IW, S_BF, S_CR = 0, 16, 22
R_GRP, R_EXP = 0, 32
INPROJ_TN = 1536
GLA_C = 32
DSA_G = 3
MOE_TM = 256
EXP_TOPK_CONST = 2
NEG = -1e30
LOG2E = 1.4426950408889634
INT_MIN = -2 ** 31
VMEM_LIMIT = 56 * 1024 * 1024


def _cparams(sem):
    return pltpu.CompilerParams(dimension_semantics=sem, vmem_limit_bytes=VMEM_LIMIT)


def _row_tile(n, cap):
    t = cap
    while n % t:
        t //= 2
    return t


def _log_sigmoid(x):
    return jnp.minimum(x, 0.0) - jnp.log(1.0 + jnp.exp(-jnp.abs(x)))


def _dot(a, b):
    return jnp.dot(a, b, preferred_element_type=f32)


def _dot_nt(a, b):
    return lax.dot_general(a, b, (((1,), (1,)), ((), ())), preferred_element_type=f32)


def _dot_tn(a, b):
    return lax.dot_general(a, b, (((0,), (0,)), ((), ())), preferred_element_type=f32)


def _split3(x):
    hi = x.astype(bf16)
    r1 = x - hi.astype(f32)
    mid = r1.astype(bf16)
    lo = (r1 - mid.astype(f32)).astype(bf16)
    return hi, mid, lo


def _rope128(x, cos, sin, lane):
    return x * cos + pltpu.roll(x, 64, 1) * jnp.where(lane < 64, -sin, sin)


def _rope64(x, cos, sin, lane):
    low = (lane & 63) < 32
    return (x * cos + pltpu.roll(x, 32, 1) * jnp.where(low, 0.0, sin)
            + pltpu.roll(x, 96, 1) * jnp.where(low, -sin, 0.0))


def _rms_gain(x, g):
    return x * lax.rsqrt(jnp.mean(x * x, axis=-1, keepdims=True) + EPS) * g


def _prenorm_kernel(l_ref, hmid_ref, ya_ref, yb_ref, g_ref, ws_ref, h_ref, xn_ref, zs_ref):
    h = hmid_ref[...] + ya_ref[...].astype(f32) + yb_ref[...].astype(f32)
    h_ref[...] = h
    xn = _rms_gain(h, g_ref[...]).astype(bf16)
    xn_ref[...] = xn
    zs_ref[...] = _dot(xn, ws_ref[...])


def _inproj_kernel(l_ref, xn_ref, w_ref, z_ref):
    z_ref[...] = _dot(xn_ref[...], w_ref[...]).astype(bf16)


def _inproj(l, hmid, y2, gain, wcat, ws):
    n, d = hmid.shape
    tm = _row_tile(n, 512)
    nt = n // tm
    row = lambda w: pl.BlockSpec((tm, w), lambda i, l: (i, 0))
    h, xn, zs = pl.pallas_call(
        _prenorm_kernel,
        grid_spec=pltpu.PrefetchScalarGridSpec(
            num_scalar_prefetch=1, grid=(nt,),
            in_specs=[row(d), row(d),
                      pl.BlockSpec((tm, d), lambda i, l: (i + nt, 0)),
                      pl.BlockSpec((None, 1, d), lambda i, l: (l[0], 0, 0)),
                      pl.BlockSpec((None, d, LANES), lambda i, l: (l[0], 0, 0))],
            out_specs=[row(d), row(d), row(LANES)]),
        out_shape=[jax.ShapeDtypeStruct((n, d), f32), jax.ShapeDtypeStruct((n, d), bf16),
                   jax.ShapeDtypeStruct((n, LANES), f32)],
        compiler_params=_cparams(("parallel",)),
        name="prenorm",
    )(l, hmid, y2, y2, gain, ws)
    tn = INPROJ_TN
    zcat = pl.pallas_call(
        _inproj_kernel,
        grid_spec=pltpu.PrefetchScalarGridSpec(
            num_scalar_prefetch=1, grid=(N_CAT // tn, nt),
            in_specs=[pl.BlockSpec((tm, d), lambda j, i, l: (i, 0)),
                      pl.BlockSpec((None, d, tn), lambda j, i, l: (l[0], 0, j))],
            out_specs=pl.BlockSpec((tm, tn), lambda j, i, l: (i, j))),
        out_shape=jax.ShapeDtypeStruct((n, N_CAT), bf16),
        compiler_params=_cparams(("parallel", "parallel")),
        name="inproj",
    )(l, xn, wcat)
    return h, zcat, zs


def _kprep_kernel(l_ref, ak_ref, ik_ref, bk_ref, zs_ref,
                  cos_ref, sin_ref, icos_ref, isin_ref, gak_ref, gbk_ref, fb_ref,
                  akr_ref, ikr_ref, bka_ref, carry_ref, *, t_real):
    k = pl.program_id(1)
    lane = lax.broadcasted_iota(i32, (BLK, LANES), 1)
    row = lax.broadcasted_iota(i32, (BLK, LANES), 0)
    pos = k * BLK + row
    valid = (pos >= FRONT) & (pos < FRONT + t_real)

    x = ak_ref[...].astype(f32)
    akr_ref[...] = _rope128(_rms_gain(x, gak_ref[...]), cos_ref[...], sin_ref[...], lane).astype(bf16)
    ikr_ref[...] = _rope64(ik_ref[...].astype(f32), icos_ref[...], isin_ref[...], lane).astype(bf16)

    @pl.when(k == 0)
    def _():
        carry_ref[...] = jnp.zeros_like(carry_ref)

    lf = jnp.where(valid, _log_sigmoid(zs_ref[...] + fb_ref[...]), 0.0)
    tri = (row >= lane).astype(bf16)
    hi, mid, lo = _split3(lf)
    fcum = _dot(tri, hi) + _dot(tri, mid) + _dot(tri, lo) + carry_ref[...]
    carry_ref[...] = fcum[BLK - 1:BLK, :]

    fs = jnp.where(valid, fcum * (-(HEAD_DIM ** 0.5)), NEG)
    for h in range(B_HEADS):
        sl = slice(h * LANES, (h + 1) * LANES)
        bka_ref[h, :, 0:LANES] = _rms_gain(bk_ref[:, sl].astype(f32), gbk_ref[...]).astype(bf16)
        p0, p1, p2 = _split3(fs[:, S_BF + h:S_BF + h + 1])
        aug = jnp.where(lane == 0, p0.astype(f32),
                        jnp.where(lane == 1, p1.astype(f32), jnp.where(lane == 2, p2.astype(f32), 0.0)))
        bka_ref[h, :, LANES:2 * LANES] = aug.astype(bf16)


def _kprep(l, zcat3, zs3, tabs, gak, gbk, fbias, t_real):
    bsz, tp, _ = zcat3.shape
    nkb = tp // BLK
    cos, sin, icos, isin = tabs
    tile = lambda c: pl.BlockSpec((None, BLK, LANES), lambda b, k, l, c=c: (b, k, c))
    wide = lambda c: pl.BlockSpec((None, BLK, B_WIDTH), lambda b, k, l, c=c: (b, k, c))
    tab = pl.BlockSpec((BLK, LANES), lambda b, k, l: (k, 0))
    gain = pl.BlockSpec((None, 1, LANES), lambda b, k, l: (l[0], 0, 0))
    return pl.pallas_call(
        functools.partial(_kprep_kernel, t_real=t_real),
        grid_spec=pltpu.PrefetchScalarGridSpec(
            num_scalar_prefetch=1, grid=(bsz, nkb),
            in_specs=[tile(T_AK), tile(T_IK),
                      wide(T_BK * LANES // B_WIDTH),
                      pl.BlockSpec((None, BLK, LANES), lambda b, k, l: (b, k, 0)),
                      tab, tab, tab, tab, gain, gain, gain],
            out_specs=[
                pl.BlockSpec((None, BLK, LANES), lambda b, k, l: (b, k, 0)),
                pl.BlockSpec((None, BLK, LANES), lambda b, k, l: (b, k, 0)),
                pl.BlockSpec((None, B_HEADS, BLK, 2 * LANES), lambda b, k, l: (b, 0, k, 0)),
            ],
            scratch_shapes=[pltpu.VMEM((1, LANES), f32)]),
        out_shape=[jax.ShapeDtypeStruct((bsz, tp, LANES), bf16),
                   jax.ShapeDtypeStruct((bsz, tp, LANES), bf16),
                   jax.ShapeDtypeStruct((bsz, B_HEADS, tp, 2 * LANES), bf16)],
        compiler_params=_cparams(("parallel", "arbitrary")),
        name="kprep",
    )(l, zcat3, zcat3, zcat3, zs3, cos, sin, icos, isin, gak, gbk, fbias)


def _dsa_kernel(l_ref, aq_ref, iq_ref, zs_ref, akr_ref, ikr_ref, av_ref,
                cos_ref, sin_ref, icos_ref, isin_ref, gq_ref, out_ref,
                key_ref, iqs_ref, q6_ref, iwt_ref, m_ref, s_ref, acc_ref, *, t_real, k_top):
    i = pl.program_id(1)
    nk = i + 1
    lane = lax.broadcasted_iota(i32, (BLK, LANES), 1)

    for h in range(A_HEADS):
        x = aq_ref[:, h * LANES:(h + 1) * LANES].astype(f32)
        xr = _rope128(_rms_gain(x, gq_ref[...]), cos_ref[...], sin_ref[...], lane)
        q6_ref[h * BLK:(h + 1) * BLK, :] = xr.astype(bf16)
    for t in range(IDX_HEADS // 2):
        x = iq_ref[:, t * LANES:(t + 1) * LANES].astype(f32)
        xr = _rope64(x, icos_ref[...], isin_ref[...], lane)
        iqs_ref[(2 * t) * BLK:(2 * t + 1) * BLK, :] = jnp.where(lane < IDX_DIM, xr, 0.0).astype(bf16)
        iqs_ref[(2 * t + 1) * BLK:(2 * t + 2) * BLK, :] = jnp.where(
            lane < IDX_DIM, pltpu.roll(xr, 64, 1), 0.0).astype(bf16)
    iwt_ref[...] = (zs_ref[...] * (IDX_HEADS ** -0.5 * IDX_DIM ** -0.5)).T

    gb = DSA_G * BLK
    ng = lax.div(nk, jnp.int32(DSA_G))
    nr = nk - ng * DSA_G

    def over_keys(fn, init):
        c = lax.fori_loop(0, ng, lambda g, c: fn(pl.multiple_of(g * gb, gb), gb, c), init)
        return lax.fori_loop(0, nr, lambda r, c: fn(pl.multiple_of((ng * DSA_G + r) * BLK, BLK), BLK, c), c)

    def score_rows(k0, nrows, carry):
        dt = _dot_nt(ikr_ref[pl.ds(k0, nrows), :], iqs_ref[...])
        s = jnp.zeros((nrows, LANES), f32)
        for h in range(IDX_HEADS):
            s = s + iwt_ref[h:h + 1, :] * jnp.maximum(dt[:, h * LANES:(h + 1) * LANES], 0.0)
        kpos = k0 + lax.broadcasted_iota(i32, (nrows, LANES), 0)
        qpos = i * BLK + lax.broadcasted_iota(i32, (nrows, LANES), 1)
        adm = ((kpos >> 6) <= (qpos >> 6)) & (kpos >= FRONT) & (kpos < FRONT + t_real)
        bits = lax.bitcast_convert_type(s, i32)
        key = bits ^ ((bits >> 31) & 0x7FFFFFFF)
        key_ref[pl.ds(k0, nrows), :] = jnp.where(adm, key, INT_MIN)
        return carry

    over_keys(score_rows, 0)

    def bit_body(t, thr_u):
        bit = jnp.left_shift(jnp.int32(1), 31 - t)
        cand_u = thr_u | bit
        cand_s = cand_u ^ INT_MIN

        def count_rows(k0, nrows, c):
            hit = (key_ref[pl.ds(k0, nrows), :] >= cand_s).astype(i32)
            for j in range(nrows // BLK):
                c = c + hit[j * BLK:(j + 1) * BLK, :]
            return c

        cnt = over_keys(count_rows, jnp.zeros((BLK, LANES), i32))
        tot = jnp.sum(cnt.astype(f32), axis=0, keepdims=True)
        return jnp.where(tot >= k_top, cand_u, thr_u)

    thr_u = lax.fori_loop(0, 32, bit_body, jnp.zeros((1, LANES), i32))
    thr_s = jnp.maximum(thr_u ^ INT_MIN, INT_MIN + 1)

    m_ref[...] = jnp.full(m_ref.shape, NEG, f32)
    s_ref[...] = jnp.zeros(s_ref.shape, f32)
    acc_ref[...] = jnp.zeros(acc_ref.shape, f32)
    c2 = HEAD_DIM ** -0.5 * LOG2E

    def attn_rows(k0, nrows, carry):
        st = _dot_nt(akr_ref[pl.ds(k0, nrows), :], q6_ref[...])
        bias = jnp.where(key_ref[pl.ds(k0, nrows), :] >= thr_s, 0.0, NEG)
        ps, alphas = [], []
        for h in range(A_HEADS):
            sl = slice(h * LANES, (h + 1) * LANES)
            sh = st[:, sl] + bias
            m_old = m_ref[:, sl]
            m_new = jnp.maximum(m_old, jnp.max(sh, axis=0, keepdims=True))
            alpha = jnp.exp2((m_old - m_new) * c2)
            p = jnp.exp2((sh - m_new) * c2)
            s_ref[:, sl] = s_ref[:, sl] * alpha + jnp.sum(p, axis=0, keepdims=True)
            m_ref[:, sl] = m_new
            alphas.append(alpha)
            ps.append(p.astype(bf16))
        pv = _dot_tn(av_ref[pl.ds(k0, nrows), :], jnp.concatenate(ps, axis=1))
        acc_ref[...] = acc_ref[...] * jnp.concatenate(alphas, axis=1) + pv
        return carry

    over_keys(attn_rows, 0)

    for h in range(A_HEADS):
        sl = slice(h * LANES, (h + 1) * LANES)
        o = acc_ref[:, sl] / jnp.maximum(s_ref[:, sl], 1e-30)
        out_ref[:, sl] = o.T.astype(bf16)


def _dsa(l, zcat3, zs3, akr, ikr, tabs, gaq, t_real, k_top):
    bsz, tp, _ = zcat3.shape
    nkb = tp // BLK
    cos, sin, icos, isin = tabs
    tab = pl.BlockSpec((BLK, LANES), lambda b, i, l: (i, 0))
    full = pl.BlockSpec((None, tp, LANES), lambda b, i, l: (b, 0, 0))
    return pl.pallas_call(
        functools.partial(_dsa_kernel, t_real=t_real, k_top=k_top),
        grid_spec=pltpu.PrefetchScalarGridSpec(
            num_scalar_prefetch=1, grid=(bsz, nkb),
            in_specs=[
                pl.BlockSpec((None, BLK, A_WIDTH), lambda b, i, l: (b, i, 0)),
                pl.BlockSpec((None, BLK, IDX_HEADS * IDX_DIM), lambda b, i, l: (b, i, T_IQ * LANES // (IDX_HEADS * IDX_DIM))),
                pl.BlockSpec((None, BLK, LANES), lambda b, i, l: (b, i, 0)),
                full, full,
                pl.BlockSpec((None, tp, LANES), lambda b, i, l: (b, 0, T_AV)),
                tab, tab, tab, tab,
                pl.BlockSpec((None, 1, LANES), lambda b, i, l: (l[0], 0, 0)),
            ],
            out_specs=pl.BlockSpec((None, BLK, A_WIDTH), lambda b, i, l: (b, i, 0)),
            scratch_shapes=[
                pltpu.VMEM((tp, LANES), i32),
                pltpu.VMEM((IDX_HEADS * BLK, LANES), bf16),
                pltpu.VMEM((A_HEADS * BLK, LANES), bf16),
                pltpu.VMEM((LANES, BLK), f32),
                pltpu.VMEM((1, A_HEADS * BLK), f32),
                pltpu.VMEM((1, A_HEADS * BLK), f32),
                pltpu.VMEM((HEAD_DIM, A_HEADS * BLK), f32),
            ]),
        out_shape=jax.ShapeDtypeStruct((bsz, tp, A_WIDTH), bf16),
        compiler_params=_cparams(("parallel", "arbitrary")),
        name="dsa",
    )(l, zcat3, zcat3, zs3, akr, ikr, zcat3, cos, sin, icos, isin, gaq)


def _fattn_kernel(l_ref, bq_ref, bka_ref, bv_ref, gq_ref, out_ref, qa_ref, m_ref, s_ref, acc_ref):
    i = pl.program_id(1)
    fb = bq_ref.shape[0]
    ones3 = jnp.where(lax.broadcasted_iota(i32, (fb, LANES), 1) < 3, 1.0, 0.0).astype(bf16)
    for h in range(B_HEADS):
        sl = slice(h * LANES, (h + 1) * LANES)
        qa_ref[h, :, 0:LANES] = _rms_gain(bq_ref[:, sl].astype(f32), gq_ref[...]).astype(bf16)
        qa_ref[h, :, LANES:2 * LANES] = ones3
    m_ref[...] = jnp.full(m_ref.shape, NEG, f32)
    s_ref[...] = jnp.zeros(s_ref.shape, f32)
    acc_ref[...] = jnp.zeros(acc_ref.shape, f32)
    c2 = HEAD_DIM ** -0.5 * LOG2E
    causal = lax.broadcasted_iota(i32, (fb, fb), 0) <= lax.broadcasted_iota(i32, (fb, fb), 1)

    def step(kb, diagonal):
        k0 = pl.multiple_of(kb * fb, fb)
        for h in range(B_HEADS):
            sl = slice(h * LANES, (h + 1) * LANES)
            st = _dot_nt(bka_ref[h, pl.ds(k0, fb), :], qa_ref[h])
            if diagonal:
                st = jnp.where(causal, st, NEG)
            m_old = m_ref[h]
            m_new = jnp.maximum(m_old, jnp.max(st, axis=0, keepdims=True))
            alpha = jnp.exp2((m_old - m_new) * c2)
            p = jnp.exp2((st - m_new) * c2)
            s_ref[h] = s_ref[h] * alpha + jnp.sum(p, axis=0, keepdims=True)
            m_ref[h] = m_new
            acc_ref[h] = acc_ref[h] * alpha + _dot_tn(bv_ref[pl.ds(k0, fb), sl], p.astype(bf16))

    def body(kb, carry):
        step(kb, False)
        return carry

    lax.fori_loop(0, i, body, 0)
    step(i, True)
    for h in range(B_HEADS):
        sl = slice(h * LANES, (h + 1) * LANES)
        out_ref[:, sl] = (acc_ref[h] / s_ref[h]).T.astype(bf16)


def _fattn(l, zcat3, bka, gbq):
    bsz, tp, _ = zcat3.shape
    fb = _seq_tile(tp, 384)
    once = pl.Buffered(1)
    return pl.pallas_call(
        _fattn_kernel,
        grid_spec=pltpu.PrefetchScalarGridSpec(
            num_scalar_prefetch=1, grid=(bsz, tp // fb),
            in_specs=[
                pl.BlockSpec((None, fb, B_WIDTH), lambda b, i, l: (b, i, T_BQ * LANES // B_WIDTH)),
                pl.BlockSpec((None, B_HEADS, tp, 2 * LANES), lambda b, i, l: (b, 0, 0, 0), pipeline_mode=once),
                pl.BlockSpec((None, tp, B_WIDTH), lambda b, i, l: (b, 0, T_BV * LANES // B_WIDTH),
                             pipeline_mode=once),
                pl.BlockSpec((None, 1, LANES), lambda b, i, l: (l[0], 0, 0)),
            ],
            out_specs=pl.BlockSpec((None, fb, B_WIDTH), lambda b, i, l: (b, i, 0)),
            scratch_shapes=[
                pltpu.VMEM((B_HEADS, fb, 2 * LANES), bf16),
                pltpu.VMEM((B_HEADS, 1, fb), f32),
                pltpu.VMEM((B_HEADS, 1, fb), f32),
                pltpu.VMEM((B_HEADS, HEAD_DIM, fb), f32),
            ]),
        out_shape=jax.ShapeDtypeStruct((bsz, tp, B_WIDTH), bf16),
        compiler_params=_cparams(("parallel", "arbitrary")),
        name="fattn",
    )(l, zcat3, bka, zcat3, gbq)


def _gla_kernel(l_ref, cq_ref, ck_ref, cv_ref, cg_ref, zs_ref, w2_ref, gb_ref, og_ref, e_ref,
                out_ref, st_ref, oi_ref, *, t_real):
    c = pl.program_id(0)
    nb, cc, dk = cq_ref.shape

    @pl.when(c == 0)
    def _():
        st_ref[...] = jnp.zeros_like(st_ref)

    rowc = lax.broadcasted_iota(i32, (cc, 1), 0)
    pos = c * cc + rowc
    valid = (pos >= FRONT) & (pos < FRONT + t_real)
    tri = (lax.broadcasted_iota(i32, (cc, cc), 0) >= lax.broadcasted_iota(i32, (cc, cc), 1)).astype(bf16)
    lane_k = lax.broadcasted_iota(i32, (1, dk), 1)
    head_masks = [(lane_k >= h * C_DK) & (lane_k < (h + 1) * C_DK) for h in range(C_HEADS)]

    for b in range(nb):
        x = _dot(zs_ref[b].astype(bf16), w2_ref[...]) + gb_ref[...]
        la = jnp.where(valid, _log_sigmoid(x) * (1.0 / C_TAU), 0.0)
        hi, mid, lo = _split3(la)
        bc = _dot(tri, hi) + _dot(tri, mid) + _dot(tri, lo)
        q = cq_ref[b].astype(f32) * (C_DK ** -0.5)
        k = ck_ref[b].astype(f32)
        v = cv_ref[b]
        vf = v.astype(f32)
        blast = bc[cc - 1:cc, :]
        qe = q * jnp.exp(bc)
        ke = k * jnp.exp(blast - bc)
        st = st_ref[b]
        stb = st.astype(bf16)

        o_inter = jnp.concatenate(
            [_dot_nt(jnp.where(head_masks[h], qe, 0.0).astype(bf16), stb) for h in range(C_HEADS)], axis=1)

        rows = []
        for r in range(cc):
            dec = jnp.exp(jnp.minimum(bc[r:r + 1, :] - bc, 0.0))
            rows.append(jnp.where(rowc <= r, q[r:r + 1, :] * k * dec, 0.0).astype(bf16))
        rr = _dot(jnp.concatenate(rows, axis=0), e_ref[...])
        for r in range(cc):
            oi_ref[r:r + 1, :] = jnp.sum(rr[r * cc:(r + 1) * cc, :] * vf, axis=0, keepdims=True)
        o = o_inter + oi_ref[...]

        new_st = st * jnp.exp(blast)
        for h in range(C_HEADS):
            km = jnp.where(head_masks[h], ke, 0.0).astype(bf16)
            new_st = new_st + _dot_tn(v[:, h * C_DV:(h + 1) * C_DV], km)
        st_ref[b] = new_st

        g = cg_ref[b].astype(f32)
        gs = g * (1.0 / (1.0 + jnp.exp(-g)))
        for h in range(C_HEADS):
            sl = slice(h * C_DV, (h + 1) * C_DV)
            out_ref[b, :, sl] = (_rms_gain(o[:, sl], og_ref[...]) * gs[:, sl]).astype(bf16)


def _gla(l, zcat3, zs3, w2p, gb, og, emat, t_real):
    bsz, tp, _ = zcat3.shape
    cc = GLA_C
    dk = C_HEADS * C_DK
    return pl.pallas_call(
        functools.partial(_gla_kernel, t_real=t_real),
        grid_spec=pltpu.PrefetchScalarGridSpec(
            num_scalar_prefetch=1, grid=(tp // cc,),
            in_specs=[
                pl.BlockSpec((bsz, cc, dk), lambda c, l: (0, c, T_CQ * LANES // dk)),
                pl.BlockSpec((bsz, cc, dk), lambda c, l: (0, c, T_CK * LANES // dk)),
                pl.BlockSpec((bsz, cc, C_WIDTH), lambda c, l: (0, c, T_CV * LANES // C_WIDTH)),
                pl.BlockSpec((bsz, cc, C_WIDTH), lambda c, l: (0, c, T_CG * LANES // C_WIDTH)),
                pl.BlockSpec((bsz, cc, LANES), lambda c, l: (0, c, 0)),
                pl.BlockSpec((None, LANES, dk), lambda c, l: (l[0], 0, 0)),
                pl.BlockSpec((None, 1, dk), lambda c, l: (l[0], 0, 0)),
                pl.BlockSpec((None, 1, C_DV), lambda c, l: (l[0], 0, 0)),
                pl.BlockSpec((dk, C_WIDTH), lambda c, l: (0, 0)),
            ],
            out_specs=pl.BlockSpec((bsz, cc, C_WIDTH), lambda c, l: (0, c, 0)),
            scratch_shapes=[pltpu.VMEM((bsz, C_DV, dk), f32), pltpu.VMEM((cc, C_WIDTH), f32)]),
        out_shape=jax.ShapeDtypeStruct((bsz, tp, C_WIDTH), bf16),
        compiler_params=_cparams(("arbitrary",)),
        name="gla",
    )(l, zcat3, zcat3, zcat3, zcat3, zs3, w2p, gb, og, emat)


def _outproj_kernel(l_ref, oa_ref, ob_ref, oc_ref, h_ref, wo_ref, g_ref, wr_ref, br_ref,
                    hmid_ref, u_ref, eid_ref, ew_ref, *, t_real):
    tm = h_ref.shape[0]
    mix = (_dot(oa_ref[...], wo_ref[0:A_WIDTH, :])
           + _dot(ob_ref[...], wo_ref[A_WIDTH:A_WIDTH + B_WIDTH, :])
           + _dot(oc_ref[...], wo_ref[A_WIDTH + B_WIDTH:, :]))
    pos = pl.program_id(1) * tm + lax.broadcasted_iota(i32, (tm, 1), 0)
    valid = (pos >= FRONT) & (pos < FRONT + t_real)
    hm = h_ref[...] + jnp.where(valid, mix, 0.0)
    hmid_ref[...] = hm
    u = _rms_gain(hm, g_ref[...]).astype(bf16)
    u_ref[...] = u

    logits = _dot(u, wr_ref[...]) + br_ref[...]
    lane = lax.broadcasted_iota(i32, (tm, LANES), 1)
    lanef = lane.astype(f32)
    big = float(4 * LANES)
    first = lambda hit: jnp.min(jnp.where(hit, lanef, big), axis=-1, keepdims=True).astype(i32)
    gl = jnp.where(lane < R_GRP + N_GROUPS, logits, -jnp.inf)
    gmax = jnp.max(gl, axis=-1, keepdims=True)
    g_p = 1.0 / jnp.sum(jnp.exp(gl - gmax), axis=-1, keepdims=True)
    g_i = first(gl == gmax)
    e_lane = lane - R_EXP
    emask = (e_lane >= 0) & (e_lane < N_EXPERTS) & ((e_lane >> 3) == g_i)
    el = jnp.where(emask, logits, -jnp.inf)
    m1 = jnp.max(el, axis=-1, keepdims=True)
    i1 = first(el == m1)
    el2 = jnp.where(lane == i1, -jnp.inf, el)
    m2 = jnp.max(el2, axis=-1, keepdims=True)
    i2 = first(el2 == m2)
    r = jnp.exp(m2 - m1)
    w1 = g_p / (1.0 + r)
    w2 = g_p * r / (1.0 + r)
    eid_ref[...] = jnp.where(lane == 0, i1 - R_EXP, jnp.where(lane == 1, i2 - R_EXP, 0))
    ew_ref[...] = jnp.where(lane == 0, w1, jnp.where(lane == 1, w2, 0.0))


def _seq_tile(tp, cap):
    return max(t for t in range(BLK, cap + 1, BLK) if tp % t == 0)


def _outproj(l, oa, ob, oc, h, wo, gain, wr, br, t_real, tp):
    n, d = h.shape
    tm = _seq_tile(tp, 384)
    nj = tp // tm
    row = lambda w: pl.BlockSpec((tm, w), lambda b, j, l: (b * nj + j, 0))
    return pl.pallas_call(
        functools.partial(_outproj_kernel, t_real=t_real),
        grid_spec=pltpu.PrefetchScalarGridSpec(
            num_scalar_prefetch=1, grid=(n // tp, nj),
            in_specs=[row(A_WIDTH), row(B_WIDTH), row(C_WIDTH), row(d),
                      pl.BlockSpec((None, d, d), lambda b, j, l: (l[0], 0, 0)),
                      pl.BlockSpec((None, 1, d), lambda b, j, l: (l[0], 0, 0)),
                      pl.BlockSpec((None, d, LANES), lambda b, j, l: (l[0], 0, 0)),
                      pl.BlockSpec((None, 1, LANES), lambda b, j, l: (l[0], 0, 0))],
            out_specs=[row(d), row(d), row(LANES), row(LANES)]),
        out_shape=[jax.ShapeDtypeStruct((n, d), f32), jax.ShapeDtypeStruct((n, d), bf16),
                   jax.ShapeDtypeStruct((n, LANES), i32), jax.ShapeDtypeStruct((n, LANES), f32)],
        compiler_params=_cparams(("parallel", "parallel")),
        name="outproj",
    )(l, oa, ob, oc, h, wo, gain, wr, br)


def _moe_kernel(l_ref, te_ref, na_ref, xs_ref, w1_ref, w3_ref, w2_ref, rw_ref, ys_ref,
                w1b_ref, w3b_ref, w2b_ref):
    i = pl.program_id(0)

    @pl.when((i == 0) | (te_ref[i] != te_ref[jnp.maximum(i - 1, 0)]))
    def _():
        w1b_ref[...] = w1_ref[...].astype(bf16)
        w3b_ref[...] = w3_ref[...].astype(bf16)
        w2b_ref[...] = w2_ref[...].astype(bf16)

    @pl.when(i < na_ref[0])
    def _():
        x = xs_ref[...]
        h1 = _dot(x, w1b_ref[...])
        h3 = _dot(x, w3b_ref[...])
        hid = (h1 * (1.0 / (1.0 + jnp.exp(-h1))) * h3).astype(bf16)
        ys_ref[...] = (_dot(hid, w2b_ref[...]) * rw_ref[...]).astype(bf16)

    @pl.when(i >= na_ref[0])
    def _():
        ys_ref[...] = jnp.zeros_like(ys_ref)


def _moe(l, te, na, xs, w1, w3, w2, rw):
    p, d = xs.shape
    tm = MOE_TM
    return pl.pallas_call(
        _moe_kernel,
        grid_spec=pltpu.PrefetchScalarGridSpec(
            num_scalar_prefetch=3, grid=(p // tm,),
            in_specs=[
                pl.BlockSpec((tm, d), lambda i, l, te, na: (i, 0)),
                pl.BlockSpec((None, None, d, D_EXPERT), lambda i, l, te, na: (l[0], te[i], 0, 0)),
                pl.BlockSpec((None, None, d, D_EXPERT), lambda i, l, te, na: (l[0], te[i], 0, 0)),
                pl.BlockSpec((None, None, D_EXPERT, d), lambda i, l, te, na: (l[0], te[i], 0, 0)),
                pl.BlockSpec((tm, 1), lambda i, l, te, na: (i, 0)),
            ],
            out_specs=pl.BlockSpec((tm, d), lambda i, l, te, na: (i, 0)),
            scratch_shapes=[pltpu.VMEM((d, D_EXPERT), bf16), pltpu.VMEM((d, D_EXPERT), bf16),
                            pltpu.VMEM((D_EXPERT, d), bf16)]),
        out_shape=jax.ShapeDtypeStruct((p, d), bf16),
        compiler_params=_cparams(("arbitrary",)),
        name="moe",
    )(l, te, na, xs, w1, w3, w2, rw)


def _route(eid, ew, n_tiles):
    tm = MOE_TM
    p = n_tiles * tm
    n = eid.shape[0]
    e_flat = eid[:, :EXP_TOPK_CONST].T.reshape(-1)
    w_flat = ew[:, :EXP_TOPK_CONST].T.reshape(-1)
    na_all = e_flat.shape[0]
    hp = lax.Precision.HIGHEST
    ex = jnp.arange(N_EXPERTS, dtype=i32)[:, None]
    onehot = (ex == e_flat[None, :]).astype(f32)
    counts = jnp.sum(onehot, axis=1).astype(i32)
    pc = ((counts + tm - 1) // tm) * tm
    pend = jnp.cumsum(pc)
    po = pend - pc
    co = jnp.cumsum(counts) - counts
    order = jnp.argsort(e_flat, stable=True).astype(i32)
    r = jnp.arange(p, dtype=i32)
    step = (r[None, :] >= pend[:, None]).astype(f32)
    dlt = lambda v: jnp.concatenate([v[1:] - v[:-1], jnp.zeros((1,), v.dtype)]).astype(f32)
    tabs = jnp.stack([jnp.ones((N_EXPERTS,), f32), dlt(po), dlt(counts), dlt(co)])
    picked = jnp.dot(tabs, step, precision=hp).astype(i32)
    e_r = jnp.minimum(picked[0], N_EXPERTS - 1)
    local = r - (po[0] + picked[1])
    valid_r = (local < counts[0] + picked[2]) & (r < pend[-1])
    a_r = order[jnp.clip(co[0] + picked[3] + local, 0, na_all - 1)]
    tok_r = jnp.where(valid_r, jnp.where(a_r >= n, a_r - n, a_r), r % n)
    w_r = jnp.where(valid_r, w_flat[a_r], 0.0)
    n_act = (pend[-1] // tm).astype(i32)
    tile_e = e_r[::tm]
    te = jnp.where(jnp.arange(n_tiles, dtype=i32) < n_act, tile_e, tile_e[jnp.maximum(n_act - 1, 0)])
    inv = jnp.argsort(order).astype(i32)
    pos_a = jnp.dot((po - co).astype(f32)[None, :], onehot, precision=hp)[0].astype(i32) + inv
    return tok_r, w_r[:, None], te, n_act.reshape(1), pos_a


def _final_kernel(hmid_ref, ya_ref, yb_ref, out_ref):
    out_ref[...] = hmid_ref[...] + ya_ref[...].astype(f32) + yb_ref[...].astype(f32)


def _final(hmid, y2):
    n, d = hmid.shape
    tm = _row_tile(n, 512)
    nt = n // tm
    return pl.pallas_call(
        _final_kernel, grid=(nt,),
        in_specs=[pl.BlockSpec((tm, d), lambda i: (i, 0)), pl.BlockSpec((tm, d), lambda i: (i, 0)),
                  pl.BlockSpec((tm, d), lambda i: (i + nt, 0))],
        out_specs=pl.BlockSpec((tm, d), lambda i: (i, 0)),
        out_shape=jax.ShapeDtypeStruct((n, d), f32),
        compiler_params=_cparams(("parallel",)),
        name="final",
    )(hmid, y2, y2)


def _rope_tables(tp):
    pos = (jnp.arange(tp, dtype=f32) - FRONT)[:, None]

    def tab(dim, reps):
        inv = 1.0 / (ROPE_THETA ** (jnp.arange(0, dim, 2, dtype=f32) / dim))
        ang = pos * inv[None, :]
        return jnp.tile(jnp.cos(ang), (1, reps)), jnp.tile(jnp.sin(ang), (1, reps))

    cos, sin = tab(HEAD_DIM, 2)
    icos, isin = tab(IDX_DIM, 4)
    return cos, sin, icos, isin


def _split_w_in(w_in):
    parts, off = [], 0
    for s in IN_SPLITS:
        parts.append(w_in[..., off:off + s])
        off += s
    return parts


def kernel(x, meta, norm_mix_g, norm_ffn_g, w_in, a_qn_g, a_kn_g, b_qn_g, b_kn_g, b_f_bias,
           c_gate_w2, c_gate_b, c_on_g, w_out, r_group_w, r_group_b, r_exp_w, r_exp_b,
           e_w1, e_w3, e_w2):
    bsz, n_seq, d = x.shape
    depth = w_in.shape[0]
    t_real = n_seq + N_META
    k_top = min(TOPK_MAX, n_seq // 4)
    tp = -(-(FRONT + t_real) // BLK) * BLK
    n = bsz * tp

    (waq, wak, wav, wiq, wik, wiw, wbq, wbk, wbv, wbf, wcq, wck, wcv, wcr, wcg) = _split_w_in(w_in)
    zc = lambda w: jnp.zeros((depth, d, w), w_in.dtype)
    wcat = jnp.concatenate([waq, wak, wav, wiq, wik, zc(LANES - IDX_DIM), zc(LANES),
                            wbq, wbk, wbv, wcq, wck, wcv, wcg], axis=-1).astype(bf16)
    ws = jnp.concatenate([wiw, wbf, wcr, zc(LANES - S_CR - C_GATE_RANK)], axis=-1).astype(bf16)
    wo = w_out.astype(bf16)
    wr = jnp.concatenate([r_group_w, zc(R_EXP - N_GROUPS), r_exp_w, zc(LANES - R_EXP - N_EXPERTS)],
                         axis=-1).astype(bf16)
    zl = lambda w: jnp.zeros((depth, w), f32)
    br = jnp.concatenate([r_group_b, zl(R_EXP - N_GROUPS), r_exp_b, zl(LANES - R_EXP - N_EXPERTS)],
                         axis=-1)[:, None, :]
    fbias = jnp.concatenate([zl(S_BF), b_f_bias, zl(LANES - S_BF - B_HEADS)], axis=-1)[:, None, :]
    dk = C_HEADS * C_DK
    w2p = jnp.concatenate([jnp.zeros((depth, S_CR, dk), f32), c_gate_w2,
                           jnp.zeros((depth, LANES - S_CR - C_GATE_RANK, dk), f32)], axis=1).astype(bf16)
    gb = c_gate_b[:, None, :]
    og = c_on_g[:, None, :]
    g_mix = norm_mix_g[:, None, :]
    g_ffn = norm_ffn_g[:, None, :]
    gaq, gak, gbq, gbk = (g[:, None, :] for g in (a_qn_g, a_kn_g, b_qn_g, b_kn_g))
    emat = (jnp.arange(dk, dtype=i32)[:, None] // C_DK
            == jnp.arange(C_WIDTH, dtype=i32)[None, :] // C_DV).astype(bf16)
    tabs = _rope_tables(tp)

    h0 = jnp.concatenate([
        jnp.zeros((bsz, FRONT, d), f32),
        jnp.broadcast_to(meta[None].astype(f32), (bsz, N_META, d)),
        x.astype(f32),
        jnp.zeros((bsz, tp - FRONT - t_real, d), f32)], axis=1).reshape(n, d)
    n_tiles = -(-(EXP_TOPK_CONST * n + N_EXPERTS * (MOE_TM - 1)) // MOE_TM)

    def layer(li, carry):
        hmid, y2 = carry
        l = jnp.reshape(jnp.asarray(li, i32), (1,))
        h, zcat, zs = _inproj(l, hmid, y2, g_mix, wcat, ws)
        zcat3 = zcat.reshape(bsz, tp, N_CAT)
        zs3 = zs.reshape(bsz, tp, LANES)
        akr, ikr, bka = _kprep(l, zcat3, zs3, tabs, gak, gbk, fbias, t_real)
        oa = _dsa(l, zcat3, zs3, akr, ikr, tabs, gaq, t_real, k_top)
        ob = _fattn(l, zcat3, bka, gbq)
        oc = _gla(l, zcat3, zs3, w2p, gb, og, emat, t_real)
        hmid2, u, eid, ew = _outproj(l, oa.reshape(n, A_WIDTH), ob.reshape(n, B_WIDTH),
                                     oc.reshape(n, C_WIDTH), h, wo, g_ffn, wr, br, t_real, tp)
        tok_r, w_r, te, n_act, pos_a = _route(eid, ew, n_tiles)
        ys = _moe(l, te, n_act, jnp.take(u, tok_r, axis=0, mode="clip"), e_w1, e_w3, e_w2, w_r)
        return hmid2, jnp.take(ys, pos_a, axis=0, mode="clip")

    y0 = lax.optimization_barrier(jnp.zeros((EXP_TOPK_CONST * n, d), bf16))
    hmid, y2 = lax.fori_loop(0, depth, layer, (h0, y0))
    out = _final(hmid, y2).reshape(bsz, tp, d)
    return out[:, FRONT + N_META:FRONT + t_real].astype(x.dtype)
```

```python
import functools

import jax
import jax.numpy as jnp
from jax import lax
from jax.experimental import pallas as pl
from jax.experimental.pallas import tpu as pltpu

f32 = jnp.float32
bf16 = jnp.bfloat16
i32 = jnp.int32

D_MODEL = 2048
CHUNK = 64
N_META = 16
ROPE_THETA = 10000.0
EPS = 1e-6
HEAD_DIM = 128
A_HEADS = 6
IDX_HEADS = 16
IDX_DIM = 64
TOPK_MAX = 256
B_HEADS = 6
C_HEADS = 4
C_DK = 64
C_DV = 128
C_GATE_RANK = 16
C_TAU = 16.0
N_GROUPS = 4
EXP_PER_GROUP = 8
N_EXPERTS = N_GROUPS * EXP_PER_GROUP
D_EXPERT = 512
A_WIDTH = A_HEADS * HEAD_DIM
B_WIDTH = B_HEADS * HEAD_DIM
C_WIDTH = C_HEADS * C_DV
IN_SPLITS = (A_WIDTH, HEAD_DIM, HEAD_DIM, IDX_HEADS * IDX_DIM, IDX_DIM, IDX_HEADS,
             B_WIDTH, B_WIDTH, B_WIDTH, B_HEADS,
             C_HEADS * C_DK, C_HEADS * C_DK, C_WIDTH, C_GATE_RANK, C_WIDTH)

LANES = 128
FRONT = (-N_META) % CHUNK
BLK = 128
T_AQ, T_AK, T_AV, T_IQ, T_IK = 0, 6, 7, 8, 16
T_BQ, T_BK, T_BV = 18, 24, 30
T_CQ, T_CK, T_CV, T_CG = 36, 38, 40, 44
N_CAT = 48 * LANES
S_IW, S_BF, S_CR = 0, 16, 22
R_GRP, R_EXP = 0, 32
INPROJ_TN = 1536
GLA_C = 32
DSA_G = 3
MOE_TM = 256
EXP_TOPK_CONST = 2
NEG = -1e30
LOG2E = 1.4426950408889634
INT_MIN = -2 ** 31
VMEM_LIMIT = 56 * 1024 * 1024


def _cparams(sem):
    return pltpu.CompilerParams(dimension_semantics=sem, vmem_limit_bytes=VMEM_LIMIT)


def _row_tile(n, cap):
    t = cap
    while n % t:
        t //= 2
    return t


def _log_sigmoid(x):
    return jnp.minimum(x, 0.0) - jnp.log(1.0 + jnp.exp(-jnp.abs(x)))


def _dot(a, b):
    return jnp.dot(a, b, preferred_element_type=f32)


def _dot_nt(a, b):
    return lax.dot_general(a, b, (((1,), (1,)), ((), ())), preferred_element_type=f32)


def _dot_tn(a, b):
    return lax.dot_general(a, b, (((0,), (0,)), ((), ())), preferred_element_type=f32)


def _split3(x):
    hi = x.astype(bf16)
    r1 = x - hi.astype(f32)
    mid = r1.astype(bf16)
    lo = (r1 - mid.astype(f32)).astype(bf16)
    return hi, mid, lo


def _rope128(x, cos, sin, lane):
    return x * cos + pltpu.roll(x, 64, 1) * jnp.where(lane < 64, -sin, sin)


def _rope64(x, cos, sin, lane):
    low = (lane & 63) < 32
    return (x * cos + pltpu.roll(x, 32, 1) * jnp.where(low, 0.0, sin)
            + pltpu.roll(x, 96, 1) * jnp.where(low, -sin, 0.0))


def _rms_gain(x, g):
    return x * lax.rsqrt(jnp.mean(x * x, axis=-1, keepdims=True) + EPS) * g


def _prenorm_kernel(l_ref, hmid_ref, ya_ref, yb_ref, g_ref, ws_ref, h_ref, xn_ref, zs_ref):
    h = hmid_ref[...] + ya_ref[...].astype(f32) + yb_ref[...].astype(f32)
    h_ref[...] = h
    xn = _rms_gain(h, g_ref[...]).astype(bf16)
    xn_ref[...] = xn
    zs_ref[...] = _dot(xn, ws_ref[...])


def _inproj_kernel(l_ref, xn_ref, w_ref, z_ref):
    z_ref[...] = _dot(xn_ref[...], w_ref[...]).astype(bf16)


def _inproj(l, hmid, y2, gain, wcat, ws):
    n, d = hmid.shape
    tm = _row_tile(n, 512)
    nt = n // tm
    row = lambda w: pl.BlockSpec((tm, w), lambda i, l: (i, 0))
    h, xn, zs = pl.pallas_call(
        _prenorm_kernel,
        grid_spec=pltpu.PrefetchScalarGridSpec(
            num_scalar_prefetch=1, grid=(nt,),
            in_specs=[row(d), row(d),
                      pl.BlockSpec((tm, d), lambda i, l: (i + nt, 0)),
                      pl.BlockSpec((None, 1, d), lambda i, l: (l[0], 0, 0)),
                      pl.BlockSpec((None, d, LANES), lambda i, l: (l[0], 0, 0))],
            out_specs=[row(d), row(d), row(LANES)]),
        out_shape=[jax.ShapeDtypeStruct((n, d), f32), jax.ShapeDtypeStruct((n, d), bf16),
                   jax.ShapeDtypeStruct((n, LANES), f32)],
        compiler_params=_cparams(("parallel",)),
        name="prenorm",
    )(l, hmid, y2, y2, gain, ws)
    tn = INPROJ_TN
    zcat = pl.pallas_call(
        _inproj_kernel,
        grid_spec=pltpu.PrefetchScalarGridSpec(
            num_scalar_prefetch=1, grid=(N_CAT // tn, nt),
            in_specs=[pl.BlockSpec((tm, d), lambda j, i, l: (i, 0)),
                      pl.BlockSpec((None, d, tn), lambda j, i, l: (l[0], 0, j))],
            out_specs=pl.BlockSpec((tm, tn), lambda j, i, l: (i, j))),
        out_shape=jax.ShapeDtypeStruct((n, N_CAT), bf16),
        compiler_params=_cparams(("parallel", "parallel")),
        name="inproj",
    )(l, xn, wcat)
    return h, zcat, zs


def _kprep_kernel(l_ref, ak_ref, ik_ref, bk_ref, zs_ref,
                  cos_ref, sin_ref, icos_ref, isin_ref, gak_ref, gbk_ref, fb_ref,
                  akr_ref, ikr_ref, bka_ref, carry_ref, *, t_real):
    k = pl.program_id(1)
    lane = lax.broadcasted_iota(i32, (BLK, LANES), 1)
    row = lax.broadcasted_iota(i32, (BLK, LANES), 0)
    pos = k * BLK + row
    valid = (pos >= FRONT) & (pos < FRONT + t_real)

    x = ak_ref[...].astype(f32)
    akr_ref[...] = _rope128(_rms_gain(x, gak_ref[...]), cos_ref[...], sin_ref[...], lane).astype(bf16)
    ikr_ref[...] = _rope64(ik_ref[...].astype(f32), icos_ref[...], isin_ref[...], lane).astype(bf16)

    @pl.when(k == 0)
    def _():
        carry_ref[...] = jnp.zeros_like(carry_ref)

    lf = jnp.where(valid, _log_sigmoid(zs_ref[...] + fb_ref[...]), 0.0)
    tri = (row >= lane).astype(bf16)
    hi, mid, lo = _split3(lf)
    fcum = _dot(tri, hi) + _dot(tri, mid) + _dot(tri, lo) + carry_ref[...]
    carry_ref[...] = fcum[BLK - 1:BLK, :]

    fs = jnp.where(valid, fcum * (-(HEAD_DIM ** 0.5)), NEG)
    for h in range(B_HEADS):
        sl = slice(h * LANES, (h + 1) * LANES)
        bka_ref[h, :, 0:LANES] = _rms_gain(bk_ref[:, sl].astype(f32), gbk_ref[...]).astype(bf16)
        p0, p1, p2 = _split3(fs[:, S_BF + h:S_BF + h + 1])
        aug = jnp.where(lane == 0, p0.astype(f32),
                        jnp.where(lane == 1, p1.astype(f32), jnp.where(lane == 2, p2.astype(f32), 0.0)))
        bka_ref[h, :, LANES:2 * LANES] = aug.astype(bf16)


def _kprep(l, zcat3, zs3, tabs, gak, gbk, fbias, t_real):
    bsz, tp, _ = zcat3.shape
    nkb = tp // BLK
    cos, sin, icos, isin = tabs
    tile = lambda c: pl.BlockSpec((None, BLK, LANES), lambda b, k, l, c=c: (b, k, c))
    wide = lambda c: pl.BlockSpec((None, BLK, B_WIDTH), lambda b, k, l, c=c: (b, k, c))
    tab = pl.BlockSpec((BLK, LANES), lambda b, k, l: (k, 0))
    gain = pl.BlockSpec((None, 1, LANES), lambda b, k, l: (l[0], 0, 0))
    return pl.pallas_call(
        functools.partial(_kprep_kernel, t_real=t_real),
        grid_spec=pltpu.PrefetchScalarGridSpec(
            num_scalar_prefetch=1, grid=(bsz, nkb),
            in_specs=[tile(T_AK), tile(T_IK),
                      wide(T_BK * LANES // B_WIDTH),
                      pl.BlockSpec((None, BLK, LANES), lambda b, k, l: (b, k, 0)),
                      tab, tab, tab, tab, gain, gain, gain],
            out_specs=[
                pl.BlockSpec((None, BLK, LANES), lambda b, k, l: (b, k, 0)),
                pl.BlockSpec((None, BLK, LANES), lambda b, k, l: (b, k, 0)),
                pl.BlockSpec((None, B_HEADS, BLK, 2 * LANES), lambda b, k, l: (b, 0, k, 0)),
            ],
            scratch_shapes=[pltpu.VMEM((1, LANES), f32)]),
        out_shape=[jax.ShapeDtypeStruct((bsz, tp, LANES), bf16),
                   jax.ShapeDtypeStruct((bsz, tp, LANES), bf16),
                   jax.ShapeDtypeStruct((bsz, B_HEADS, tp, 2 * LANES), bf16)],
        compiler_params=_cparams(("parallel", "arbitrary")),
        name="kprep",
    )(l, zcat3, zcat3, zcat3, zs3, cos, sin, icos, isin, gak, gbk, fbias)


def _dsa_kernel(l_ref, aq_ref, iq_ref, zs_ref, akr_ref, ikr_ref, av_ref,
                cos_ref, sin_ref, icos_ref, isin_ref, gq_ref, out_ref,
                key_ref, iqs_ref, q6_ref, iwt_ref, m_ref, s_ref, acc_ref, stq_ref, *, t_real, k_top):
    i = pl.program_id(1)
    nk = i + 1
    lane = lax.broadcasted_iota(i32, (BLK, LANES), 1)

    for h in range(A_HEADS):
        x = aq_ref[:, h * LANES:(h + 1) * LANES].astype(f32)
        xr = _rope128(_rms_gain(x, gq_ref[...]), cos_ref[...], sin_ref[...], lane)
        q6_ref[h * BLK:(h + 1) * BLK, :] = xr.astype(bf16)
    for t in range(IDX_HEADS // 2):
        x = iq_ref[:, t * LANES:(t + 1) * LANES].astype(f32)
        xr = _rope64(x, icos_ref[...], isin_ref[...], lane)
        iqs_ref[(2 * t) * BLK:(2 * t + 1) * BLK, :] = jnp.where(lane < IDX_DIM, xr, 0.0).astype(bf16)
        iqs_ref[(2 * t + 1) * BLK:(2 * t + 2) * BLK, :] = jnp.where(
            lane < IDX_DIM, pltpu.roll(xr, 64, 1), 0.0).astype(bf16)
    iwt_ref[...] = (zs_ref[...] * (IDX_HEADS ** -0.5 * IDX_DIM ** -0.5)).T

    gb = DSA_G * BLK
    ng = lax.div(nk, jnp.int32(DSA_G))
    nr = nk - ng * DSA_G

    def over_keys(fn, init):
        c = lax.fori_loop(0, ng, lambda g, c: fn(pl.multiple_of(g * gb, gb), gb, c), init)
        return lax.fori_loop(0, nr, lambda r, c: fn(pl.multiple_of((ng * DSA_G + r) * BLK, BLK), BLK, c), c)

    def score_rows(k0, nrows, carry):
        dt = _dot_nt(ikr_ref[pl.ds(k0, nrows), :], iqs_ref[...])
        s = jnp.zeros((nrows, LANES), f32)
        for h in range(IDX_HEADS):
            s = s + iwt_ref[h:h + 1, :] * jnp.maximum(dt[:, h * LANES:(h + 1) * LANES], 0.0)
        kpos = k0 + lax.broadcasted_iota(i32, (nrows, LANES), 0)
        qpos = i * BLK + lax.broadcasted_iota(i32, (nrows, LANES), 1)
        adm = ((kpos >> 6) <= (qpos >> 6)) & (kpos >= FRONT) & (kpos < FRONT + t_real)
        bits = lax.bitcast_convert_type(s, i32)
        key = bits ^ ((bits >> 31) & 0x7FFFFFFF)
        key_ref[pl.ds(k0, nrows), :] = jnp.where(adm, key, INT_MIN)
        return carry

    over_keys(score_rows, 0)

    def bit_body(t, thr_u):
        bit = jnp.left_shift(jnp.int32(1), 31 - t)
        cand_u = thr_u | bit
        cand_s = cand_u ^ INT_MIN

        def count_rows(k0, nrows, c):
            hit = (key_ref[pl.ds(k0, nrows), :] >= cand_s).astype(i32)
            for j in range(nrows // BLK):
                c = c + hit[j * BLK:(j + 1) * BLK, :]
            return c

        cnt = over_keys(count_rows, jnp.zeros((BLK, LANES), i32))
        tot = jnp.sum(cnt.astype(f32), axis=0, keepdims=True)
        return jnp.where(tot >= k_top, cand_u, thr_u)

    thr_u = lax.fori_loop(0, 32, bit_body, jnp.zeros((1, LANES), i32))
    thr_s = jnp.maximum(thr_u ^ INT_MIN, INT_MIN + 1)

    m_ref[...] = jnp.full(m_ref.shape, NEG, f32)
    s_ref[...] = jnp.zeros(s_ref.shape, f32)
    acc_ref[...] = jnp.zeros(acc_ref.shape, f32)
    c2 = HEAD_DIM ** -0.5 * LOG2E

    def attn_rows(k0, nrows, carry, st=None):
        if st is None:
            st = _dot_nt(akr_ref[pl.ds(k0, nrows), :], q6_ref[...])
        bias = jnp.where(key_ref[pl.ds(k0, nrows), :] >= thr_s, 0.0, NEG)
        m_news = []
        for h in range(A_HEADS):
            sl = slice(h * LANES, (h + 1) * LANES)
            sh = st[:, sl] + bias
            stq_ref[0, 0:nrows, sl] = sh
            m_news.append(jnp.maximum(m_ref[:, sl], jnp.max(sh, axis=0, keepdims=True)))
        ps, alphas = [], []
        for h in range(A_HEADS):
            sl = slice(h * LANES, (h + 1) * LANES)
            sh = stq_ref[0, 0:nrows, sl]
            m_old = m_ref[:, sl]
            m_new = m_news[h]
            alpha = jnp.exp2((m_old - m_new) * c2)
            p = jnp.exp2((sh - m_new) * c2)
            s_ref[:, sl] = s_ref[:, sl] * alpha + jnp.sum(p, axis=0, keepdims=True)
            m_ref[:, sl] = m_new
            alphas.append(alpha)
            ps.append(p.astype(bf16))
        pv = _dot_tn(av_ref[pl.ds(k0, nrows), :], jnp.concatenate(ps, axis=1))
        acc_ref[...] = acc_ref[...] * jnp.concatenate(alphas, axis=1) + pv
        return carry

    over_keys(attn_rows, 0)

    for h in range(A_HEADS):
        sl = slice(h * LANES, (h + 1) * LANES)
        o = acc_ref[:, sl] / jnp.maximum(s_ref[:, sl], 1e-30)
        out_ref[:, sl] = o.T.astype(bf16)


def _dsa(l, zcat3, zs3, akr, ikr, tabs, gaq, t_real, k_top):
    bsz, tp, _ = zcat3.shape
    nkb = tp // BLK
    cos, sin, icos, isin = tabs
    tab = pl.BlockSpec((BLK, LANES), lambda b, i, l: (i, 0))
    full = pl.BlockSpec((None, tp, LANES), lambda b, i, l: (b, 0, 0))
    return pl.pallas_call(
        functools.partial(_dsa_kernel, t_real=t_real, k_top=k_top),
        grid_spec=pltpu.PrefetchScalarGridSpec(
            num_scalar_prefetch=1, grid=(bsz, nkb),
            in_specs=[
                pl.BlockSpec((None, BLK, A_WIDTH), lambda b, i, l: (b, i, 0)),
                pl.BlockSpec((None, BLK, IDX_HEADS * IDX_DIM), lambda b, i, l: (b, i, T_IQ * LANES // (IDX_HEADS * IDX_DIM))),
                pl.BlockSpec((None, BLK, LANES), lambda b, i, l: (b, i, 0)),
                full, full,
                pl.BlockSpec((None, tp, LANES), lambda b, i, l: (b, 0, T_AV)),
                tab, tab, tab, tab,
                pl.BlockSpec((None, 1, LANES), lambda b, i, l: (l[0], 0, 0)),
            ],
            out_specs=pl.BlockSpec((None, BLK, A_WIDTH), lambda b, i, l: (b, i, 0)),
            scratch_shapes=[
                pltpu.VMEM((tp, LANES), i32),
                pltpu.VMEM((IDX_HEADS * BLK, LANES), bf16),
                pltpu.VMEM((A_HEADS * BLK, LANES), bf16),
                pltpu.VMEM((LANES, BLK), f32),
                pltpu.VMEM((1, A_HEADS * BLK), f32),
                pltpu.VMEM((1, A_HEADS * BLK), f32),
                pltpu.VMEM((HEAD_DIM, A_HEADS * BLK), f32),
                pltpu.VMEM((2, DSA_G * BLK, A_HEADS * BLK), f32),
            ]),
        out_shape=jax.ShapeDtypeStruct((bsz, tp, A_WIDTH), bf16),
        compiler_params=_cparams(("parallel", "arbitrary")),
        name="dsa",
    )(l, zcat3, zcat3, zs3, akr, ikr, zcat3, cos, sin, icos, isin, gaq)


def _fattn_kernel(l_ref, bq_ref, bka_ref, bv_ref, gq_ref, out_ref, qa_ref, m_ref, s_ref, acc_ref, st_ref):
    i = pl.program_id(1)
    fb = bq_ref.shape[0]
    ones3 = jnp.where(lax.broadcasted_iota(i32, (fb, LANES), 1) < 3, 1.0, 0.0).astype(bf16)
    for h in range(B_HEADS):
        sl = slice(h * LANES, (h + 1) * LANES)
        qa_ref[h, :, 0:LANES] = _rms_gain(bq_ref[:, sl].astype(f32), gq_ref[...]).astype(bf16)
        qa_ref[h, :, LANES:2 * LANES] = ones3
    m_ref[...] = jnp.full(m_ref.shape, NEG, f32)
    s_ref[...] = jnp.zeros(s_ref.shape, f32)
    acc_ref[...] = jnp.zeros(acc_ref.shape, f32)
    c2 = HEAD_DIM ** -0.5 * LOG2E
    causal = lax.broadcasted_iota(i32, (fb, fb), 0) <= lax.broadcasted_iota(i32, (fb, fb), 1)

    def step(kb, diagonal):
        k0 = pl.multiple_of(kb * fb, fb)
        m_news = []
        for h in range(B_HEADS):
            st = _dot_nt(bka_ref[h, pl.ds(k0, fb), :], qa_ref[h])
            if diagonal:
                st = jnp.where(causal, st, NEG)
            st_ref[h] = st
            m_news.append(jnp.maximum(m_ref[h], jnp.max(st, axis=0, keepdims=True)))
        for h in range(B_HEADS):
            sl = slice(h * LANES, (h + 1) * LANES)
            st = st_ref[h]
            m_old = m_ref[h]
            m_new = m_news[h]
            alpha = jnp.exp2((m_old - m_new) * c2)
            p = jnp.exp2((st - m_new) * c2)
            s_ref[h] = s_ref[h] * alpha + jnp.sum(p, axis=0, keepdims=True)
            m_ref[h] = m_new
            acc_ref[h] = acc_ref[h] * alpha + _dot_tn(bv_ref[pl.ds(k0, fb), sl], p.astype(bf16))

    def body(kb, carry):
        step(kb, False)
        return carry

    lax.fori_loop(0, i, body, 0)
    step(i, True)
    for h in range(B_HEADS):
        sl = slice(h * LANES, (h + 1) * LANES)
        out_ref[:, sl] = (acc_ref[h] / s_ref[h]).T.astype(bf16)


def _fattn(l, zcat3, bka, gbq):
    bsz, tp, _ = zcat3.shape
    fb = _seq_tile(tp, 384)
    once = pl.Buffered(1)
    return pl.pallas_call(
        _fattn_kernel,
        grid_spec=pltpu.PrefetchScalarGridSpec(
            num_scalar_prefetch=1, grid=(bsz, tp // fb),
            in_specs=[
                pl.BlockSpec((None, fb, B_WIDTH), lambda b, i, l: (b, i, T_BQ * LANES // B_WIDTH)),
                pl.BlockSpec((None, B_HEADS, tp, 2 * LANES), lambda b, i, l: (b, 0, 0, 0), pipeline_mode=once),
                pl.BlockSpec((None, tp, B_WIDTH), lambda b, i, l: (b, 0, T_BV * LANES // B_WIDTH),
                             pipeline_mode=once),
                pl.BlockSpec((None, 1, LANES), lambda b, i, l: (l[0], 0, 0)),
            ],
            out_specs=pl.BlockSpec((None, fb, B_WIDTH), lambda b, i, l: (b, i, 0)),
            scratch_shapes=[
                pltpu.VMEM((B_HEADS, fb, 2 * LANES), bf16),
                pltpu.VMEM((B_HEADS, 1, fb), f32),
                pltpu.VMEM((B_HEADS, 1, fb), f32),
                pltpu.VMEM((B_HEADS, HEAD_DIM, fb), f32),
                pltpu.VMEM((B_HEADS, fb, fb), f32),
            ]),
        out_shape=jax.ShapeDtypeStruct((bsz, tp, B_WIDTH), bf16),
        compiler_params=_cparams(("parallel", "arbitrary")),
        name="fattn",
    )(l, zcat3, bka, zcat3, gbq)


def _gla_kernel(l_ref, cq_ref, ck_ref, cv_ref, cg_ref, zs_ref, w2_ref, gb_ref, og_ref, e_ref,
                out_ref, st_ref, oi_ref, *, t_real):
    c = pl.program_id(0)
    nb, cc, dk = cq_ref.shape

    @pl.when(c == 0)
    def _():
        st_ref[...] = jnp.zeros_like(st_ref)

    rowc = lax.broadcasted_iota(i32, (cc, 1), 0)
    pos = c * cc + rowc
    valid = (pos >= FRONT) & (pos < FRONT + t_real)
    tri = (lax.broadcasted_iota(i32, (cc, cc), 0) >= lax.broadcasted_iota(i32, (cc, cc), 1)).astype(bf16)
    lane_k = lax.broadcasted_iota(i32, (1, dk), 1)
    head_masks = [(lane_k >= h * C_DK) & (lane_k < (h + 1) * C_DK) for h in range(C_HEADS)]

    for b in range(nb):
        x = _dot(zs_ref[b].astype(bf16), w2_ref[...]) + gb_ref[...]
        la = jnp.where(valid, _log_sigmoid(x) * (1.0 / C_TAU), 0.0)
        hi, mid, lo = _split3(la)
        bc = _dot(tri, hi) + _dot(tri, mid) + _dot(tri, lo)
        q = cq_ref[b].astype(f32) * (C_DK ** -0.5)
        k = ck_ref[b].astype(f32)
        v = cv_ref[b]
        vf = v.astype(f32)
        blast = bc[cc - 1:cc, :]
        qe = q * jnp.exp(bc)
        ke = k * jnp.exp(blast - bc)
        st = st_ref[b]
        stb = st.astype(bf16)

        o_inter = jnp.concatenate(
            [_dot_nt(jnp.where(head_masks[h], qe, 0.0).astype(bf16), stb) for h in range(C_HEADS)], axis=1)

        rows = []
        for r in range(cc):
            dec = jnp.exp(jnp.minimum(bc[r:r + 1, :] - bc, 0.0))
            rows.append(jnp.where(rowc <= r, q[r:r + 1, :] * k * dec, 0.0).astype(bf16))
        rr = _dot(jnp.concatenate(rows, axis=0), e_ref[...])
        for r in range(cc):
            oi_ref[r:r + 1, :] = jnp.sum(rr[r * cc:(r + 1) * cc, :] * vf, axis=0, keepdims=True)
        o = o_inter + oi_ref[...]

        new_st = st * jnp.exp(blast)
        for h in range(C_HEADS):
            km = jnp.where(head_masks[h], ke, 0.0).astype(bf16)
            new_st = new_st + _dot_tn(v[:, h * C_DV:(h + 1) * C_DV], km)
        st_ref[b] = new_st

        g = cg_ref[b].astype(f32)
        gs = g * (1.0 / (1.0 + jnp.exp(-g)))
        for h in range(C_HEADS):
            sl = slice(h * C_DV, (h + 1) * C_DV)
            out_ref[b, :, sl] = (_rms_gain(o[:, sl], og_ref[...]) * gs[:, sl]).astype(bf16)


def _gla(l, zcat3, zs3, w2p, gb, og, emat, t_real):
    bsz, tp, _ = zcat3.shape
    cc = GLA_C
    dk = C_HEADS * C_DK
    return pl.pallas_call(
        functools.partial(_gla_kernel, t_real=t_real),
        grid_spec=pltpu.PrefetchScalarGridSpec(
            num_scalar_prefetch=1, grid=(tp // cc,),
            in_specs=[
                pl.BlockSpec((bsz, cc, dk), lambda c, l: (0, c, T_CQ * LANES // dk)),
                pl.BlockSpec((bsz, cc, dk), lambda c, l: (0, c, T_CK * LANES // dk)),
                pl.BlockSpec((bsz, cc, C_WIDTH), lambda c, l: (0, c, T_CV * LANES // C_WIDTH)),
                pl.BlockSpec((bsz, cc, C_WIDTH), lambda c, l: (0, c, T_CG * LANES // C_WIDTH)),
                pl.BlockSpec((bsz, cc, LANES), lambda c, l: (0, c, 0)),
                pl.BlockSpec((None, LANES, dk), lambda c, l: (l[0], 0, 0)),
                pl.BlockSpec((None, 1, dk), lambda c, l: (l[0], 0, 0)),
                pl.BlockSpec((None, 1, C_DV), lambda c, l: (l[0], 0, 0)),
                pl.BlockSpec((dk, C_WIDTH), lambda c, l: (0, 0)),
            ],
            out_specs=pl.BlockSpec((bsz, cc, C_WIDTH), lambda c, l: (0, c, 0)),
            scratch_shapes=[pltpu.VMEM((bsz, C_DV, dk), f32), pltpu.VMEM((cc, C_WIDTH), f32)]),
        out_shape=jax.ShapeDtypeStruct((bsz, tp, C_WIDTH), bf16),
        compiler_params=_cparams(("arbitrary",)),
        name="gla",
    )(l, zcat3, zcat3, zcat3, zcat3, zs3, w2p, gb, og, emat)


def _outproj_kernel(l_ref, oa_ref, ob_ref, oc_ref, h_ref, wo_ref, g_ref, wr_ref, br_ref,
                    hmid_ref, u_ref, eid_ref, ew_ref, *, t_real):
    tm = h_ref.shape[0]
    mix = (_dot(oa_ref[...], wo_ref[0:A_WIDTH, :])
           + _dot(ob_ref[...], wo_ref[A_WIDTH:A_WIDTH + B_WIDTH, :])
           + _dot(oc_ref[...], wo_ref[A_WIDTH + B_WIDTH:, :]))
    pos = pl.program_id(1) * tm + lax.broadcasted_iota(i32, (tm, 1), 0)
    valid = (pos >= FRONT) & (pos < FRONT + t_real)
    hm = h_ref[...] + jnp.where(valid, mix, 0.0)
    hmid_ref[...] = hm
    u = _rms_gain(hm, g_ref[...]).astype(bf16)
    u_ref[...] = u

    logits = _dot(u, wr_ref[...]) + br_ref[...]
    lane = lax.broadcasted_iota(i32, (tm, LANES), 1)
    lanef = lane.astype(f32)
    big = float(4 * LANES)
    first = lambda hit: jnp.min(jnp.where(hit, lanef, big), axis=-1, keepdims=True).astype(i32)
    gl = jnp.where(lane < R_GRP + N_GROUPS, logits, -jnp.inf)
    gmax = jnp.max(gl, axis=-1, keepdims=True)
    g_p = 1.0 / jnp.sum(jnp.exp(gl - gmax), axis=-1, keepdims=True)
    g_i = first(gl == gmax)
    e_lane = lane - R_EXP
    emask = (e_lane >= 0) & (e_lane < N_EXPERTS) & ((e_lane >> 3) == g_i)
    el = jnp.where(emask, logits, -jnp.inf)
    m1 = jnp.max(el, axis=-1, keepdims=True)
    i1 = first(el == m1)
    el2 = jnp.where(lane == i1, -jnp.inf, el)
    m2 = jnp.max(el2, axis=-1, keepdims=True)
    i2 = first(el2 == m2)
    r = jnp.exp(m2 - m1)
    w1 = g_p / (1.0 + r)
    w2 = g_p * r / (1.0 + r)
    eid_ref[...] = jnp.where(lane == 0, i1 - R_EXP, jnp.where(lane == 1, i2 - R_EXP, 0))
    ew_ref[...] = jnp.where(lane == 0, w1, jnp.where(lane == 1, w2, 0.0))


def _seq_tile(tp, cap):
    return max(t for t in range(BLK, cap + 1, BLK) if tp % t == 0)


def _outproj(l, oa, ob, oc, h, wo, gain, wr, br, t_real, tp):
    n, d = h.shape
    tm = _seq_tile(tp, 384)
    nj = tp // tm
    row = lambda w: pl.BlockSpec((tm, w), lambda b, j, l: (b * nj + j, 0))
    return pl.pallas_call(
        functools.partial(_outproj_kernel, t_real=t_real),
        grid_spec=pltpu.PrefetchScalarGridSpec(
            num_scalar_prefetch=1, grid=(n // tp, nj),
            in_specs=[row(A_WIDTH), row(B_WIDTH), row(C_WIDTH), row(d),
                      pl.BlockSpec((None, d, d), lambda b, j, l: (l[0], 0, 0)),
                      pl.BlockSpec((None, 1, d), lambda b, j, l: (l[0], 0, 0)),
                      pl.BlockSpec((None, d, LANES), lambda b, j, l: (l[0], 0, 0)),
                      pl.BlockSpec((None, 1, LANES), lambda b, j, l: (l[0], 0, 0))],
            out_specs=[row(d), row(d), row(LANES), row(LANES)]),
        out_shape=[jax.ShapeDtypeStruct((n, d), f32), jax.ShapeDtypeStruct((n, d), bf16),
                   jax.ShapeDtypeStruct((n, LANES), i32), jax.ShapeDtypeStruct((n, LANES), f32)],
        compiler_params=_cparams(("parallel", "parallel")),
        name="outproj",
    )(l, oa, ob, oc, h, wo, gain, wr, br)


def _moe_kernel(l_ref, te_ref, na_ref, xs_ref, w1_ref, w3_ref, w2_ref, rw_ref, ys_ref,
                w1b_ref, w3b_ref, w2b_ref):
    i = pl.program_id(0)

    @pl.when((i == 0) | (te_ref[i] != te_ref[jnp.maximum(i - 1, 0)]))
    def _():
        w1b_ref[...] = w1_ref[...].astype(bf16)
        w3b_ref[...] = w3_ref[...].astype(bf16)
        w2b_ref[...] = w2_ref[...].astype(bf16)

    @pl.when(i < na_ref[0])
    def _():
        x = xs_ref[...]
        h1 = _dot(x, w1b_ref[...])
        h3 = _dot(x, w3b_ref[...])
        hid = (h1 * (1.0 / (1.0 + jnp.exp(-h1))) * h3).astype(bf16)
        ys_ref[...] = (_dot(hid, w2b_ref[...]) * rw_ref[...]).astype(bf16)

    @pl.when(i >= na_ref[0])
    def _():
        ys_ref[...] = jnp.zeros_like(ys_ref)


def _moe(l, te, na, xs, w1, w3, w2, rw):
    p, d = xs.shape
    tm = MOE_TM
    return pl.pallas_call(
        _moe_kernel,
        grid_spec=pltpu.PrefetchScalarGridSpec(
            num_scalar_prefetch=3, grid=(p // tm,),
            in_specs=[
                pl.BlockSpec((tm, d), lambda i, l, te, na: (i, 0)),
                pl.BlockSpec((None, None, d, D_EXPERT), lambda i, l, te, na: (l[0], te[i], 0, 0)),
                pl.BlockSpec((None, None, d, D_EXPERT), lambda i, l, te, na: (l[0], te[i], 0, 0)),
                pl.BlockSpec((None, None, D_EXPERT, d), lambda i, l, te, na: (l[0], te[i], 0, 0)),
                pl.BlockSpec((tm, 1), lambda i, l, te, na: (i, 0)),
            ],
            out_specs=pl.BlockSpec((tm, d), lambda i, l, te, na: (i, 0)),
            scratch_shapes=[pltpu.VMEM((d, D_EXPERT), bf16), pltpu.VMEM((d, D_EXPERT), bf16),
                            pltpu.VMEM((D_EXPERT, d), bf16)]),
        out_shape=jax.ShapeDtypeStruct((p, d), bf16),
        compiler_params=_cparams(("arbitrary",)),
        name="moe",
    )(l, te, na, xs, w1, w3, w2, rw)


def _route(eid, ew, n_tiles):
    tm = MOE_TM
    p = n_tiles * tm
    n = eid.shape[0]
    e_flat = eid[:, :EXP_TOPK_CONST].T.reshape(-1)
    w_flat = ew[:, :EXP_TOPK_CONST].T.reshape(-1)
    na_all = e_flat.shape[0]
    hp = lax.Precision.HIGHEST
    ex = jnp.arange(N_EXPERTS, dtype=i32)[:, None]
    onehot = (ex == e_flat[None, :]).astype(f32)
    counts = jnp.sum(onehot, axis=1).astype(i32)
    pc = ((counts + tm - 1) // tm) * tm
    pend = jnp.cumsum(pc)
    po = pend - pc
    co = jnp.cumsum(counts) - counts
    order = jnp.argsort(e_flat, stable=True).astype(i32)
    r = jnp.arange(p, dtype=i32)
    step = (r[None, :] >= pend[:, None]).astype(f32)
    dlt = lambda v: jnp.concatenate([v[1:] - v[:-1], jnp.zeros((1,), v.dtype)]).astype(f32)
    tabs = jnp.stack([jnp.ones((N_EXPERTS,), f32), dlt(po), dlt(counts), dlt(co)])
    picked = jnp.dot(tabs, step, precision=hp).astype(i32)
    e_r = jnp.minimum(picked[0], N_EXPERTS - 1)
    local = r - (po[0] + picked[1])
    valid_r = (local < counts[0] + picked[2]) & (r < pend[-1])
    a_r = order[jnp.clip(co[0] + picked[3] + local, 0, na_all - 1)]
    tok_r = jnp.where(valid_r, jnp.where(a_r >= n, a_r - n, a_r), r % n)
    w_r = jnp.where(valid_r, w_flat[a_r], 0.0)
    n_act = (pend[-1] // tm).astype(i32)
    tile_e = e_r[::tm]
    te = jnp.where(jnp.arange(n_tiles, dtype=i32) < n_act, tile_e, tile_e[jnp.maximum(n_act - 1, 0)])
    inv = jnp.argsort(order).astype(i32)
    pos_a = jnp.dot((po - co).astype(f32)[None, :], onehot, precision=hp)[0].astype(i32) + inv
    return tok_r, w_r[:, None], te, n_act.reshape(1), pos_a


def _final_kernel(hmid_ref, ya_ref, yb_ref, out_ref):
    out_ref[...] = hmid_ref[...] + ya_ref[...].astype(f32) + yb_ref[...].astype(f32)


def _final(hmid, y2):
    n, d = hmid.shape
    tm = _row_tile(n, 512)
    nt = n // tm
    return pl.pallas_call(
        _final_kernel, grid=(nt,),
        in_specs=[pl.BlockSpec((tm, d), lambda i: (i, 0)), pl.BlockSpec((tm, d), lambda i: (i, 0)),
                  pl.BlockSpec((tm, d), lambda i: (i + nt, 0))],
        out_specs=pl.BlockSpec((tm, d), lambda i: (i, 0)),
        out_shape=jax.ShapeDtypeStruct((n, d), f32),
        compiler_params=_cparams(("parallel",)),
        name="final",
    )(hmid, y2, y2)


def _rope_tables(tp):
    pos = (jnp.arange(tp, dtype=f32) - FRONT)[:, None]

    def tab(dim, reps):
        inv = 1.0 / (ROPE_THETA ** (jnp.arange(0, dim, 2, dtype=f32) / dim))
        ang = pos * inv[None, :]
        return jnp.tile(jnp.cos(ang), (1, reps)), jnp.tile(jnp.sin(ang), (1, reps))

    cos, sin = tab(HEAD_DIM, 2)
    icos, isin = tab(IDX_DIM, 4)
    return cos, sin, icos, isin


def _split_w_in(w_in):
    parts, off = [], 0
    for s in IN_SPLITS:
        parts.append(w_in[..., off:off + s])
        off += s
    return parts


def kernel(x, meta, norm_mix_g, norm_ffn_g, w_in, a_qn_g, a_kn_g, b_qn_g, b_kn_g, b_f_bias,
           c_gate_w2, c_gate_b, c_on_g, w_out, r_group_w, r_group_b, r_exp_w, r_exp_b,
           e_w1, e_w3, e_w2):
    bsz, n_seq, d = x.shape
    depth = w_in.shape[0]
    t_real = n_seq + N_META
    k_top = min(TOPK_MAX, n_seq // 4)
    tp = -(-(FRONT + t_real) // BLK) * BLK
    n = bsz * tp

    (waq, wak, wav, wiq, wik, wiw, wbq, wbk, wbv, wbf, wcq, wck, wcv, wcr, wcg) = _split_w_in(w_in)
    zc = lambda w: jnp.zeros((depth, d, w), w_in.dtype)
    wcat = jnp.concatenate([waq, wak, wav, wiq, wik, zc(LANES - IDX_DIM), zc(LANES),
                            wbq, wbk, wbv, wcq, wck, wcv, wcg], axis=-1).astype(bf16)
    ws = jnp.concatenate([wiw, wbf, wcr, zc(LANES - S_CR - C_GATE_RANK)], axis=-1).astype(bf16)
    wo = w_out.astype(bf16)
    wr = jnp.concatenate([r_group_w, zc(R_EXP - N_GROUPS), r_exp_w, zc(LANES - R_EXP - N_EXPERTS)],
                         axis=-1).astype(bf16)
    zl = lambda w: jnp.zeros((depth, w), f32)
    br = jnp.concatenate([r_group_b, zl(R_EXP - N_GROUPS), r_exp_b, zl(LANES - R_EXP - N_EXPERTS)],
                         axis=-1)[:, None, :]
    fbias = jnp.concatenate([zl(S_BF), b_f_bias, zl(LANES - S_BF - B_HEADS)], axis=-1)[:, None, :]
    dk = C_HEADS * C_DK
    w2p = jnp.concatenate([jnp.zeros((depth, S_CR, dk), f32), c_gate_w2,
                           jnp.zeros((depth, LANES - S_CR - C_GATE_RANK, dk), f32)], axis=1).astype(bf16)
    gb = c_gate_b[:, None, :]
    og = c_on_g[:, None, :]
    g_mix = norm_mix_g[:, None, :]
    g_ffn = norm_ffn_g[:, None, :]
    gaq, gak, gbq, gbk = (g[:, None, :] for g in (a_qn_g, a_kn_g, b_qn_g, b_kn_g))
    emat = (jnp.arange(dk, dtype=i32)[:, None] // C_DK
            == jnp.arange(C_WIDTH, dtype=i32)[None, :] // C_DV).astype(bf16)
    tabs = _rope_tables(tp)

    h0 = jnp.concatenate([
        jnp.zeros((bsz, FRONT, d), f32),
        jnp.broadcast_to(meta[None].astype(f32), (bsz, N_META, d)),
        x.astype(f32),
        jnp.zeros((bsz, tp - FRONT - t_real, d), f32)], axis=1).reshape(n, d)
    n_tiles = -(-(EXP_TOPK_CONST * n + N_EXPERTS * (MOE_TM - 1)) // MOE_TM)

    def layer(li, carry):
        hmid, y2 = carry
        l = jnp.reshape(jnp.asarray(li, i32), (1,))
        h, zcat, zs = _inproj(l, hmid, y2, g_mix, wcat, ws)
        zcat3 = zcat.reshape(bsz, tp, N_CAT)
        zs3 = zs.reshape(bsz, tp, LANES)
        akr, ikr, bka = _kprep(l, zcat3, zs3, tabs, gak, gbk, fbias, t_real)
        oa = _dsa(l, zcat3, zs3, akr, ikr, tabs, gaq, t_real, k_top)
        ob = _fattn(l, zcat3, bka, gbq)
        oc = _gla(l, zcat3, zs3, w2p, gb, og, emat, t_real)
        hmid2, u, eid, ew = _outproj(l, oa.reshape(n, A_WIDTH), ob.reshape(n, B_WIDTH),
                                     oc.reshape(n, C_WIDTH), h, wo, g_ffn, wr, br, t_real, tp)
        tok_r, w_r, te, n_act, pos_a = _route(eid, ew, n_tiles)
        ys = _moe(l, te, n_act, jnp.take(u, tok_r, axis=0, mode="clip"), e_w1, e_w3, e_w2, w_r)
        return hmid2, jnp.take(ys, pos_a, axis=0, mode="clip")

    y0 = lax.optimization_barrier(jnp.zeros((EXP_TOPK_CONST * n, d), bf16))
    hmid, y2 = lax.fori_loop(0, depth, layer, (h0, y0))
    out = _final(hmid, y2).reshape(bsz, tp, d)
    return out[:, FRONT + N_META:FRONT + t_real].astype(x.dtype)
```

```python
import functools

import jax
import jax.numpy as jnp
from jax import lax
from jax.experimental import pallas as pl
from jax.experimental.pallas import tpu as pltpu

f32 = jnp.float32
bf16 = jnp.bfloat16
i32 = jnp.int32
i16 = jnp.int16

D_MODEL = 2048
CHUNK = 64
N_META = 16
ROPE_THETA = 10000.0
EPS = 1e-6
HEAD_DIM = 128
A_HEADS = 6
IDX_HEADS = 16
IDX_DIM = 64
TOPK_MAX = 256
B_HEADS = 6
C_HEADS = 4
C_DK = 64
C_DV = 128
C_GATE_RANK = 16
C_TAU = 16.0
N_GROUPS = 4
EXP_PER_GROUP = 8
N_EXPERTS = N_GROUPS * EXP_PER_GROUP
D_EXPERT = 512
A_WIDTH = A_HEADS * HEAD_DIM
B_WIDTH = B_HEADS * HEAD_DIM
C_WIDTH = C_HEADS * C_DV
IN_SPLITS = (A_WIDTH, HEAD_DIM, HEAD_DIM, IDX_HEADS * IDX_DIM, IDX_DIM, IDX_HEADS,
             B_WIDTH, B_WIDTH, B_WIDTH, B_HEADS,
             C_HEADS * C_DK, C_HEADS * C_DK, C_WIDTH, C_GATE_RANK, C_WIDTH)

LANES = 128
FRONT = (-N_META) % CHUNK
BLK = 128
T_AQ, T_AK, T_AV, T_IQ, T_IK = 0, 6, 7, 8, 16
T_BQ, T_BK, T_BV = 18, 24, 30
T_CQ, T_CK, T_CV, T_CG = 36, 38, 40, 44
N_CAT = 48 * LANES
S_IW, S_BF, S_CR = 0, 16, 22
R_GRP, R_EXP = 0, 32
INPROJ_TN = 1536
GLA_C = 32
DSA_G = 3
MOE_TM = 256
EXP_TOPK_CONST = 2
NEG = -1e30
LOG2E = 1.4426950408889634
INT_MIN = -2 ** 31
HALF = 2 ** 15
VMEM_LIMIT = 56 * 1024 * 1024


def _cparams(sem):
    return pltpu.CompilerParams(dimension_semantics=sem, vmem_limit_bytes=VMEM_LIMIT)


def _row_tile(n, cap):
    t = cap
    while n % t:
        t //= 2
    return t


def _log_sigmoid(x):
    return jnp.minimum(x, 0.0) - jnp.log(1.0 + jnp.exp(-jnp.abs(x)))


def _dot(a, b):
    return jnp.dot(a, b, preferred_element_type=f32)


def _dot_nt(a, b):
    return lax.dot_general(a, b, (((1,), (1,)), ((), ())), preferred_element_type=f32)


def _dot_tn(a, b):
    return lax.dot_general(a, b, (((0,), (0,)), ((), ())), preferred_element_type=f32)


def _split3(x):
    hi = x.astype(bf16)
    r1 = x - hi.astype(f32)
    mid = r1.astype(bf16)
    lo = (r1 - mid.astype(f32)).astype(bf16)
    return hi, mid, lo


def _rope128(x, cos, sin, lane):
    return x * cos + pltpu.roll(x, 64, 1) * jnp.where(lane < 64, -sin, sin)


def _rope64(x, cos, sin, lane):
    low = (lane & 63) < 32
    return (x * cos + pltpu.roll(x, 32, 1) * jnp.where(low, 0.0, sin)
            + pltpu.roll(x, 96, 1) * jnp.where(low, -sin, 0.0))


def _rms_gain(x, g):
    return x * lax.rsqrt(jnp.mean(x * x, axis=-1, keepdims=True) + EPS) * g


def _prenorm_kernel(l_ref, hmid_ref, ya_ref, yb_ref, g_ref, ws_ref, h_ref, xn_ref, zs_ref):
    h = hmid_ref[...] + ya_ref[...].astype(f32) + yb_ref[...].astype(f32)
    h_ref[...] = h
    xn = _rms_gain(h, g_ref[...]).astype(bf16)
    xn_ref[...] = xn
    zs_ref[...] = _dot(xn, ws_ref[...])


def _inproj_kernel(l_ref, xn_ref, w_ref, z_ref):
    z_ref[...] = _dot(xn_ref[...], w_ref[...]).astype(bf16)


def _inproj(l, hmid, y2, gain, wcat, ws):
    n, d = hmid.shape
    tm = _row_tile(n, 512)
    nt = n // tm
    row = lambda w: pl.BlockSpec((tm, w), lambda i, l: (i, 0))
    h, xn, zs = pl.pallas_call(
        _prenorm_kernel,
        grid_spec=pltpu.PrefetchScalarGridSpec(
            num_scalar_prefetch=1, grid=(nt,),
            in_specs=[row(d), row(d),
                      pl.BlockSpec((tm, d), lambda i, l: (i + nt, 0)),
                      pl.BlockSpec((None, 1, d), lambda i, l: (l[0], 0, 0)),
                      pl.BlockSpec((None, d, LANES), lambda i, l: (l[0], 0, 0))],
            out_specs=[row(d), row(d), row(LANES)]),
        out_shape=[jax.ShapeDtypeStruct((n, d), f32), jax.ShapeDtypeStruct((n, d), bf16),
                   jax.ShapeDtypeStruct((n, LANES), f32)],
        compiler_params=_cparams(("parallel",)),
        name="prenorm",
    )(l, hmid, y2, y2, gain, ws)
    tn = INPROJ_TN
    zcat = pl.pallas_call(
        _inproj_kernel,
        grid_spec=pltpu.PrefetchScalarGridSpec(
            num_scalar_prefetch=1, grid=(N_CAT // tn, nt),
            in_specs=[pl.BlockSpec((tm, d), lambda j, i, l: (i, 0)),
                      pl.BlockSpec((None, d, tn), lambda j, i, l: (l[0], 0, j))],
            out_specs=pl.BlockSpec((tm, tn), lambda j, i, l: (i, j))),
        out_shape=jax.ShapeDtypeStruct((n, N_CAT), bf16),
        compiler_params=_cparams(("parallel", "parallel")),
        name="inproj",
    )(l, xn, wcat)
    return h, zcat, zs


def _kprep_kernel(l_ref, ak_ref, ik_ref, bk_ref, zs_ref,
                  cos_ref, sin_ref, icos_ref, isin_ref, gak_ref, gbk_ref, fb_ref,
                  akr_ref, ikr_ref, bka_ref, carry_ref, *, t_real):
    k = pl.program_id(1)
    lane = lax.broadcasted_iota(i32, (BLK, LANES), 1)
    row = lax.broadcasted_iota(i32, (BLK, LANES), 0)
    pos = k * BLK + row
    valid = (pos >= FRONT) & (pos < FRONT + t_real)

    x = ak_ref[...].astype(f32)
    akr_ref[...] = _rope128(_rms_gain(x, gak_ref[...]), cos_ref[...], sin_ref[...], lane).astype(bf16)
    ikr_ref[...] = _rope64(ik_ref[...].astype(f32), icos_ref[...], isin_ref[...], lane).astype(bf16)

    @pl.when(k == 0)
    def _():
        carry_ref[...] = jnp.zeros_like(carry_ref)

    lf = jnp.where(valid, _log_sigmoid(zs_ref[...] + fb_ref[...]), 0.0)
    tri = (row >= lane).astype(bf16)
    hi, mid, lo = _split3(lf)
    fcum = _dot(tri, hi) + _dot(tri, mid) + _dot(tri, lo) + carry_ref[...]
    carry_ref[...] = fcum[BLK - 1:BLK, :]

    fs = jnp.where(valid, fcum * (-(HEAD_DIM ** 0.5)), NEG)
    for h in range(B_HEADS):
        sl = slice(h * LANES, (h + 1) * LANES)
        bka_ref[h, :, 0:LANES] = _rms_gain(bk_ref[:, sl].astype(f32), gbk_ref[...]).astype(bf16)
        p0, p1, p2 = _split3(fs[:, S_BF + h:S_BF + h + 1])
        aug = jnp.where(lane == 0, p0.astype(f32),
                        jnp.where(lane == 1, p1.astype(f32), jnp.where(lane == 2, p2.astype(f32), 0.0)))
        bka_ref[h, :, LANES:2 * LANES] = aug.astype(bf16)


def _kprep(l, zcat3, zs3, tabs, gak, gbk, fbias, t_real):
    bsz, tp, _ = zcat3.shape
    nkb = tp // BLK
    cos, sin, icos, isin = tabs
    tile = lambda c: pl.BlockSpec((None, BLK, LANES), lambda b, k, l, c=c: (b, k, c))
    wide = lambda c: pl.BlockSpec((None, BLK, B_WIDTH), lambda b, k, l, c=c: (b, k, c))
    tab = pl.BlockSpec((BLK, LANES), lambda b, k, l: (k, 0))
    gain = pl.BlockSpec((None, 1, LANES), lambda b, k, l: (l[0], 0, 0))
    return pl.pallas_call(
        functools.partial(_kprep_kernel, t_real=t_real),
        grid_spec=pltpu.PrefetchScalarGridSpec(
            num_scalar_prefetch=1, grid=(bsz, nkb),
            in_specs=[tile(T_AK), tile(T_IK),
                      wide(T_BK * LANES // B_WIDTH),
                      pl.BlockSpec((None, BLK, LANES), lambda b, k, l: (b, k, 0)),
                      tab, tab, tab, tab, gain, gain, gain],
            out_specs=[
                pl.BlockSpec((None, BLK, LANES), lambda b, k, l: (b, k, 0)),
                pl.BlockSpec((None, BLK, LANES), lambda b, k, l: (b, k, 0)),
                pl.BlockSpec((None, B_HEADS, BLK, 2 * LANES), lambda b, k, l: (b, 0, k, 0)),
            ],
            scratch_shapes=[pltpu.VMEM((1, LANES), f32)]),
        out_shape=[jax.ShapeDtypeStruct((bsz, tp, LANES), bf16),
                   jax.ShapeDtypeStruct((bsz, tp, LANES), bf16),
                   jax.ShapeDtypeStruct((bsz, B_HEADS, tp, 2 * LANES), bf16)],
        compiler_params=_cparams(("parallel", "arbitrary")),
        name="kprep",
    )(l, zcat3, zcat3, zcat3, zs3, cos, sin, icos, isin, gak, gbk, fbias)


def _dsa_kernel(l_ref, aq_ref, iq_ref, zs_ref, akr_ref, ikr_ref, av_ref,
                cos_ref, sin_ref, icos_ref, isin_ref, gq_ref, out_ref,
                key_ref, khi_ref, klo_ref, iqs_ref, q6_ref, iwt_ref, m_ref, s_ref, acc_ref, stq_ref,
                *, t_real, k_top):
    i = pl.program_id(1)
    nk = i + 1
    lane = lax.broadcasted_iota(i32, (BLK, LANES), 1)

    for h in range(A_HEADS):
        x = aq_ref[:, h * LANES:(h + 1) * LANES].astype(f32)
        xr = _rope128(_rms_gain(x, gq_ref[...]), cos_ref[...], sin_ref[...], lane)
        q6_ref[h * BLK:(h + 1) * BLK, :] = xr.astype(bf16)
    for t in range(IDX_HEADS // 2):
        x = iq_ref[:, t * LANES:(t + 1) * LANES].astype(f32)
        xr = _rope64(x, icos_ref[...], isin_ref[...], lane)
        iqs_ref[(2 * t) * BLK:(2 * t + 1) * BLK, :] = jnp.where(lane < IDX_DIM, xr, 0.0).astype(bf16)
        iqs_ref[(2 * t + 1) * BLK:(2 * t + 2) * BLK, :] = jnp.where(
            lane < IDX_DIM, pltpu.roll(xr, 64, 1), 0.0).astype(bf16)
    iwt_ref[...] = (zs_ref[...] * (IDX_HEADS ** -0.5 * IDX_DIM ** -0.5)).T

    gb = DSA_G * BLK
    ng = lax.div(nk, jnp.int32(DSA_G))
    nr = nk - ng * DSA_G

    def over_keys(fn, init):
        c = lax.fori_loop(0, ng, lambda g, c: fn(pl.multiple_of(g * gb, gb), gb, c), init)
        return lax.fori_loop(0, nr, lambda r, c: fn(pl.multiple_of((ng * DSA_G + r) * BLK, BLK), BLK, c), c)

    def score_rows(k0, nrows, carry):
        dt = _dot_nt(ikr_ref[pl.ds(k0, nrows), :], iqs_ref[...])
        s = jnp.zeros((nrows, LANES), f32)
        for h in range(IDX_HEADS):
            s = s + iwt_ref[h:h + 1, :] * jnp.maximum(dt[:, h * LANES:(h + 1) * LANES], 0.0)
        kpos = k0 + lax.broadcasted_iota(i32, (nrows, LANES), 0)
        qpos = i * BLK + lax.broadcasted_iota(i32, (nrows, LANES), 1)
        adm = ((kpos >> 6) <= (qpos >> 6)) & (kpos >= FRONT) & (kpos < FRONT + t_real)
        bits = lax.bitcast_convert_type(s, i32)
        key = bits ^ ((bits >> 31) & 0x7FFFFFFF)
        key = jnp.where(adm, key, INT_MIN)
        key_ref[pl.ds(k0, nrows), :] = key
        khi_ref[pl.ds(k0, nrows), :] = (key >> 16).astype(i16)
        klo_ref[pl.ds(k0, nrows), :] = ((key & 0xFFFF) - HALF).astype(i16)
        return carry

    over_keys(score_rows, 0)

    def count_hits(hit_fn):
        def rows(k0, nrows, c):
            hit = hit_fn(k0, nrows)
            for j in range(nrows // BLK):
                c = c + hit[j * BLK:(j + 1) * BLK, :]
            return c
        cnt = over_keys(rows, jnp.zeros((BLK, LANES), i16))
        return jnp.sum(cnt.astype(f32), axis=0, keepdims=True)

    one, zero = jnp.int16(1), jnp.int16(0)

    def kth_largest_16(src_ref, k_need):
        def bit_body(t, thr_u):
            cand_u = thr_u | jnp.left_shift(jnp.int32(1), 15 - t)
            cand = (cand_u - HALF).astype(i16)
            tot = count_hits(lambda k0, nrows: jnp.where(src_ref[pl.ds(k0, nrows), :] >= cand, one, zero))
            return jnp.where(tot >= k_need, cand_u, thr_u)
        return lax.fori_loop(0, 16, bit_body, jnp.zeros((1, LANES), i32))

    hi_u = kth_largest_16(khi_ref, float(k_top))
    hi_s = (hi_u - HALF).astype(i16)

    def above_and_mask(k0, nrows):
        hi = khi_ref[pl.ds(k0, nrows), :]
        lo = klo_ref[pl.ds(k0, nrows), :]
        klo_ref[pl.ds(k0, nrows), :] = jnp.where(hi == hi_s, lo, jnp.int16(-HALF))
        return jnp.where(hi > hi_s, one, zero)

    n_above = count_hits(above_and_mask)
    lo_u = kth_largest_16(klo_ref, float(k_top) - n_above)
    thr_key = jnp.left_shift(hi_u - HALF, 16) | lo_u
    thr_s = jnp.maximum(thr_key, INT_MIN + 1)

    m_ref[...] = jnp.full(m_ref.shape, NEG, f32)
    s_ref[...] = jnp.zeros(s_ref.shape, f32)
    acc_ref[...] = jnp.zeros(acc_ref.shape, f32)
    c2 = HEAD_DIM ** -0.5 * LOG2E

    def attn_rows(k0, nrows, carry, st=None):
        if st is None:
            st = _dot_nt(akr_ref[pl.ds(k0, nrows), :], q6_ref[...])
        bias = jnp.where(key_ref[pl.ds(k0, nrows), :] >= thr_s, 0.0, NEG)
        m_news = []
        for h in range(A_HEADS):
            sl = slice(h * LANES, (h + 1) * LANES)
            sh = st[:, sl] + bias
            stq_ref[0, 0:nrows, sl] = sh
            m_news.append(jnp.maximum(m_ref[:, sl], jnp.max(sh, axis=0, keepdims=True)))
        ps, alphas = [], []
        for h in range(A_HEADS):
            sl = slice(h * LANES, (h + 1) * LANES)
            sh = stq_ref[0, 0:nrows, sl]
            m_old = m_ref[:, sl]
            m_new = m_news[h]
            alpha = jnp.exp2((m_old - m_new) * c2)
            p = jnp.exp2((sh - m_new) * c2)
            s_ref[:, sl] = s_ref[:, sl] * alpha + jnp.sum(p, axis=0, keepdims=True)
            m_ref[:, sl] = m_new
            alphas.append(alpha)
            ps.append(p.astype(bf16))
        pv = _dot_tn(av_ref[pl.ds(k0, nrows), :], jnp.concatenate(ps, axis=1))
        acc_ref[...] = acc_ref[...] * jnp.concatenate(alphas, axis=1) + pv
        return carry

    over_keys(attn_rows, 0)

    for h in range(A_HEADS):
        sl = slice(h * LANES, (h + 1) * LANES)
        o = acc_ref[:, sl] / jnp.maximum(s_ref[:, sl], 1e-30)
        out_ref[:, sl] = o.T.astype(bf16)


def _dsa(l, zcat3, zs3, akr, ikr, tabs, gaq, t_real, k_top):
    bsz, tp, _ = zcat3.shape
    nkb = tp // BLK
    cos, sin, icos, isin = tabs
    tab = pl.BlockSpec((BLK, LANES), lambda b, i, l: (i, 0))
    full = pl.BlockSpec((None, tp, LANES), lambda b, i, l: (b, 0, 0))
    return pl.pallas_call(
        functools.partial(_dsa_kernel, t_real=t_real, k_top=k_top),
        grid_spec=pltpu.PrefetchScalarGridSpec(
            num_scalar_prefetch=1, grid=(bsz, nkb),
            in_specs=[
                pl.BlockSpec((None, BLK, A_WIDTH), lambda b, i, l: (b, i, 0)),
                pl.BlockSpec((None, BLK, IDX_HEADS * IDX_DIM), lambda b, i, l: (b, i, T_IQ * LANES // (IDX_HEADS * IDX_DIM))),
                pl.BlockSpec((None, BLK, LANES), lambda b, i, l: (b, i, 0)),
                full, full,
                pl.BlockSpec((None, tp, LANES), lambda b, i, l: (b, 0, T_AV)),
                tab, tab, tab, tab,
                pl.BlockSpec((None, 1, LANES), lambda b, i, l: (l[0], 0, 0)),
            ],
            out_specs=pl.BlockSpec((None, BLK, A_WIDTH), lambda b, i, l: (b, i, 0)),
            scratch_shapes=[
                pltpu.VMEM((tp, LANES), i32),
                pltpu.VMEM((tp, LANES), i16),
                pltpu.VMEM((tp, LANES), i16),
                pltpu.VMEM((IDX_HEADS * BLK, LANES), bf16),
                pltpu.VMEM((A_HEADS * BLK, LANES), bf16),
                pltpu.VMEM((LANES, BLK), f32),
                pltpu.VMEM((1, A_HEADS * BLK), f32),
                pltpu.VMEM((1, A_HEADS * BLK), f32),
                pltpu.VMEM((HEAD_DIM, A_HEADS * BLK), f32),
                pltpu.VMEM((2, DSA_G * BLK, A_HEADS * BLK), f32),
            ]),
        out_shape=jax.ShapeDtypeStruct((bsz, tp, A_WIDTH), bf16),
        compiler_params=_cparams(("parallel", "arbitrary")),
        name="dsa",
    )(l, zcat3, zcat3, zs3, akr, ikr, zcat3, cos, sin, icos, isin, gaq)


def _fattn_kernel(l_ref, bq_ref, bka_ref, bv_ref, gq_ref, out_ref, qa_ref, m_ref, s_ref, acc_ref, st_ref):
    i = pl.program_id(1)
    fb = bq_ref.shape[0]
    ones3 = jnp.where(lax.broadcasted_iota(i32, (fb, LANES), 1) < 3, 1.0, 0.0).astype(bf16)
    for h in range(B_HEADS):
        sl = slice(h * LANES, (h + 1) * LANES)
        qa_ref[h, :, 0:LANES] = _rms_gain(bq_ref[:, sl].astype(f32), gq_ref[...]).astype(bf16)
        qa_ref[h, :, LANES:2 * LANES] = ones3
    m_ref[...] = jnp.full(m_ref.shape, NEG, f32)
    s_ref[...] = jnp.zeros(s_ref.shape, f32)
    acc_ref[...] = jnp.zeros(acc_ref.shape, f32)
    c2 = HEAD_DIM ** -0.5 * LOG2E
    causal = lax.broadcasted_iota(i32, (fb, fb), 0) <= lax.broadcasted_iota(i32, (fb, fb), 1)

    def step(kb, diagonal):
        k0 = pl.multiple_of(kb * fb, fb)
        m_news = []
        for h in range(B_HEADS):
            st = _dot_nt(bka_ref[h, pl.ds(k0, fb), :], qa_ref[h])
            if diagonal:
                st = jnp.where(causal, st, NEG)
            st_ref[h] = st
            m_news.append(jnp.maximum(m_ref[h], jnp.max(st, axis=0, keepdims=True)))
        for h in range(B_HEADS):
            sl = slice(h * LANES, (h + 1) * LANES)
            st = st_ref[h]
            m_old = m_ref[h]
            m_new = m_news[h]
            alpha = jnp.exp2((m_old - m_new) * c2)
            p = jnp.exp2((st - m_new) * c2)
            s_ref[h] = s_ref[h] * alpha + jnp.sum(p, axis=0, keepdims=True)
            m_ref[h] = m_new
            acc_ref[h] = acc_ref[h] * alpha + _dot_tn(bv_ref[pl.ds(k0, fb), sl], p.astype(bf16))

    def body(kb, carry):
        step(kb, False)
        return carry

    lax.fori_loop(0, i, body, 0)
    step(i, True)
    for h in range(B_HEADS):
        sl = slice(h * LANES, (h + 1) * LANES)
        out_ref[:, sl] = (acc_ref[h] / s_ref[h]).T.astype(bf16)


def _fattn(l, zcat3, bka, gbq):
    bsz, tp, _ = zcat3.shape
    fb = _seq_tile(tp, 384)
    once = pl.Buffered(1)
    return pl.pallas_call(
        _fattn_kernel,
        grid_spec=pltpu.PrefetchScalarGridSpec(
            num_scalar_prefetch=1, grid=(bsz, tp // fb),
            in_specs=[
                pl.BlockSpec((None, fb, B_WIDTH), lambda b, i, l: (b, i, T_BQ * LANES // B_WIDTH)),
                pl.BlockSpec((None, B_HEADS, tp, 2 * LANES), lambda b, i, l: (b, 0, 0, 0), pipeline_mode=once),
                pl.BlockSpec((None, tp, B_WIDTH), lambda b, i, l: (b, 0, T_BV * LANES // B_WIDTH),
                             pipeline_mode=once),
                pl.BlockSpec((None, 1, LANES), lambda b, i, l: (l[0], 0, 0)),
            ],
            out_specs=pl.BlockSpec((None, fb, B_WIDTH), lambda b, i, l: (b, i, 0)),
            scratch_shapes=[
                pltpu.VMEM((B_HEADS, fb, 2 * LANES), bf16),
                pltpu.VMEM((B_HEADS, 1, fb), f32),
                pltpu.VMEM((B_HEADS, 1, fb), f32),
                pltpu.VMEM((B_HEADS, HEAD_DIM, fb), f32),
                pltpu.VMEM((B_HEADS, fb, fb), f32),
            ]),
        out_shape=jax.ShapeDtypeStruct((bsz, tp, B_WIDTH), bf16),
        compiler_params=_cparams(("parallel", "arbitrary")),
        name="fattn",
    )(l, zcat3, bka, zcat3, gbq)


def _gla_kernel(l_ref, cq_ref, ck_ref, cv_ref, cg_ref, zs_ref, w2_ref, gb_ref, og_ref, e_ref,
                out_ref, st_ref, oi_ref, *, t_real):
    c = pl.program_id(0)
    nb, cc, dk = cq_ref.shape

    @pl.when(c == 0)
    def _():
        st_ref[...] = jnp.zeros_like(st_ref)

    rowc = lax.broadcasted_iota(i32, (cc, 1), 0)
    pos = c * cc + rowc
    valid = (pos >= FRONT) & (pos < FRONT + t_real)
    tri = (lax.broadcasted_iota(i32, (cc, cc), 0) >= lax.broadcasted_iota(i32, (cc, cc), 1)).astype(bf16)
    lane_k = lax.broadcasted_iota(i32, (1, dk), 1)
    head_masks = [(lane_k >= h * C_DK) & (lane_k < (h + 1) * C_DK) for h in range(C_HEADS)]

    for b in range(nb):
        x = _dot(zs_ref[b].astype(bf16), w2_ref[...]) + gb_ref[...]
        la = jnp.where(valid, _log_sigmoid(x) * (1.0 / C_TAU), 0.0)
        hi, mid, lo = _split3(la)
        bc = _dot(tri, hi) + _dot(tri, mid) + _dot(tri, lo)
        q = cq_ref[b].astype(f32) * (C_DK ** -0.5)
        k = ck_ref[b].astype(f32)
        v = cv_ref[b]
        vf = v.astype(f32)
        blast = bc[cc - 1:cc, :]
        qe = q * jnp.exp(bc)
        ke = k * jnp.exp(blast - bc)
        st = st_ref[b]
        stb = st.astype(bf16)

        o_inter = jnp.concatenate(
            [_dot_nt(jnp.where(head_masks[h], qe, 0.0).astype(bf16), stb) for h in range(C_HEADS)], axis=1)

        rows = []
        for r in range(cc):
            dec = jnp.exp(jnp.minimum(bc[r:r + 1, :] - bc, 0.0))
            rows.append(jnp.where(rowc <= r, q[r:r + 1, :] * k * dec, 0.0).astype(bf16))
        rr = _dot(jnp.concatenate(rows, axis=0), e_ref[...])
        for r in range(cc):
            oi_ref[r:r + 1, :] = jnp.sum(rr[r * cc:(r + 1) * cc, :] * vf, axis=0, keepdims=True)
        o = o_inter + oi_ref[...]

        new_st = st * jnp.exp(blast)
        for h in range(C_HEADS):
            km = jnp.where(head_masks[h], ke, 0.0).astype(bf16)
            new_st = new_st + _dot_tn(v[:, h * C_DV:(h + 1) * C_DV], km)
        st_ref[b] = new_st

        g = cg_ref[b].astype(f32)
        gs = g * (1.0 / (1.0 + jnp.exp(-g)))
        for h in range(C_HEADS):
            sl = slice(h * C_DV, (h + 1) * C_DV)
            out_ref[b, :, sl] = (_rms_gain(o[:, sl], og_ref[...]) * gs[:, sl]).astype(bf16)


def _gla(l, zcat3, zs3, w2p, gb, og, emat, t_real):
    bsz, tp, _ = zcat3.shape
    cc = GLA_C
    dk = C_HEADS * C_DK
    return pl.pallas_call(
        functools.partial(_gla_kernel, t_real=t_real),
        grid_spec=pltpu.PrefetchScalarGridSpec(
            num_scalar_prefetch=1, grid=(tp // cc,),
            in_specs=[
                pl.BlockSpec((bsz, cc, dk), lambda c, l: (0, c, T_CQ * LANES // dk)),
                pl.BlockSpec((bsz, cc, dk), lambda c, l: (0, c, T_CK * LANES // dk)),
                pl.BlockSpec((bsz, cc, C_WIDTH), lambda c, l: (0, c, T_CV * LANES // C_WIDTH)),
                pl.BlockSpec((bsz, cc, C_WIDTH), lambda c, l: (0, c, T_CG * LANES // C_WIDTH)),
                pl.BlockSpec((bsz, cc, LANES), lambda c, l: (0, c, 0)),
                pl.BlockSpec((None, LANES, dk), lambda c, l: (l[0], 0, 0)),
                pl.BlockSpec((None, 1, dk), lambda c, l: (l[0], 0, 0)),
                pl.BlockSpec((None, 1, C_DV), lambda c, l: (l[0], 0, 0)),
                pl.BlockSpec((dk, C_WIDTH), lambda c, l: (0, 0)),
            ],
            out_specs=pl.BlockSpec((bsz, cc, C_WIDTH), lambda c, l: (0, c, 0)),
            scratch_shapes=[pltpu.VMEM((bsz, C_DV, dk), f32), pltpu.VMEM((cc, C_WIDTH), f32)]),
        out_shape=jax.ShapeDtypeStruct((bsz, tp, C_WIDTH), bf16),
        compiler_params=_cparams(("arbitrary",)),
        name="gla",
    )(l, zcat3, zcat3, zcat3, zcat3, zs3, w2p, gb, og, emat)


def _outproj_kernel(l_ref, oa_ref, ob_ref, oc_ref, h_ref, wo_ref, g_ref, wr_ref, br_ref,
                    hmid_ref, u_ref, eid_ref, ew_ref, *, t_real):
    tm = h_ref.shape[0]
    mix = (_dot(oa_ref[...], wo_ref[0:A_WIDTH, :])
           + _dot(ob_ref[...], wo_ref[A_WIDTH:A_WIDTH + B_WIDTH, :])
           + _dot(oc_ref[...], wo_ref[A_WIDTH + B_WIDTH:, :]))
    pos = pl.program_id(1) * tm + lax.broadcasted_iota(i32, (tm, 1), 0)
    valid = (pos >= FRONT) & (pos < FRONT + t_real)
    hm = h_ref[...] + jnp.where(valid, mix, 0.0)
    hmid_ref[...] = hm
    u = _rms_gain(hm, g_ref[...]).astype(bf16)
    u_ref[...] = u

    logits = _dot(u, wr_ref[...]) + br_ref[...]
    lane = lax.broadcasted_iota(i32, (tm, LANES), 1)
    lanef = lane.astype(f32)
    big = float(4 * LANES)
    first = lambda hit: jnp.min(jnp.where(hit, lanef, big), axis=-1, keepdims=True).astype(i32)
    gl = jnp.where(lane < R_GRP + N_GROUPS, logits, -jnp.inf)
    gmax = jnp.max(gl, axis=-1, keepdims=True)
    g_p = 1.0 / jnp.sum(jnp.exp(gl - gmax), axis=-1, keepdims=True)
    g_i = first(gl == gmax)
    e_lane = lane - R_EXP
    emask = (e_lane >= 0) & (e_lane < N_EXPERTS) & ((e_lane >> 3) == g_i)
    el = jnp.where(emask, logits, -jnp.inf)
    m1 = jnp.max(el, axis=-1, keepdims=True)
    i1 = first(el == m1)
    el2 = jnp.where(lane == i1, -jnp.inf, el)
    m2 = jnp.max(el2, axis=-1, keepdims=True)
    i2 = first(el2 == m2)
    r = jnp.exp(m2 - m1)
    w1 = g_p / (1.0 + r)
    w2 = g_p * r / (1.0 + r)
    eid_ref[...] = jnp.where(lane == 0, i1 - R_EXP, jnp.where(lane == 1, i2 - R_EXP, 0))
    ew_ref[...] = jnp.where(lane == 0, w1, jnp.where(lane == 1, w2, 0.0))


def _seq_tile(tp, cap):
    return max(t for t in range(BLK, cap + 1, BLK) if tp % t == 0)


def _outproj(l, oa, ob, oc, h, wo, gain, wr, br, t_real, tp):
    n, d = h.shape
    tm = _seq_tile(tp, 384)
    nj = tp // tm
    row = lambda w: pl.BlockSpec((tm, w), lambda b, j, l: (b * nj + j, 0))
    return pl.pallas_call(
        functools.partial(_outproj_kernel, t_real=t_real),
        grid_spec=pltpu.PrefetchScalarGridSpec(
            num_scalar_prefetch=1, grid=(n // tp, nj),
            in_specs=[row(A_WIDTH), row(B_WIDTH), row(C_WIDTH), row(d),
                      pl.BlockSpec((None, d, d), lambda b, j, l: (l[0], 0, 0)),
                      pl.BlockSpec((None, 1, d), lambda b, j, l: (l[0], 0, 0)),
                      pl.BlockSpec((None, d, LANES), lambda b, j, l: (l[0], 0, 0)),
                      pl.BlockSpec((None, 1, LANES), lambda b, j, l: (l[0], 0, 0))],
            out_specs=[row(d), row(d), row(LANES), row(LANES)]),
        out_shape=[jax.ShapeDtypeStruct((n, d), f32), jax.ShapeDtypeStruct((n, d), bf16),
                   jax.ShapeDtypeStruct((n, LANES), i32), jax.ShapeDtypeStruct((n, LANES), f32)],
        compiler_params=_cparams(("parallel", "parallel")),
        name="outproj",
    )(l, oa, ob, oc, h, wo, gain, wr, br)


def _moe_kernel(l_ref, te_ref, na_ref, xs_ref, w1_ref, w3_ref, w2_ref, rw_ref, ys_ref,
                w1b_ref, w3b_ref, w2b_ref):
    i = pl.program_id(0)

    @pl.when((i == 0) | (te_ref[i] != te_ref[jnp.maximum(i - 1, 0)]))
    def _():
        w1b_ref[...] = w1_ref[...].astype(bf16)
        w3b_ref[...] = w3_ref[...].astype(bf16)
        w2b_ref[...] = w2_ref[...].astype(bf16)

    @pl.when(i < na_ref[0])
    def _():
        x = xs_ref[...]
        h1 = _dot(x, w1b_ref[...])
        h3 = _dot(x, w3b_ref[...])
        hid = (h1 * (1.0 / (1.0 + jnp.exp(-h1))) * h3).astype(bf16)
        ys_ref[...] = (_dot(hid, w2b_ref[...]) * rw_ref[...]).astype(bf16)

    @pl.when(i >= na_ref[0])
    def _():
        ys_ref[...] = jnp.zeros_like(ys_ref)


def _moe(l, te, na, xs, w1, w3, w2, rw):
    p, d = xs.shape
    tm = MOE_TM
    return pl.pallas_call(
        _moe_kernel,
        grid_spec=pltpu.PrefetchScalarGridSpec(
            num_scalar_prefetch=3, grid=(p // tm,),
            in_specs=[
                pl.BlockSpec((tm, d), lambda i, l, te, na: (i, 0)),
                pl.BlockSpec((None, None, d, D_EXPERT), lambda i, l, te, na: (l[0], te[i], 0, 0)),
                pl.BlockSpec((None, None, d, D_EXPERT), lambda i, l, te, na: (l[0], te[i], 0, 0)),
                pl.BlockSpec((None, None, D_EXPERT, d), lambda i, l, te, na: (l[0], te[i], 0, 0)),
                pl.BlockSpec((tm, 1), lambda i, l, te, na: (i, 0)),
            ],
            out_specs=pl.BlockSpec((tm, d), lambda i, l, te, na: (i, 0)),
            scratch_shapes=[pltpu.VMEM((d, D_EXPERT), bf16), pltpu.VMEM((d, D_EXPERT), bf16),
                            pltpu.VMEM((D_EXPERT, d), bf16)]),
        out_shape=jax.ShapeDtypeStruct((p, d), bf16),
        compiler_params=_cparams(("arbitrary",)),
        name="moe",
    )(l, te, na, xs, w1, w3, w2, rw)


def _route(eid, ew, n_tiles):
    tm = MOE_TM
    p = n_tiles * tm
    n = eid.shape[0]
    e_flat = eid[:, :EXP_TOPK_CONST].T.reshape(-1)
    w_flat = ew[:, :EXP_TOPK_CONST].T.reshape(-1)
    na_all = e_flat.shape[0]
    hp = lax.Precision.HIGHEST
    ex = jnp.arange(N_EXPERTS, dtype=i32)[:, None]
    onehot = (ex == e_flat[None, :]).astype(f32)
    counts = jnp.sum(onehot, axis=1).astype(i32)
    pc = ((counts + tm - 1) // tm) * tm
    pend = jnp.cumsum(pc)
    po = pend - pc
    co = jnp.cumsum(counts) - counts
    order = jnp.argsort(e_flat, stable=True).astype(i32)
    r = jnp.arange(p, dtype=i32)
    step = (r[None, :] >= pend[:, None]).astype(f32)
    dlt = lambda v: jnp.concatenate([v[1:] - v[:-1], jnp.zeros((1,), v.dtype)]).astype(f32)
    tabs = jnp.stack([jnp.ones((N_EXPERTS,), f32), dlt(po), dlt(counts), dlt(co)])
    picked = jnp.dot(tabs, step, precision=hp).astype(i32)
    e_r = jnp.minimum(picked[0], N_EXPERTS - 1)
    local = r - (po[0] + picked[1])
    valid_r = (local < counts[0] + picked[2]) & (r < pend[-1])
    a_r = order[jnp.clip(co[0] + picked[3] + local, 0, na_all - 1)]
    tok_r = jnp.where(valid_r, jnp.where(a_r >= n, a_r - n, a_r), r % n)
    w_r = jnp.where(valid_r, w_flat[a_r], 0.0)
    n_act = (pend[-1] // tm).astype(i32)
    tile_e = e_r[::tm]
    te = jnp.where(jnp.arange(n_tiles, dtype=i32) < n_act, tile_e, tile_e[jnp.maximum(n_act - 1, 0)])
    inv = jnp.argsort(order).astype(i32)
    pos_a = jnp.dot((po - co).astype(f32)[None, :], onehot, precision=hp)[0].astype(i32) + inv
    return tok_r, w_r[:, None], te, n_act.reshape(1), pos_a


def _final_kernel(hmid_ref, ya_ref, yb_ref, out_ref):
    out_ref[...] = hmid_ref[...] + ya_ref[...].astype(f32) + yb_ref[...].astype(f32)


def _final(hmid, y2):
    n, d = hmid.shape
    tm = _row_tile(n, 512)
    nt = n // tm
    return pl.pallas_call(
        _final_kernel, grid=(nt,),
        in_specs=[pl.BlockSpec((tm, d), lambda i: (i, 0)), pl.BlockSpec((tm, d), lambda i: (i, 0)),
                  pl.BlockSpec((tm, d), lambda i: (i + nt, 0))],
        out_specs=pl.BlockSpec((tm, d), lambda i: (i, 0)),
        out_shape=jax.ShapeDtypeStruct((n, d), f32),
        compiler_params=_cparams(("parallel",)),
        name="final",
    )(hmid, y2, y2)


def _rope_tables(tp):
    pos = (jnp.arange(tp, dtype=f32) - FRONT)[:, None]

    def tab(dim, reps):
        inv = 1.0 / (ROPE_THETA ** (jnp.arange(0, dim, 2, dtype=f32) / dim))
        ang = pos * inv[None, :]
        return jnp.tile(jnp.cos(ang), (1, reps)), jnp.tile(jnp.sin(ang), (1, reps))

    cos, sin = tab(HEAD_DIM, 2)
    icos, isin = tab(IDX_DIM, 4)
    return cos, sin, icos, isin


def _split_w_in(w_in):
    parts, off = [], 0
    for s in IN_SPLITS:
        parts.append(w_in[..., off:off + s])
        off += s
    return parts


def kernel(x, meta, norm_mix_g, norm_ffn_g, w_in, a_qn_g, a_kn_g, b_qn_g, b_kn_g, b_f_bias,
           c_gate_w2, c_gate_b, c_on_g, w_out, r_group_w, r_group_b, r_exp_w, r_exp_b,
           e_w1, e_w3, e_w2):
    bsz, n_seq, d = x.shape
    depth = w_in.shape[0]
    t_real = n_seq + N_META
    k_top = min(TOPK_MAX, n_seq // 4)
    tp = -(-(FRONT + t_real) // BLK) * BLK
    n = bsz * tp

    (waq, wak, wav, wiq, wik, wiw, wbq, wbk, wbv, wbf, wcq, wck, wcv, wcr, wcg) = _split_w_in(w_in)
    zc = lambda w: jnp.zeros((depth, d, w), w_in.dtype)
    wcat = jnp.concatenate([waq, wak, wav, wiq, wik, zc(LANES - IDX_DIM), zc(LANES),
                            wbq, wbk, wbv, wcq, wck, wcv, wcg], axis=-1).astype(bf16)
    ws = jnp.concatenate([wiw, wbf, wcr, zc(LANES - S_CR - C_GATE_RANK)], axis=-1).astype(bf16)
    wo = w_out.astype(bf16)
    wr = jnp.concatenate([r_group_w, zc(R_EXP - N_GROUPS), r_exp_w, zc(LANES - R_EXP - N_EXPERTS)],
                         axis=-1).astype(bf16)
    zl = lambda w: jnp.zeros((depth, w), f32)
    br = jnp.concatenate([r_group_b, zl(R_EXP - N_GROUPS), r_exp_b, zl(LANES - R_EXP - N_EXPERTS)],
                         axis=-1)[:, None, :]
    fbias = jnp.concatenate([zl(S_BF), b_f_bias, zl(LANES - S_BF - B_HEADS)], axis=-1)[:, None, :]
    dk = C_HEADS * C_DK
    w2p = jnp.concatenate([jnp.zeros((depth, S_CR, dk), f32), c_gate_w2,
                           jnp.zeros((depth, LANES - S_CR - C_GATE_RANK, dk), f32)], axis=1).astype(bf16)
    gb = c_gate_b[:, None, :]
    og = c_on_g[:, None, :]
    g_mix = norm_mix_g[:, None, :]
    g_ffn = norm_ffn_g[:, None, :]
    gaq, gak, gbq, gbk = (g[:, None, :] for g in (a_qn_g, a_kn_g, b_qn_g, b_kn_g))
    emat = (jnp.arange(dk, dtype=i32)[:, None] // C_DK
            == jnp.arange(C_WIDTH, dtype=i32)[None, :] // C_DV).astype(bf16)
    tabs = _rope_tables(tp)

    h0 = jnp.concatenate([
        jnp.zeros((bsz, FRONT, d), f32),
        jnp.broadcast_to(meta[None].astype(f32), (bsz, N_META, d)),
        x.astype(f32),
        jnp.zeros((bsz, tp - FRONT - t_real, d), f32)], axis=1).reshape(n, d)
    n_tiles = -(-(EXP_TOPK_CONST * n + N_EXPERTS * (MOE_TM - 1)) // MOE_TM)

    def layer(li, carry):
        hmid, y2 = carry
        l = jnp.reshape(jnp.asarray(li, i32), (1,))
        h, zcat, zs = _inproj(l, hmid, y2, g_mix, wcat, ws)
        zcat3 = zcat.reshape(bsz, tp, N_CAT)
        zs3 = zs.reshape(bsz, tp, LANES)
        akr, ikr, bka = _kprep(l, zcat3, zs3, tabs, gak, gbk, fbias, t_real)
        oa = _dsa(l, zcat3, zs3, akr, ikr, tabs, gaq, t_real, k_top)
        ob = _fattn(l, zcat3, bka, gbq)
        oc = _gla(l, zcat3, zs3, w2p, gb, og, emat, t_real)
        hmid2, u, eid, ew = _outproj(l, oa.reshape(n, A_WIDTH), ob.reshape(n, B_WIDTH),
                                     oc.reshape(n, C_WIDTH), h, wo, g_ffn, wr, br, t_real, tp)
        tok_r, w_r, te, n_act, pos_a = _route(eid, ew, n_tiles)
        ys = _moe(l, te, n_act, jnp.take(u, tok_r, axis=0, mode="clip"), e_w1, e_w3, e_w2, w_r)
        return hmid2, jnp.take(ys, pos_a, axis=0, mode="clip")

    y0 = lax.optimization_barrier(jnp.zeros((EXP_TOPK_CONST * n, d), bf16))
    hmid, y2 = lax.fori_loop(0, depth, layer, (h0, y0))
    out = _final(hmid, y2).reshape(bsz, tp, d)
    return out[:, FRONT + N_META:FRONT + t_real].astype(x.dtype)
```

```python
import functools

import jax
import jax.numpy as jnp
from jax import lax
from jax.experimental import pallas as pl
from jax.experimental.pallas import tpu as pltpu

f32 = jnp.float32
bf16 = jnp.bfloat16
i32 = jnp.int32

D_MODEL = 2048
CHUNK = 64
N_META = 16
ROPE_THETA = 10000.0
EPS = 1e-6
HEAD_DIM = 128
A_HEADS = 6
IDX_HEADS = 16
IDX_DIM = 64
TOPK_MAX = 256
B_HEADS = 6
C_HEADS = 4
C_DK = 64
C_DV = 128
C_GATE_RANK = 16
C_TAU = 16.0
N_GROUPS = 4
EXP_PER_GROUP = 8
N_EXPERTS = N_GROUPS * EXP_PER_GROUP
D_EXPERT = 512
A_WIDTH = A_HEADS * HEAD_DIM
B_WIDTH = B_HEADS * HEAD_DIM
C_WIDTH = C_HEADS * C_DV
IN_SPLITS = (A_WIDTH, HEAD_DIM, HEAD_DIM, IDX_HEADS * IDX_DIM, IDX_DIM, IDX_HEADS,
             B_WIDTH, B_WIDTH, B_WIDTH, B_HEADS,
             C_HEADS * C_DK, C_HEADS * C_DK, C_WIDTH, C_GATE_RANK, C_WIDTH)

LANES = 128
SUBLANES = 8
FRONT = (-N_META) % CHUNK
BLK = 128
T_AQ, T_AK, T_AV, T_IQ, T_IK = 0, 6, 7, 8, 16
T_BQ, T_BK, T_BV = 18, 24, 30
T_CQ, T_CK, T_CV, T_CG = 36, 38, 40, 44
N_CAT = 48 * LANES
S_IW, S_BF, S_CR = 0, 16, 22
R_GRP, R_EXP = 0, 32
INPROJ_TN = 1536
GLA_C = 32
DSA_G = 3
MOE_TM = 256
EXP_TOPK_CONST = 2
NEG = -1e30
LOG2E = 1.4426950408889634
INT_MIN = -2 ** 31
VMEM_LIMIT = 56 * 1024 * 1024


def _cparams(sem):
    return pltpu.CompilerParams(dimension_semantics=sem, vmem_limit_bytes=VMEM_LIMIT)


def _row_tile(n, cap):
    t = cap
    while n % t:
        t //= 2
    return t


def _log_sigmoid(x):
    return jnp.minimum(x, 0.0) - jnp.log(1.0 + jnp.exp(-jnp.abs(x)))


def _dot(a, b):
    return jnp.dot(a, b, preferred_element_type=f32)


def _dot_nt(a, b):
    return lax.dot_general(a, b, (((1,), (1,)), ((), ())), preferred_element_type=f32)


def _dot_tn(a, b):
    return lax.dot_general(a, b, (((0,), (0,)), ((), ())), preferred_element_type=f32)


def _split3(x):
    hi = x.astype(bf16)
    r1 = x - hi.astype(f32)
    mid = r1.astype(bf16)
    lo = (r1 - mid.astype(f32)).astype(bf16)
    return hi, mid, lo


def _rope128(x, cos, sin, lane):
    return x * cos + pltpu.roll(x, 64, 1) * jnp.where(lane < 64, -sin, sin)


def _rope64(x, cos, sin, lane):
    low = (lane & 63) < 32
    return (x * cos + pltpu.roll(x, 32, 1) * jnp.where(low, 0.0, sin)
            + pltpu.roll(x, 96, 1) * jnp.where(low, -sin, 0.0))


def _rms_gain(x, g):
    return x * lax.rsqrt(jnp.mean(x * x, axis=-1, keepdims=True) + EPS) * g


def _prenorm_kernel(l_ref, hmid_ref, ya_ref, yb_ref, g_ref, ws_ref, h_ref, xn_ref, zs_ref):
    h = hmid_ref[...] + ya_ref[...].astype(f32) + yb_ref[...].astype(f32)
    h_ref[...] = h
    xn = _rms_gain(h, g_ref[...]).astype(bf16)
    xn_ref[...] = xn
    zs_ref[...] = _dot(xn, ws_ref[...])


def _inproj_kernel(l_ref, xn_ref, w_ref, z_ref):
    z_ref[...] = _dot(xn_ref[...], w_ref[...]).astype(bf16)


def _inproj(l, hmid, y2, gain, wcat, ws):
    n, d = hmid.shape
    tm = _row_tile(n, 512)
    nt = n // tm
    row = lambda w: pl.BlockSpec((tm, w), lambda i, l: (i, 0))
    h, xn, zs = pl.pallas_call(
        _prenorm_kernel,
        grid_spec=pltpu.PrefetchScalarGridSpec(
            num_scalar_prefetch=1, grid=(nt,),
            in_specs=[row(d), row(d),
                      pl.BlockSpec((tm, d), lambda i, l: (i + nt, 0)),
                      pl.BlockSpec((None, 1, d), lambda i, l: (l[0], 0, 0)),
                      pl.BlockSpec((None, d, LANES), lambda i, l: (l[0], 0, 0))],
            out_specs=[row(d), row(d), row(LANES)]),
        out_shape=[jax.ShapeDtypeStruct((n, d), f32), jax.ShapeDtypeStruct((n, d), bf16),
                   jax.ShapeDtypeStruct((n, LANES), f32)],
        compiler_params=_cparams(("parallel",)),
        name="prenorm",
    )(l, hmid, y2, y2, gain, ws)
    tn = INPROJ_TN
    zcat = pl.pallas_call(
        _inproj_kernel,
        grid_spec=pltpu.PrefetchScalarGridSpec(
            num_scalar_prefetch=1, grid=(N_CAT // tn, nt),
            in_specs=[pl.BlockSpec((tm, d), lambda j, i, l: (i, 0)),
                      pl.BlockSpec((None, d, tn), lambda j, i, l: (l[0], 0, j))],
            out_specs=pl.BlockSpec((tm, tn), lambda j, i, l: (i, j))),
        out_shape=jax.ShapeDtypeStruct((n, N_CAT), bf16),
        compiler_params=_cparams(("parallel", "parallel")),
        name="inproj",
    )(l, xn, wcat)
    return h, zcat, zs


def _kprep_kernel(l_ref, ak_ref, ik_ref, bk_ref, zs_ref,
                  cos_ref, sin_ref, icos_ref, isin_ref, gak_ref, gbk_ref, fb_ref,
                  akr_ref, ikr_ref, bka_ref, carry_ref, *, t_real):
    k = pl.program_id(1)
    lane = lax.broadcasted_iota(i32, (BLK, LANES), 1)
    row = lax.broadcasted_iota(i32, (BLK, LANES), 0)
    pos = k * BLK + row
    valid = (pos >= FRONT) & (pos < FRONT + t_real)

    x = ak_ref[...].astype(f32)
    akr_ref[...] = _rope128(_rms_gain(x, gak_ref[...]), cos_ref[...], sin_ref[...], lane).astype(bf16)
    ikr_ref[...] = _rope64(ik_ref[...].astype(f32), icos_ref[...], isin_ref[...], lane).astype(bf16)

    @pl.when(k == 0)
    def _():
        carry_ref[...] = jnp.zeros_like(carry_ref)

    lf = jnp.where(valid, _log_sigmoid(zs_ref[...] + fb_ref[...]), 0.0)
    tri = (row >= lane).astype(bf16)
    hi, mid, lo = _split3(lf)
    fcum = _dot(tri, hi) + _dot(tri, mid) + _dot(tri, lo) + carry_ref[...]
    carry_ref[...] = fcum[BLK - 1:BLK, :]

    fs = jnp.where(valid, fcum * (-(HEAD_DIM ** 0.5)), NEG)
    for h in range(B_HEADS):
        sl = slice(h * LANES, (h + 1) * LANES)
        bka_ref[h, :, 0:LANES] = _rms_gain(bk_ref[:, sl].astype(f32), gbk_ref[...]).astype(bf16)
        p0, p1, p2 = _split3(fs[:, S_BF + h:S_BF + h + 1])
        aug = jnp.where(lane == 0, p0.astype(f32),
                        jnp.where(lane == 1, p1.astype(f32), jnp.where(lane == 2, p2.astype(f32), 0.0)))
        bka_ref[h, :, LANES:2 * LANES] = aug.astype(bf16)


def _kprep(l, zcat3, zs3, tabs, gak, gbk, fbias, t_real):
    bsz, tp, _ = zcat3.shape
    nkb = tp // BLK
    cos, sin, icos, isin = tabs
    tile = lambda c: pl.BlockSpec((None, BLK, LANES), lambda b, k, l, c=c: (b, k, c))
    wide = lambda c: pl.BlockSpec((None, BLK, B_WIDTH), lambda b, k, l, c=c: (b, k, c))
    tab = pl.BlockSpec((BLK, LANES), lambda b, k, l: (k, 0))
    gain = pl.BlockSpec((None, 1, LANES), lambda b, k, l: (l[0], 0, 0))
    return pl.pallas_call(
        functools.partial(_kprep_kernel, t_real=t_real),
        grid_spec=pltpu.PrefetchScalarGridSpec(
            num_scalar_prefetch=1, grid=(bsz, nkb),
            in_specs=[tile(T_AK), tile(T_IK),
                      wide(T_BK * LANES // B_WIDTH),
                      pl.BlockSpec((None, BLK, LANES), lambda b, k, l: (b, k, 0)),
                      tab, tab, tab, tab, gain, gain, gain],
            out_specs=[
                pl.BlockSpec((None, BLK, LANES), lambda b, k, l: (b, k, 0)),
                pl.BlockSpec((None, BLK, LANES), lambda b, k, l: (b, k, 0)),
                pl.BlockSpec((None, B_HEADS, BLK, 2 * LANES), lambda b, k, l: (b, 0, k, 0)),
            ],
            scratch_shapes=[pltpu.VMEM((1, LANES), f32)]),
        out_shape=[jax.ShapeDtypeStruct((bsz, tp, LANES), bf16),
                   jax.ShapeDtypeStruct((bsz, tp, LANES), bf16),
                   jax.ShapeDtypeStruct((bsz, B_HEADS, tp, 2 * LANES), bf16)],
        compiler_params=_cparams(("parallel", "arbitrary")),
        name="kprep",
    )(l, zcat3, zcat3, zcat3, zs3, cos, sin, icos, isin, gak, gbk, fbias)


def _dsa_kernel(l_ref, aq_ref, iq_ref, zs_ref, akr_ref, ikr_ref, av_ref,
                cos_ref, sin_ref, icos_ref, isin_ref, gq_ref, out_ref,
                key_ref, iqs_ref, q6_ref, iwt_ref, m_ref, s_ref, acc_ref, stq_ref, *, t_real, k_top):
    i = pl.program_id(1)
    nk = i + 1
    lane = lax.broadcasted_iota(i32, (BLK, LANES), 1)

    for h in range(A_HEADS):
        x = aq_ref[:, h * LANES:(h + 1) * LANES].astype(f32)
        xr = _rope128(_rms_gain(x, gq_ref[...]), cos_ref[...], sin_ref[...], lane)
        q6_ref[h * BLK:(h + 1) * BLK, :] = xr.astype(bf16)
    for t in range(IDX_HEADS // 2):
        x = iq_ref[:, t * LANES:(t + 1) * LANES].astype(f32)
        xr = _rope64(x, icos_ref[...], isin_ref[...], lane)
        iqs_ref[(2 * t) * BLK:(2 * t + 1) * BLK, :] = xr.astype(bf16)
        iqs_ref[(2 * t + 1) * BLK:(2 * t + 2) * BLK, :] = pltpu.roll(xr, 64, 1).astype(bf16)
    iwt_ref[...] = (zs_ref[...] * (IDX_HEADS ** -0.5 * IDX_DIM ** -0.5)).T

    gb = DSA_G * BLK
    ng = lax.div(nk, jnp.int32(DSA_G))
    nr = nk - ng * DSA_G

    def over_keys(fn, init):
        c = lax.fori_loop(0, ng, lambda g, c: fn(pl.multiple_of(g * gb, gb), gb, c), init)
        return lax.fori_loop(0, nr, lambda r, c: fn(pl.multiple_of((ng * DSA_G + r) * BLK, BLK), BLK, c), c)

    def score_rows(k0, nrows, carry):
        dt = _dot_nt(ikr_ref[pl.ds(k0, nrows), :], iqs_ref[...])
        s = jnp.zeros((nrows, LANES), f32)
        for h in range(IDX_HEADS):
            s = s + iwt_ref[h:h + 1, :] * jnp.maximum(dt[:, h * LANES:(h + 1) * LANES], 0.0)
        kpos = k0 + lax.broadcasted_iota(i32, (nrows, LANES), 0)
        qpos = i * BLK + lax.broadcasted_iota(i32, (nrows, LANES), 1)
        adm = ((kpos >> 6) <= (qpos >> 6)) & (kpos >= FRONT) & (kpos < FRONT + t_real)
        bits = lax.bitcast_convert_type(s, i32)
        key = bits ^ ((bits >> 31) & 0x7FFFFFFF)
        key_ref[pl.ds(k0, nrows), :] = jnp.where(adm, key, INT_MIN)
        return carry

    over_keys(score_rows, 0)

    def bit_body(t, thr_u):
        bit = jnp.left_shift(jnp.int32(1), 31 - t)
        cand_u = thr_u | bit
        cand_s = cand_u ^ INT_MIN

        def count_rows(k0, nrows, c):
            hit = (key_ref[pl.ds(k0, nrows), :] >= cand_s).astype(i32)
            for j in range(nrows // BLK):
                c = c + hit[j * BLK:(j + 1) * BLK, :]
            return c

        cnt = over_keys(count_rows, jnp.zeros((BLK, LANES), i32))
        tot = jnp.sum(cnt.astype(f32), axis=0, keepdims=True)
        return jnp.where(tot >= k_top, cand_u, thr_u)

    thr_u = lax.fori_loop(0, 32, bit_body, jnp.zeros((1, LANES), i32))
    thr_s = jnp.maximum(thr_u ^ INT_MIN, INT_MIN + 1)

    m_ref[...] = jnp.full(m_ref.shape, NEG, f32)
    s_ref[...] = jnp.zeros(s_ref.shape, f32)
    acc_ref[...] = jnp.zeros(acc_ref.shape, f32)
    c2 = HEAD_DIM ** -0.5 * LOG2E

    def attn_rows(k0, nrows, carry, st=None):
        if st is None:
            st = _dot_nt(akr_ref[pl.ds(k0, nrows), :], q6_ref[...])
        bias = jnp.where(key_ref[pl.ds(k0, nrows), :] >= thr_s, 0.0, NEG)
        m_news = []
        for h in range(A_HEADS):
            sl = slice(h * LANES, (h + 1) * LANES)
            sh = st[:, sl] + bias
            stq_ref[0, 0:nrows, sl] = sh
            m_news.append(jnp.maximum(m_ref[:, sl], jnp.max(sh, axis=0, keepdims=True)))
        ps, alphas = [], []
        for h in range(A_HEADS):
            sl = slice(h * LANES, (h + 1) * LANES)
            sh = stq_ref[0, 0:nrows, sl]
            m_old = m_ref[:, sl]
            m_new = m_news[h]
            alpha = jnp.exp2((m_old - m_new) * c2)
            p = jnp.exp2((sh - m_new) * c2)
            s_ref[:, sl] = s_ref[:, sl] * alpha + jnp.sum(p, axis=0, keepdims=True)
            m_ref[:, sl] = m_new
            alphas.append(alpha)
            ps.append(p.astype(bf16))
        pv = _dot_tn(av_ref[pl.ds(k0, nrows), :], jnp.concatenate(ps, axis=1))
        acc_ref[...] = acc_ref[...] * jnp.concatenate(alphas, axis=1) + pv
        return carry

    over_keys(attn_rows, 0)

    for h in range(A_HEADS):
        sl = slice(h * LANES, (h + 1) * LANES)
        o = acc_ref[:, sl] / jnp.maximum(s_ref[:, sl], 1e-30)
        out_ref[:, sl] = o.T.astype(bf16)


def _dsa(l, zcat3, zs3, akr, ikr, tabs, gaq, t_real, k_top):
    bsz, tp, _ = zcat3.shape
    nkb = tp // BLK
    cos, sin, icos, isin = tabs
    tab = pl.BlockSpec((BLK, LANES), lambda b, i, l: (i, 0))
    full = pl.BlockSpec((None, tp, LANES), lambda b, i, l: (b, 0, 0))
    return pl.pallas_call(
        functools.partial(_dsa_kernel, t_real=t_real, k_top=k_top),
        grid_spec=pltpu.PrefetchScalarGridSpec(
            num_scalar_prefetch=1, grid=(bsz, nkb),
            in_specs=[
                pl.BlockSpec((None, BLK, A_WIDTH), lambda b, i, l: (b, i, 0)),
                pl.BlockSpec((None, BLK, IDX_HEADS * IDX_DIM), lambda b, i, l: (b, i, T_IQ * LANES // (IDX_HEADS * IDX_DIM))),
                pl.BlockSpec((None, BLK, LANES), lambda b, i, l: (b, i, 0)),
                full, full,
                pl.BlockSpec((None, tp, LANES), lambda b, i, l: (b, 0, T_AV)),
                tab, tab, tab, tab,
                pl.BlockSpec((None, 1, LANES), lambda b, i, l: (l[0], 0, 0)),
            ],
            out_specs=pl.BlockSpec((None, BLK, A_WIDTH), lambda b, i, l: (b, i, 0)),
            scratch_shapes=[
                pltpu.VMEM((tp, LANES), i32),
                pltpu.VMEM((IDX_HEADS * BLK, LANES), bf16),
                pltpu.VMEM((A_HEADS * BLK, LANES), bf16),
                pltpu.VMEM((LANES, BLK), f32),
                pltpu.VMEM((1, A_HEADS * BLK), f32),
                pltpu.VMEM((1, A_HEADS * BLK), f32),
                pltpu.VMEM((HEAD_DIM, A_HEADS * BLK), f32),
                pltpu.VMEM((2, DSA_G * BLK, A_HEADS * BLK), f32),
            ]),
        out_shape=jax.ShapeDtypeStruct((bsz, tp, A_WIDTH), bf16),
        compiler_params=_cparams(("parallel", "arbitrary")),
        name="dsa",
    )(l, zcat3, zcat3, zs3, akr, ikr, zcat3, cos, sin, icos, isin, gaq)


def _fattn_kernel(l_ref, bq_ref, bka_ref, bv_ref, gq_ref, out_ref, qa_ref, m_ref, s_ref, acc_ref, st_ref):
    i = pl.program_id(1)
    fb = bq_ref.shape[0]
    ones3 = jnp.where(lax.broadcasted_iota(i32, (fb, LANES), 1) < 3, 1.0, 0.0).astype(bf16)
    for h in range(B_HEADS):
        sl = slice(h * LANES, (h + 1) * LANES)
        qa_ref[h, :, 0:LANES] = _rms_gain(bq_ref[:, sl].astype(f32), gq_ref[...]).astype(bf16)
        qa_ref[h, :, LANES:2 * LANES] = ones3
    m_ref[...] = jnp.full(m_ref.shape, NEG, f32)
    s_ref[...] = jnp.zeros(s_ref.shape, f32)
    acc_ref[...] = jnp.zeros(acc_ref.shape, f32)
    c2 = HEAD_DIM ** -0.5 * LOG2E
    causal = lax.broadcasted_iota(i32, (fb, fb), 0) <= lax.broadcasted_iota(i32, (fb, fb), 1)

    def step(kb, diagonal):
        k0 = pl.multiple_of(kb * fb, fb)
        m_news = []
        for h in range(B_HEADS):
            st = _dot_nt(bka_ref[h, pl.ds(k0, fb), :], qa_ref[h])
            if diagonal:
                st = jnp.where(causal, st, NEG)
            st_ref[h] = st
            m_news.append(jnp.maximum(m_ref[h], jnp.max(st, axis=0, keepdims=True)))
        for h in range(B_HEADS):
            sl = slice(h * LANES, (h + 1) * LANES)
            st = st_ref[h]
            m_old = m_ref[h]
            m_new = m_news[h]
            alpha = jnp.exp2((m_old - m_new) * c2)
            p = jnp.exp2((st - m_new) * c2)
            s_ref[h] = s_ref[h] * alpha + jnp.sum(p, axis=0, keepdims=True)
            m_ref[h] = m_new
            acc_ref[h] = acc_ref[h] * alpha + _dot_tn(bv_ref[pl.ds(k0, fb), sl], p.astype(bf16))

    def body(kb, carry):
        step(kb, False)
        return carry

    lax.fori_loop(0, i, body, 0)
    step(i, True)
    for h in range(B_HEADS):
        sl = slice(h * LANES, (h + 1) * LANES)
        out_ref[:, sl] = (acc_ref[h] / s_ref[h]).T.astype(bf16)


def _fattn(l, zcat3, bka, gbq):
    bsz, tp, _ = zcat3.shape
    fb = _seq_tile(tp, 384)
    once = pl.Buffered(1)
    return pl.pallas_call(
        _fattn_kernel,
        grid_spec=pltpu.PrefetchScalarGridSpec(
            num_scalar_prefetch=1, grid=(bsz, tp // fb),
            in_specs=[
                pl.BlockSpec((None, fb, B_WIDTH), lambda b, i, l: (b, i, T_BQ * LANES // B_WIDTH)),
                pl.BlockSpec((None, B_HEADS, tp, 2 * LANES), lambda b, i, l: (b, 0, 0, 0), pipeline_mode=once),
                pl.BlockSpec((None, tp, B_WIDTH), lambda b, i, l: (b, 0, T_BV * LANES // B_WIDTH),
                             pipeline_mode=once),
                pl.BlockSpec((None, 1, LANES), lambda b, i, l: (l[0], 0, 0)),
            ],
            out_specs=pl.BlockSpec((None, fb, B_WIDTH), lambda b, i, l: (b, i, 0)),
            scratch_shapes=[
                pltpu.VMEM((B_HEADS, fb, 2 * LANES), bf16),
                pltpu.VMEM((B_HEADS, 1, fb), f32),
                pltpu.VMEM((B_HEADS, 1, fb), f32),
                pltpu.VMEM((B_HEADS, HEAD_DIM, fb), f32),
                pltpu.VMEM((B_HEADS, fb, fb), f32),
            ]),
        out_shape=jax.ShapeDtypeStruct((bsz, tp, B_WIDTH), bf16),
        compiler_params=_cparams(("parallel", "arbitrary")),
        name="fattn",
    )(l, zcat3, bka, zcat3, gbq)


def _gla_kernel(l_ref, cq_ref, ck_ref, cv_ref, cg_ref, zs_ref, w2_ref, gb_ref, og_ref, e_ref,
                out_ref, st_ref, oi_ref, pp_ref, rr_ref, *, t_real):
    c = pl.program_id(0)
    nb, cc, dk = cq_ref.shape

    @pl.when(c == 0)
    def _():
        st_ref[...] = jnp.zeros_like(st_ref)

    rowc = lax.broadcasted_iota(i32, (cc, 1), 0)
    pos = c * cc + rowc
    valid = (pos >= FRONT) & (pos < FRONT + t_real)
    tri = (lax.broadcasted_iota(i32, (cc, cc), 0) >= lax.broadcasted_iota(i32, (cc, cc), 1)).astype(bf16)
    lane_k = lax.broadcasted_iota(i32, (1, dk), 1)
    head_masks = [(lane_k >= h * C_DK) & (lane_k < (h + 1) * C_DK) for h in range(C_HEADS)]

    used = [(r // SUBLANES + 1) * SUBLANES for r in range(cc)]

    for b in range(nb):
        x = _dot(zs_ref[b].astype(bf16), w2_ref[...]) + gb_ref[...]
        la = jnp.where(valid, _log_sigmoid(x) * (1.0 / C_TAU), 0.0)
        hi, mid, lo = _split3(la)
        bc = _dot(tri, hi) + _dot(tri, mid) + _dot(tri, lo)
        q = cq_ref[b].astype(f32) * (C_DK ** -0.5)
        k = ck_ref[b].astype(f32)
        v = cv_ref[b]
        bc2 = bc * LOG2E
        blast2 = bc2[cc - 1:cc, :]
        qe = q * jnp.exp2(bc2)
        ke = k * jnp.exp2(blast2 - bc2)
        st = st_ref[b]
        stb = st.astype(bf16)
        oi_ref[b] = jnp.concatenate(
            [_dot_nt(jnp.where(head_masks[h], qe, 0.0).astype(bf16), stb) for h in range(C_HEADS)], axis=1)
        new_st = st * jnp.exp2(blast2)
        for h in range(C_HEADS):
            km = jnp.where(head_masks[h], ke, 0.0).astype(bf16)
            new_st = new_st + _dot_tn(v[:, h * C_DV:(h + 1) * C_DV], km)
        st_ref[b] = new_st
        for r in range(cc):
            nu = used[r]
            dec = jnp.exp2(jnp.minimum(bc2[r:r + 1, :] - bc2[0:nu, :], 0.0))
            pr = jnp.where(rowc[0:nu] <= r, q[r:r + 1, :] * k[0:nu, :] * dec, 0.0)
            if nu < cc:
                pr = jnp.concatenate([pr, jnp.zeros((cc - nu, dk), f32)], axis=0)
            pp_ref[b, r * cc:(r + 1) * cc, :] = pr.astype(bf16)

    for b in range(nb):
        rr_ref[b] = _dot(pp_ref[b], e_ref[...])

    for b in range(nb):
        vf = cv_ref[b].astype(f32)
        for r in range(cc):
            nu = used[r]
            oi_ref[b, r:r + 1, :] += jnp.sum(rr_ref[b, r * cc:r * cc + nu, :] * vf[0:nu, :], axis=0, keepdims=True)
        o = oi_ref[b]
        g = cg_ref[b].astype(f32)
        gs = g * (1.0 / (1.0 + jnp.exp(-g)))
        for h in range(C_HEADS):
            sl = slice(h * C_DV, (h + 1) * C_DV)
            out_ref[b, :, sl] = (_rms_gain(o[:, sl], og_ref[...]) * gs[:, sl]).astype(bf16)


def _gla(l, zcat3, zs3, w2p, gb, og, emat, t_real):
    bsz, tp, _ = zcat3.shape
    cc = GLA_C
    dk = C_HEADS * C_DK
    return pl.pallas_call(
        functools.partial(_gla_kernel, t_real=t_real),
        grid_spec=pltpu.PrefetchScalarGridSpec(
            num_scalar_prefetch=1, grid=(tp // cc,),
            in_specs=[
                pl.BlockSpec((bsz, cc, dk), lambda c, l: (0, c, T_CQ * LANES // dk)),
                pl.BlockSpec((bsz, cc, dk), lambda c, l: (0, c, T_CK * LANES // dk)),
                pl.BlockSpec((bsz, cc, C_WIDTH), lambda c, l: (0, c, T_CV * LANES // C_WIDTH)),
                pl.BlockSpec((bsz, cc, C_WIDTH), lambda c, l: (0, c, T_CG * LANES // C_WIDTH)),
                pl.BlockSpec((bsz, cc, LANES), lambda c, l: (0, c, 0)),
                pl.BlockSpec((None, LANES, dk), lambda c, l: (l[0], 0, 0)),
                pl.BlockSpec((None, 1, dk), lambda c, l: (l[0], 0, 0)),
                pl.BlockSpec((None, 1, C_DV), lambda c, l: (l[0], 0, 0)),
                pl.BlockSpec((dk, C_WIDTH), lambda c, l: (0, 0)),
            ],
            out_specs=pl.BlockSpec((bsz, cc, C_WIDTH), lambda c, l: (0, c, 0)),
            scratch_shapes=[pltpu.VMEM((bsz, C_DV, dk), f32), pltpu.VMEM((bsz, cc, C_WIDTH), f32),
                            pltpu.VMEM((bsz, cc * cc, dk), bf16), pltpu.VMEM((bsz, cc * cc, C_WIDTH), f32)]),
        out_shape=jax.ShapeDtypeStruct((bsz, tp, C_WIDTH), bf16),
        compiler_params=_cparams(("arbitrary",)),
        name="gla",
    )(l, zcat3, zcat3, zcat3, zcat3, zs3, w2p, gb, og, emat)


def _outproj_kernel(l_ref, oa_ref, ob_ref, oc_ref, h_ref, wo_ref, g_ref, wr_ref, br_ref,
                    hmid_ref, u_ref, eid_ref, ew_ref, *, t_real):
    tm = h_ref.shape[0]
    mix = (_dot(oa_ref[...], wo_ref[0:A_WIDTH, :])
           + _dot(ob_ref[...], wo_ref[A_WIDTH:A_WIDTH + B_WIDTH, :])
           + _dot(oc_ref[...], wo_ref[A_WIDTH + B_WIDTH:, :]))
    pos = pl.program_id(1) * tm + lax.broadcasted_iota(i32, (tm, 1), 0)
    valid = (pos >= FRONT) & (pos < FRONT + t_real)
    hm = h_ref[...] + jnp.where(valid, mix, 0.0)
    hmid_ref[...] = hm
    u = _rms_gain(hm, g_ref[...]).astype(bf16)
    u_ref[...] = u

    logits = _dot(u, wr_ref[...]) + br_ref[...]
    lane = lax.broadcasted_iota(i32, (tm, LANES), 1)
    lanef = lane.astype(f32)
    big = float(4 * LANES)
    first = lambda hit: jnp.min(jnp.where(hit, lanef, big), axis=-1, keepdims=True).astype(i32)
    gl = jnp.where(lane < R_GRP + N_GROUPS, logits, -jnp.inf)
    gmax = jnp.max(gl, axis=-1, keepdims=True)
    g_p = 1.0 / jnp.sum(jnp.exp(gl - gmax), axis=-1, keepdims=True)
    g_i = first(gl == gmax)
    e_lane = lane - R_EXP
    emask = (e_lane >= 0) & (e_lane < N_EXPERTS) & ((e_lane >> 3) == g_i)
    el = jnp.where(emask, logits, -jnp.inf)
    m1 = jnp.max(el, axis=-1, keepdims=True)
    i1 = first(el == m1)
    el2 = jnp.where(lane == i1, -jnp.inf, el)
    m2 = jnp.max(el2, axis=-1, keepdims=True)
    i2 = first(el2 == m2)
    r = jnp.exp(m2 - m1)
    w1 = g_p / (1.0 + r)
    w2 = g_p * r / (1.0 + r)
    eid_ref[...] = jnp.where(lane == 0, i1 - R_EXP, jnp.where(lane == 1, i2 - R_EXP, 0))
    ew_ref[...] = jnp.where(lane == 0, w1, jnp.where(lane == 1, w2, 0.0))


def _seq_tile(tp, cap):
    return max(t for t in range(BLK, cap + 1, BLK) if tp % t == 0)


def _outproj(l, oa, ob, oc, h, wo, gain, wr, br, t_real, tp):
    n, d = h.shape
    tm = _seq_tile(tp, 384)
    nj = tp // tm
    row = lambda w: pl.BlockSpec((tm, w), lambda b, j, l: (b * nj + j, 0))
    return pl.pallas_call(
        functools.partial(_outproj_kernel, t_real=t_real),
        grid_spec=pltpu.PrefetchScalarGridSpec(
            num_scalar_prefetch=1, grid=(n // tp, nj),
            in_specs=[row(A_WIDTH), row(B_WIDTH), row(C_WIDTH), row(d),
                      pl.BlockSpec((None, d, d), lambda b, j, l: (l[0], 0, 0)),
                      pl.BlockSpec((None, 1, d), lambda b, j, l: (l[0], 0, 0)),
                      pl.BlockSpec((None, d, LANES), lambda b, j, l: (l[0], 0, 0)),
                      pl.BlockSpec((None, 1, LANES), lambda b, j, l: (l[0], 0, 0))],
            out_specs=[row(d), row(d), row(LANES), row(LANES)]),
        out_shape=[jax.ShapeDtypeStruct((n, d), f32), jax.ShapeDtypeStruct((n, d), bf16),
                   jax.ShapeDtypeStruct((n, LANES), i32), jax.ShapeDtypeStruct((n, LANES), f32)],
        compiler_params=_cparams(("parallel", "parallel")),
        name="outproj",
    )(l, oa, ob, oc, h, wo, gain, wr, br)


def _moe_kernel(l_ref, te_ref, na_ref, xs_ref, w1_ref, w3_ref, w2_ref, rw_ref, ys_ref,
                w1b_ref, w3b_ref, w2b_ref):
    i = pl.program_id(0)

    @pl.when((i == 0) | (te_ref[i] != te_ref[jnp.maximum(i - 1, 0)]))
    def _():
        w1b_ref[...] = w1_ref[...].astype(bf16)
        w3b_ref[...] = w3_ref[...].astype(bf16)
        w2b_ref[...] = w2_ref[...].astype(bf16)

    @pl.when(i < na_ref[0])
    def _():
        x = xs_ref[...]
        h1 = _dot(x, w1b_ref[...])
        h3 = _dot(x, w3b_ref[...])
        hid = (h1 * (1.0 / (1.0 + jnp.exp(-h1))) * h3).astype(bf16)
        ys_ref[...] = (_dot(hid, w2b_ref[...]) * rw_ref[...]).astype(bf16)

    @pl.when(i >= na_ref[0])
    def _():
        ys_ref[...] = jnp.zeros_like(ys_ref)


def _moe(l, te, na, xs, w1, w3, w2, rw):
    p, d = xs.shape
    tm = MOE_TM
    return pl.pallas_call(
        _moe_kernel,
        grid_spec=pltpu.PrefetchScalarGridSpec(
            num_scalar_prefetch=3, grid=(p // tm,),
            in_specs=[
                pl.BlockSpec((tm, d), lambda i, l, te, na: (i, 0)),
                pl.BlockSpec((None, None, d, D_EXPERT), lambda i, l, te, na: (l[0], te[i], 0, 0)),
                pl.BlockSpec((None, None, d, D_EXPERT), lambda i, l, te, na: (l[0], te[i], 0, 0)),
                pl.BlockSpec((None, None, D_EXPERT, d), lambda i, l, te, na: (l[0], te[i], 0, 0)),
                pl.BlockSpec((tm, 1), lambda i, l, te, na: (i, 0)),
            ],
            out_specs=pl.BlockSpec((tm, d), lambda i, l, te, na: (i, 0)),
            scratch_shapes=[pltpu.VMEM((d, D_EXPERT), bf16), pltpu.VMEM((d, D_EXPERT), bf16),
                            pltpu.VMEM((D_EXPERT, d), bf16)]),
        out_shape=jax.ShapeDtypeStruct((p, d), bf16),
        compiler_params=_cparams(("arbitrary",)),
        name="moe",
    )(l, te, na, xs, w1, w3, w2, rw)


def _route(eid, ew, n_tiles):
    tm = MOE_TM
    p = n_tiles * tm
    n = eid.shape[0]
    e_flat = eid[:, :EXP_TOPK_CONST].T.reshape(-1)
    w_flat = ew[:, :EXP_TOPK_CONST].T.reshape(-1)
    na_all = e_flat.shape[0]
    hp = lax.Precision.HIGHEST
    ex = jnp.arange(N_EXPERTS, dtype=i32)[:, None]
    onehot = (ex == e_flat[None, :]).astype(f32)
    counts = jnp.sum(onehot, axis=1).astype(i32)
    pc = ((counts + tm - 1) // tm) * tm
    pend = jnp.cumsum(pc)
    po = pend - pc
    co = jnp.cumsum(counts) - counts
    order = jnp.argsort(e_flat, stable=True).astype(i32)
    r = jnp.arange(p, dtype=i32)
    step = (r[None, :] >= pend[:, None]).astype(f32)
    dlt = lambda v: jnp.concatenate([v[1:] - v[:-1], jnp.zeros((1,), v.dtype)]).astype(f32)
    tabs = jnp.stack([jnp.ones((N_EXPERTS,), f32), dlt(po), dlt(counts), dlt(co)])
    picked = jnp.dot(tabs, step, precision=hp).astype(i32)
    e_r = jnp.minimum(picked[0], N_EXPERTS - 1)
    local = r - (po[0] + picked[1])
    valid_r = (local < counts[0] + picked[2]) & (r < pend[-1])
    a_r = order[jnp.clip(co[0] + picked[3] + local, 0, na_all - 1)]
    tok_r = jnp.where(valid_r, jnp.where(a_r >= n, a_r - n, a_r), r % n)
    w_r = jnp.where(valid_r, w_flat[a_r], 0.0)
    n_act = (pend[-1] // tm).astype(i32)
    tile_e = e_r[::tm]
    te = jnp.where(jnp.arange(n_tiles, dtype=i32) < n_act, tile_e, tile_e[jnp.maximum(n_act - 1, 0)])
    inv = jnp.argsort(order).astype(i32)
    pos_a = jnp.dot((po - co).astype(f32)[None, :], onehot, precision=hp)[0].astype(i32) + inv
    return tok_r, w_r[:, None], te, n_act.reshape(1), pos_a


def _final_kernel(hmid_ref, ya_ref, yb_ref, out_ref):
    out_ref[...] = hmid_ref[...] + ya_ref[...].astype(f32) + yb_ref[...].astype(f32)


def _final(hmid, y2):
    n, d = hmid.shape
    tm = _row_tile(n, 512)
    nt = n // tm
    return pl.pallas_call(
        _final_kernel, grid=(nt,),
        in_specs=[pl.BlockSpec((tm, d), lambda i: (i, 0)), pl.BlockSpec((tm, d), lambda i: (i, 0)),
                  pl.BlockSpec((tm, d), lambda i: (i + nt, 0))],
        out_specs=pl.BlockSpec((tm, d), lambda i: (i, 0)),
        out_shape=jax.ShapeDtypeStruct((n, d), f32),
        compiler_params=_cparams(("parallel",)),
        name="final",
    )(hmid, y2, y2)


def _rope_tables(tp):
    pos = (jnp.arange(tp, dtype=f32) - FRONT)[:, None]

    def tab(dim, reps):
        inv = 1.0 / (ROPE_THETA ** (jnp.arange(0, dim, 2, dtype=f32) / dim))
        ang = pos * inv[None, :]
        return jnp.tile(jnp.cos(ang), (1, reps)), jnp.tile(jnp.sin(ang), (1, reps))

    cos, sin = tab(HEAD_DIM, 2)
    icos, isin = tab(IDX_DIM, 4)
    return cos, sin, icos, isin


def _split_w_in(w_in):
    parts, off = [], 0
    for s in IN_SPLITS:
        parts.append(w_in[..., off:off + s])
        off += s
    return parts


def kernel(x, meta, norm_mix_g, norm_ffn_g, w_in, a_qn_g, a_kn_g, b_qn_g, b_kn_g, b_f_bias,
           c_gate_w2, c_gate_b, c_on_g, w_out, r_group_w, r_group_b, r_exp_w, r_exp_b,
           e_w1, e_w3, e_w2):
    bsz, n_seq, d = x.shape
    depth = w_in.shape[0]
    t_real = n_seq + N_META
    k_top = min(TOPK_MAX, n_seq // 4)
    tp = -(-(FRONT + t_real) // BLK) * BLK
    n = bsz * tp

    (waq, wak, wav, wiq, wik, wiw, wbq, wbk, wbv, wbf, wcq, wck, wcv, wcr, wcg) = _split_w_in(w_in)
    zc = lambda w: jnp.zeros((depth, d, w), w_in.dtype)
    wcat = jnp.concatenate([waq, wak, wav, wiq, wik, zc(LANES - IDX_DIM), zc(LANES),
                            wbq, wbk, wbv, wcq, wck, wcv, wcg], axis=-1).astype(bf16)
    ws = jnp.concatenate([wiw, wbf, wcr, zc(LANES - S_CR - C_GATE_RANK)], axis=-1).astype(bf16)
    wo = w_out.astype(bf16)
    wr = jnp.concatenate([r_group_w, zc(R_EXP - N_GROUPS), r_exp_w, zc(LANES - R_EXP - N_EXPERTS)],
                         axis=-1).astype(bf16)
    zl = lambda w: jnp.zeros((depth, w), f32)
    br = jnp.concatenate([r_group_b, zl(R_EXP - N_GROUPS), r_exp_b, zl(LANES - R_EXP - N_EXPERTS)],
                         axis=-1)[:, None, :]
    fbias = jnp.concatenate([zl(S_BF), b_f_bias, zl(LANES - S_BF - B_HEADS)], axis=-1)[:, None, :]
    dk = C_HEADS * C_DK
    w2p = jnp.concatenate([jnp.zeros((depth, S_CR, dk), f32), c_gate_w2,
                           jnp.zeros((depth, LANES - S_CR - C_GATE_RANK, dk), f32)], axis=1).astype(bf16)
    gb = c_gate_b[:, None, :]
    og = c_on_g[:, None, :]
    g_mix = norm_mix_g[:, None, :]
    g_ffn = norm_ffn_g[:, None, :]
    gaq, gak, gbq, gbk = (g[:, None, :] for g in (a_qn_g, a_kn_g, b_qn_g, b_kn_g))
    emat = (jnp.arange(dk, dtype=i32)[:, None] // C_DK
            == jnp.arange(C_WIDTH, dtype=i32)[None, :] // C_DV).astype(bf16)
    tabs = _rope_tables(tp)

    h0 = jnp.concatenate([
        jnp.zeros((bsz, FRONT, d), f32),
        jnp.broadcast_to(meta[None].astype(f32), (bsz, N_META, d)),
        x.astype(f32),
        jnp.zeros((bsz, tp - FRONT - t_real, d), f32)], axis=1).reshape(n, d)
    n_tiles = -(-(EXP_TOPK_CONST * n + N_EXPERTS * (MOE_TM - 1)) // MOE_TM)

    def layer(li, carry):
        hmid, y2 = carry
        l = jnp.reshape(jnp.asarray(li, i32), (1,))
        h, zcat, zs = _inproj(l, hmid, y2, g_mix, wcat, ws)
        zcat3 = zcat.reshape(bsz, tp, N_CAT)
        zs3 = zs.reshape(bsz, tp, LANES)
        akr, ikr, bka = _kprep(l, zcat3, zs3, tabs, gak, gbk, fbias, t_real)
        oa = _dsa(l, zcat3, zs3, akr, ikr, tabs, gaq, t_real, k_top)
        ob = _fattn(l, zcat3, bka, gbq)
        oc = _gla(l, zcat3, zs3, w2p, gb, og, emat, t_real)
        hmid2, u, eid, ew = _outproj(l, oa.reshape(n, A_WIDTH), ob.reshape(n, B_WIDTH),
                                     oc.reshape(n, C_WIDTH), h, wo, g_ffn, wr, br, t_real, tp)
        tok_r, w_r, te, n_act, pos_a = _route(eid, ew, n_tiles)
        ys = _moe(l, te, n_act, jnp.take(u, tok_r, axis=0, mode="clip"), e_w1, e_w3, e_w2, w_r)
        return hmid2, jnp.take(ys, pos_a, axis=0, mode="clip")

    y0 = lax.optimization_barrier(jnp.zeros((EXP_TOPK_CONST * n, d), bf16))
    hmid, y2 = lax.fori_loop(0, depth, layer, (h0, y0))
    out = _final(hmid, y2).reshape(bsz, tp, d)
    return out[:, FRONT + N_META:FRONT + t_real].astype(x.dtype)
```

```python
import functools

import jax
import jax.numpy as jnp
from jax import lax
from jax.experimental import pallas as pl
from jax.experimental.pallas import tpu as pltpu

f32 = jnp.float32
bf16 = jnp.bfloat16
i32 = jnp.int32

D_MODEL = 2048
CHUNK = 64
N_META = 16
ROPE_THETA = 10000.0
EPS = 1e-6
HEAD_DIM = 128
A_HEADS = 6
IDX_HEADS = 16
IDX_DIM = 64
TOPK_MAX = 256
B_HEADS = 6
C_HEADS = 4
C_DK = 64
C_DV = 128
C_GATE_RANK = 16
C_TAU = 16.0
N_GROUPS = 4
EXP_PER_GROUP = 8
N_EXPERTS = N_GROUPS * EXP_PER_GROUP
D_EXPERT = 512
A_WIDTH = A_HEADS * HEAD_DIM
B_WIDTH = B_HEADS * HEAD_DIM
C_WIDTH = C_HEADS * C_DV
IN_SPLITS = (A_WIDTH, HEAD_DIM, HEAD_DIM, IDX_HEADS * IDX_DIM, IDX_DIM, IDX_HEADS,
             B_WIDTH, B_WIDTH, B_WIDTH, B_HEADS,
             C_HEADS * C_DK, C_HEADS * C_DK, C_WIDTH, C_GATE_RANK, C_WIDTH)

LANES = 128
SUBLANES = 8
FRONT = (-N_META) % CHUNK
BLK = 128
T_AQ, T_AK, T_AV, T_IQ, T_IK = 0, 6, 7, 8, 16
T_BQ, T_BK, T_BV = 18, 24, 30
T_CQ, T_CK, T_CV, T_CG = 36, 38, 40, 44
N_CAT = 48 * LANES
S_IW, S_BF, S_CR = 0, 16, 22
R_GRP, R_EXP = 0, 32
INPROJ_TN = 1536
GLA_C = 32
DSA_G = 3
MOE_TM = 256
EXP_TOPK_CONST = 2
NEG = -1e30
LOG2E = 1.4426950408889634
INT_MIN = -2 ** 31
VMEM_LIMIT = 56 * 1024 * 1024


def _cparams(sem):
    return pltpu.CompilerParams(dimension_semantics=sem, vmem_limit_bytes=VMEM_LIMIT)


def _row_tile(n, cap):
    t = cap
    while n % t:
        t //= 2
    return t


def _log_sigmoid(x):
    return jnp.minimum(x, 0.0) - jnp.log(1.0 + jnp.exp(-jnp.abs(x)))


def _dot(a, b):
    return jnp.dot(a, b, preferred_element_type=f32)


def _dot_nt(a, b):
    return lax.dot_general(a, b, (((1,), (1,)), ((), ())), preferred_element_type=f32)


def _dot_tn(a, b):
    return lax.dot_general(a, b, (((0,), (0,)), ((), ())), preferred_element_type=f32)


def _split3(x):
    hi = x.astype(bf16)
    r1 = x - hi.astype(f32)
    mid = r1.astype(bf16)
    lo = (r1 - mid.astype(f32)).astype(bf16)
    return hi, mid, lo


def _rope128(x, cos, sin, lane):
    return x * cos + pltpu.roll(x, 64, 1) * jnp.where(lane < 64, -sin, sin)


def _rope64(x, cos, sin, lane):
    low = (lane & 63) < 32
    return (x * cos + pltpu.roll(x, 32, 1) * jnp.where(low, 0.0, sin)
            + pltpu.roll(x, 96, 1) * jnp.where(low, -sin, 0.0))


def _rms_gain(x, g):
    return x * lax.rsqrt(jnp.mean(x * x, axis=-1, keepdims=True) + EPS) * g


def _prenorm_kernel(l_ref, hmid_ref, ya_ref, yb_ref, g_ref, ws_ref, h_ref, xn_ref, zs_ref):
    h = hmid_ref[...] + ya_ref[...].astype(f32) + yb_ref[...].astype(f32)
    h_ref[...] = h
    xn = _rms_gain(h, g_ref[...]).astype(bf16)
    xn_ref[...] = xn
    zs_ref[...] = _dot(xn, ws_ref[...])


def _inproj_kernel(l_ref, xn_ref, w_ref, z_ref):
    z_ref[...] = _dot(xn_ref[...], w_ref[...]).astype(bf16)


def _inproj(l, hmid, y2, gain, wcat, ws):
    n, d = hmid.shape
    tm = _row_tile(n, 512)
    nt = n // tm
    row = lambda w: pl.BlockSpec((tm, w), lambda i, l: (i, 0))
    h, xn, zs = pl.pallas_call(
        _prenorm_kernel,
        grid_spec=pltpu.PrefetchScalarGridSpec(
            num_scalar_prefetch=1, grid=(nt,),
            in_specs=[row(d), row(d),
                      pl.BlockSpec((tm, d), lambda i, l: (i + nt, 0)),
                      pl.BlockSpec((None, 1, d), lambda i, l: (l[0], 0, 0)),
                      pl.BlockSpec((None, d, LANES), lambda i, l: (l[0], 0, 0))],
            out_specs=[row(d), row(d), row(LANES)]),
        out_shape=[jax.ShapeDtypeStruct((n, d), f32), jax.ShapeDtypeStruct((n, d), bf16),
                   jax.ShapeDtypeStruct((n, LANES), f32)],
        compiler_params=_cparams(("parallel",)),
        name="prenorm",
    )(l, hmid, y2, y2, gain, ws)
    tn = INPROJ_TN
    zcat = pl.pallas_call(
        _inproj_kernel,
        grid_spec=pltpu.PrefetchScalarGridSpec(
            num_scalar_prefetch=1, grid=(N_CAT // tn, nt),
            in_specs=[pl.BlockSpec((tm, d), lambda j, i, l: (i, 0)),
                      pl.BlockSpec((None, d, tn), lambda j, i, l: (l[0], 0, j))],
            out_specs=pl.BlockSpec((tm, tn), lambda j, i, l: (i, j))),
        out_shape=jax.ShapeDtypeStruct((n, N_CAT), bf16),
        compiler_params=_cparams(("parallel", "parallel")),
        name="inproj",
    )(l, xn, wcat)
    return h, zcat, zs


def _kprep_kernel(l_ref, ak_ref, ik_ref, bk_ref, zs_ref,
                  cos_ref, sin_ref, icos_ref, isin_ref, gak_ref, gbk_ref, fb_ref,
                  akr_ref, ikr_ref, bka_ref, carry_ref, *, t_real):
    k = pl.program_id(1)
    lane = lax.broadcasted_iota(i32, (BLK, LANES), 1)
    row = lax.broadcasted_iota(i32, (BLK, LANES), 0)
    pos = k * BLK + row
    valid = (pos >= FRONT) & (pos < FRONT + t_real)

    x = ak_ref[...].astype(f32)
    akr_ref[...] = _rope128(_rms_gain(x, gak_ref[...]), cos_ref[...], sin_ref[...], lane).astype(bf16)
    ikr_ref[...] = _rope64(ik_ref[...].astype(f32), icos_ref[...], isin_ref[...], lane).astype(bf16)

    @pl.when(k == 0)
    def _():
        carry_ref[...] = jnp.zeros_like(carry_ref)

    lf = jnp.where(valid, _log_sigmoid(zs_ref[...] + fb_ref[...]), 0.0)
    tri = (row >= lane).astype(bf16)
    hi, mid, lo = _split3(lf)
    fcum = _dot(tri, hi) + _dot(tri, mid) + _dot(tri, lo) + carry_ref[...]
    carry_ref[...] = fcum[BLK - 1:BLK, :]

    fs = jnp.where(valid, fcum * (-(HEAD_DIM ** 0.5)), NEG)
    for h in range(B_HEADS):
        sl = slice(h * LANES, (h + 1) * LANES)
        bka_ref[h, :, 0:LANES] = _rms_gain(bk_ref[:, sl].astype(f32), gbk_ref[...]).astype(bf16)
        p0, p1, p2 = _split3(fs[:, S_BF + h:S_BF + h + 1])
        aug = jnp.where(lane == 0, p0.astype(f32),
                        jnp.where(lane == 1, p1.astype(f32), jnp.where(lane == 2, p2.astype(f32), 0.0)))
        bka_ref[h, :, LANES:2 * LANES] = aug.astype(bf16)


def _kprep(l, zcat3, zs3, tabs, gak, gbk, fbias, t_real):
    bsz, tp, _ = zcat3.shape
    nkb = tp // BLK
    cos, sin, icos, isin = tabs
    tile = lambda c: pl.BlockSpec((None, BLK, LANES), lambda b, k, l, c=c: (b, k, c))
    wide = lambda c: pl.BlockSpec((None, BLK, B_WIDTH), lambda b, k, l, c=c: (b, k, c))
    tab = pl.BlockSpec((BLK, LANES), lambda b, k, l: (k, 0))
    gain = pl.BlockSpec((None, 1, LANES), lambda b, k, l: (l[0], 0, 0))
    return pl.pallas_call(
        functools.partial(_kprep_kernel, t_real=t_real),
        grid_spec=pltpu.PrefetchScalarGridSpec(
            num_scalar_prefetch=1, grid=(bsz, nkb),
            in_specs=[tile(T_AK), tile(T_IK),
                      wide(T_BK * LANES // B_WIDTH),
                      pl.BlockSpec((None, BLK, LANES), lambda b, k, l: (b, k, 0)),
                      tab, tab, tab, tab, gain, gain, gain],
            out_specs=[
                pl.BlockSpec((None, BLK, LANES), lambda b, k, l: (b, k, 0)),
                pl.BlockSpec((None, BLK, LANES), lambda b, k, l: (b, k, 0)),
                pl.BlockSpec((None, B_HEADS, BLK, 2 * LANES), lambda b, k, l: (b, 0, k, 0)),
            ],
            scratch_shapes=[pltpu.VMEM((1, LANES), f32)]),
        out_shape=[jax.ShapeDtypeStruct((bsz, tp, LANES), bf16),
                   jax.ShapeDtypeStruct((bsz, tp, LANES), bf16),
                   jax.ShapeDtypeStruct((bsz, B_HEADS, tp, 2 * LANES), bf16)],
        compiler_params=_cparams(("parallel", "arbitrary")),
        name="kprep",
    )(l, zcat3, zcat3, zcat3, zs3, cos, sin, icos, isin, gak, gbk, fbias)


def _dsa_kernel(l_ref, aq_ref, iq_ref, zs_ref, akr_ref, ikr_ref, av_ref,
                cos_ref, sin_ref, icos_ref, isin_ref, gq_ref, out_ref,
                key_ref, iqs_ref, q6_ref, iwt_ref, m_ref, s_ref, acc_ref, stq_ref, *, t_real, k_top):
    i = pl.program_id(1)
    nk = i + 1
    lane = lax.broadcasted_iota(i32, (BLK, LANES), 1)

    for h in range(A_HEADS):
        x = aq_ref[:, h * LANES:(h + 1) * LANES].astype(f32)
        xr = _rope128(_rms_gain(x, gq_ref[...]), cos_ref[...], sin_ref[...], lane)
        q6_ref[h * BLK:(h + 1) * BLK, :] = xr.astype(bf16)
    for t in range(IDX_HEADS // 2):
        x = iq_ref[:, t * LANES:(t + 1) * LANES].astype(f32)
        xr = _rope64(x, icos_ref[...], isin_ref[...], lane)
        iqs_ref[(2 * t) * BLK:(2 * t + 1) * BLK, :] = xr.astype(bf16)
        iqs_ref[(2 * t + 1) * BLK:(2 * t + 2) * BLK, :] = pltpu.roll(xr, 64, 1).astype(bf16)
    iwt_ref[...] = (zs_ref[...] * (IDX_HEADS ** -0.5 * IDX_DIM ** -0.5)).T

    gb = DSA_G * BLK
    ng = lax.div(nk, jnp.int32(DSA_G))
    nr = nk - ng * DSA_G

    def over_keys(fn, init, pairs=False):
        c, g0 = init, 0
        if pairs:
            ng2 = lax.div(ng, jnp.int32(2))
            c = lax.fori_loop(0, ng2, lambda g, c: fn(pl.multiple_of(g * 2 * gb, gb), 2 * gb, c), c)
            g0 = ng2 * 2
        c = lax.fori_loop(g0, ng, lambda g, c: fn(pl.multiple_of(g * gb, gb), gb, c), c)
        return lax.fori_loop(0, nr, lambda r, c: fn(pl.multiple_of((ng * DSA_G + r) * BLK, BLK), BLK, c), c)

    def score_rows(k0, nrows, carry):
        dt = _dot_nt(ikr_ref[pl.ds(k0, nrows), :], iqs_ref[...])
        s = jnp.zeros((nrows, LANES), f32)
        for h in range(IDX_HEADS):
            s = s + iwt_ref[h:h + 1, :] * jnp.maximum(dt[:, h * LANES:(h + 1) * LANES], 0.0)
        kpos = k0 + lax.broadcasted_iota(i32, (nrows, LANES), 0)
        qpos = i * BLK + lax.broadcasted_iota(i32, (nrows, LANES), 1)
        adm = ((kpos >> 6) <= (qpos >> 6)) & (kpos >= FRONT) & (kpos < FRONT + t_real)
        bits = lax.bitcast_convert_type(s, i32)
        key = bits ^ ((bits >> 31) & 0x7FFFFFFF)
        key_ref[pl.ds(k0, nrows), :] = jnp.where(adm, key, INT_MIN)
        return carry

    over_keys(score_rows, 0)

    def bit_body(t, thr_u):
        bit = jnp.left_shift(jnp.int32(1), 31 - t)
        cand_u = thr_u | bit
        cand_s = cand_u ^ INT_MIN

        def count_rows(k0, nrows, c):
            hit = (key_ref[pl.ds(k0, nrows), :] >= cand_s).astype(i32)
            for j in range(nrows // BLK):
                c = c + hit[j * BLK:(j + 1) * BLK, :]
            return c

        cnt = over_keys(count_rows, jnp.zeros((BLK, LANES), i32))
        tot = jnp.sum(cnt.astype(f32), axis=0, keepdims=True)
        return jnp.where(tot >= k_top, cand_u, thr_u)

    thr_u = lax.fori_loop(0, 32, bit_body, jnp.zeros((1, LANES), i32))
    thr_s = jnp.maximum(thr_u ^ INT_MIN, INT_MIN + 1)

    m_ref[...] = jnp.full(m_ref.shape, NEG, f32)
    s_ref[...] = jnp.zeros(s_ref.shape, f32)
    acc_ref[...] = jnp.zeros(acc_ref.shape, f32)
    c2 = HEAD_DIM ** -0.5 * LOG2E

    def attn_rows(k0, nrows, carry, st=None):
        if st is None:
            st = _dot_nt(akr_ref[pl.ds(k0, nrows), :], q6_ref[...])
        bias = jnp.where(key_ref[pl.ds(k0, nrows), :] >= thr_s, 0.0, NEG)
        m_news = []
        for h in range(A_HEADS):
            sl = slice(h * LANES, (h + 1) * LANES)
            sh = st[:, sl] + bias
            stq_ref[0, 0:nrows, sl] = sh
            m_news.append(jnp.maximum(m_ref[:, sl], jnp.max(sh, axis=0, keepdims=True)))
        ps, alphas = [], []
        for h in range(A_HEADS):
            sl = slice(h * LANES, (h + 1) * LANES)
            sh = stq_ref[0, 0:nrows, sl]
            m_old = m_ref[:, sl]
            m_new = m_news[h]
            alpha = jnp.exp2((m_old - m_new) * c2)
            p = jnp.exp2((sh - m_new) * c2)
            s_ref[:, sl] = s_ref[:, sl] * alpha + jnp.sum(p, axis=0, keepdims=True)
            m_ref[:, sl] = m_new
            alphas.append(alpha)
            ps.append(p.astype(bf16))
        pv = _dot_tn(av_ref[pl.ds(k0, nrows), :], jnp.concatenate(ps, axis=1))
        acc_ref[...] = acc_ref[...] * jnp.concatenate(alphas, axis=1) + pv
        return carry

    over_keys(attn_rows, 0, pairs=True)

    for h in range(A_HEADS):
        sl = slice(h * LANES, (h + 1) * LANES)
        o = acc_ref[:, sl] / jnp.maximum(s_ref[:, sl], 1e-30)
        out_ref[:, sl] = o.T.astype(bf16)


def _dsa(l, zcat3, zs3, akr, ikr, tabs, gaq, t_real, k_top):
    bsz, tp, _ = zcat3.shape
    nkb = tp // BLK
    cos, sin, icos, isin = tabs
    tab = pl.BlockSpec((BLK, LANES), lambda b, i, l: (i, 0))
    full = pl.BlockSpec((None, tp, LANES), lambda b, i, l: (b, 0, 0))
    return pl.pallas_call(
        functools.partial(_dsa_kernel, t_real=t_real, k_top=k_top),
        grid_spec=pltpu.PrefetchScalarGridSpec(
            num_scalar_prefetch=1, grid=(bsz, nkb),
            in_specs=[
                pl.BlockSpec((None, BLK, A_WIDTH), lambda b, i, l: (b, i, 0)),
                pl.BlockSpec((None, BLK, IDX_HEADS * IDX_DIM), lambda b, i, l: (b, i, T_IQ * LANES // (IDX_HEADS * IDX_DIM))),
                pl.BlockSpec((None, BLK, LANES), lambda b, i, l: (b, i, 0)),
                full, full,
                pl.BlockSpec((None, tp, LANES), lambda b, i, l: (b, 0, T_AV)),
                tab, tab, tab, tab,
                pl.BlockSpec((None, 1, LANES), lambda b, i, l: (l[0], 0, 0)),
            ],
            out_specs=pl.BlockSpec((None, BLK, A_WIDTH), lambda b, i, l: (b, i, 0)),
            scratch_shapes=[
                pltpu.VMEM((tp, LANES), i32),
                pltpu.VMEM((IDX_HEADS * BLK, LANES), bf16),
                pltpu.VMEM((A_HEADS * BLK, LANES), bf16),
                pltpu.VMEM((LANES, BLK), f32),
                pltpu.VMEM((1, A_HEADS * BLK), f32),
                pltpu.VMEM((1, A_HEADS * BLK), f32),
                pltpu.VMEM((HEAD_DIM, A_HEADS * BLK), f32),
                pltpu.VMEM((1, 2 * DSA_G * BLK, A_HEADS * BLK), f32),
            ]),
        out_shape=jax.ShapeDtypeStruct((bsz, tp, A_WIDTH), bf16),
        compiler_params=_cparams(("parallel", "arbitrary")),
        name="dsa",
    )(l, zcat3, zcat3, zs3, akr, ikr, zcat3, cos, sin, icos, isin, gaq)


def _fattn_kernel(l_ref, bq_ref, bka_ref, bv_ref, gq_ref, out_ref, qa_ref, m_ref, s_ref, acc_ref, st_ref):
    i = pl.program_id(1)
    fb = bq_ref.shape[0]
    ones3 = jnp.where(lax.broadcasted_iota(i32, (fb, LANES), 1) < 3, 1.0, 0.0).astype(bf16)
    for h in range(B_HEADS):
        sl = slice(h * LANES, (h + 1) * LANES)
        qa_ref[h, :, 0:LANES] = _rms_gain(bq_ref[:, sl].astype(f32), gq_ref[...]).astype(bf16)
        qa_ref[h, :, LANES:2 * LANES] = ones3
    m_ref[...] = jnp.full(m_ref.shape, NEG, f32)
    s_ref[...] = jnp.zeros(s_ref.shape, f32)
    acc_ref[...] = jnp.zeros(acc_ref.shape, f32)
    c2 = HEAD_DIM ** -0.5 * LOG2E
    causal = lax.broadcasted_iota(i32, (fb, fb), 0) <= lax.broadcasted_iota(i32, (fb, fb), 1)

    def step(kb, diagonal):
        k0 = pl.multiple_of(kb * fb, fb)
        m_news = []
        for h in range(B_HEADS):
            st = _dot_nt(bka_ref[h, pl.ds(k0, fb), :], qa_ref[h])
            if diagonal:
                st = jnp.where(causal, st, NEG)
            st_ref[h] = st
            m_news.append(jnp.maximum(m_ref[h], jnp.max(st, axis=0, keepdims=True)))
        for h in range(B_HEADS):
            sl = slice(h * LANES, (h + 1) * LANES)
            st = st_ref[h]
            m_old = m_ref[h]
            m_new = m_news[h]
            alpha = jnp.exp2((m_old - m_new) * c2)
            p = jnp.exp2((st - m_new) * c2)
            s_ref[h] = s_ref[h] * alpha + jnp.sum(p, axis=0, keepdims=True)
            m_ref[h] = m_new
            acc_ref[h] = acc_ref[h] * alpha + _dot_tn(bv_ref[pl.ds(k0, fb), sl], p.astype(bf16))

    def body(kb, carry):
        step(kb, False)
        return carry

    lax.fori_loop(0, i, body, 0)
    step(i, True)
    for h in range(B_HEADS):
        sl = slice(h * LANES, (h + 1) * LANES)
        out_ref[:, sl] = (acc_ref[h] / s_ref[h]).T.astype(bf16)


def _fattn(l, zcat3, bka, gbq):
    bsz, tp, _ = zcat3.shape
    fb = _seq_tile(tp, 384)
    once = pl.Buffered(1)
    return pl.pallas_call(
        _fattn_kernel,
        grid_spec=pltpu.PrefetchScalarGridSpec(
            num_scalar_prefetch=1, grid=(bsz, tp // fb),
            in_specs=[
                pl.BlockSpec((None, fb, B_WIDTH), lambda b, i, l: (b, i, T_BQ * LANES // B_WIDTH)),
                pl.BlockSpec((None, B_HEADS, tp, 2 * LANES), lambda b, i, l: (b, 0, 0, 0), pipeline_mode=once),
                pl.BlockSpec((None, tp, B_WIDTH), lambda b, i, l: (b, 0, T_BV * LANES // B_WIDTH),
                             pipeline_mode=once),
                pl.BlockSpec((None, 1, LANES), lambda b, i, l: (l[0], 0, 0)),
            ],
            out_specs=pl.BlockSpec((None, fb, B_WIDTH), lambda b, i, l: (b, i, 0)),
            scratch_shapes=[
                pltpu.VMEM((B_HEADS, fb, 2 * LANES), bf16),
                pltpu.VMEM((B_HEADS, 1, fb), f32),
                pltpu.VMEM((B_HEADS, 1, fb), f32),
                pltpu.VMEM((B_HEADS, HEAD_DIM, fb), f32),
                pltpu.VMEM((B_HEADS, fb, fb), f32),
            ]),
        out_shape=jax.ShapeDtypeStruct((bsz, tp, B_WIDTH), bf16),
        compiler_params=_cparams(("parallel", "arbitrary")),
        name="fattn",
    )(l, zcat3, bka, zcat3, gbq)


def _gla_kernel(l_ref, cq_ref, ck_ref, cv_ref, cg_ref, zs_ref, w2_ref, gb_ref, og_ref, e_ref,
                out_ref, st_ref, oi_ref, pp_ref, rr_ref, *, t_real):
    c = pl.program_id(0)
    nb, cc, dk = cq_ref.shape

    @pl.when(c == 0)
    def _():
        st_ref[...] = jnp.zeros_like(st_ref)

    rowc = lax.broadcasted_iota(i32, (cc, 1), 0)
    pos = c * cc + rowc
    valid = (pos >= FRONT) & (pos < FRONT + t_real)
    tri = (lax.broadcasted_iota(i32, (cc, cc), 0) >= lax.broadcasted_iota(i32, (cc, cc), 1)).astype(bf16)
    lane_k = lax.broadcasted_iota(i32, (1, dk), 1)
    head_masks = [(lane_k >= h * C_DK) & (lane_k < (h + 1) * C_DK) for h in range(C_HEADS)]

    used = [(r // SUBLANES + 1) * SUBLANES for r in range(cc)]

    for b in range(nb):
        x = _dot(zs_ref[b].astype(bf16), w2_ref[...]) + gb_ref[...]
        la = jnp.where(valid, _log_sigmoid(x) * (1.0 / C_TAU), 0.0)
        hi, mid, lo = _split3(la)
        bc = _dot(tri, hi) + _dot(tri, mid) + _dot(tri, lo)
        q = cq_ref[b].astype(f32) * (C_DK ** -0.5)
        k = ck_ref[b].astype(f32)
        v = cv_ref[b]
        bc2 = bc * LOG2E
        blast2 = bc2[cc - 1:cc, :]
        qe = q * jnp.exp2(bc2)
        ke = k * jnp.exp2(blast2 - bc2)
        st = st_ref[b]
        stb = st.astype(bf16)
        oi_ref[b] = jnp.concatenate(
            [_dot_nt(jnp.where(head_masks[h], qe, 0.0).astype(bf16), stb) for h in range(C_HEADS)], axis=1)
        new_st = st * jnp.exp2(blast2)
        for h in range(C_HEADS):
            km = jnp.where(head_masks[h], ke, 0.0).astype(bf16)
            new_st = new_st + _dot_tn(v[:, h * C_DV:(h + 1) * C_DV], km)
        st_ref[b] = new_st
        for r in range(cc):
            nu = used[r]
            dec = jnp.exp2(jnp.minimum(bc2[r:r + 1, :] - bc2[0:nu, :], 0.0))
            pr = jnp.where(rowc[0:nu] <= r, q[r:r + 1, :] * k[0:nu, :] * dec, 0.0)
            if nu < cc:
                pr = jnp.concatenate([pr, jnp.zeros((cc - nu, dk), f32)], axis=0)
            pp_ref[b, r * cc:(r + 1) * cc, :] = pr.astype(bf16)

    for b in range(nb):
        rr_ref[b] = _dot(pp_ref[b], e_ref[...])

    for b in range(nb):
        vf = cv_ref[b].astype(f32)
        for r in range(cc):
            nu = used[r]
            oi_ref[b, r:r + 1, :] += jnp.sum(rr_ref[b, r * cc:r * cc + nu, :] * vf[0:nu, :], axis=0, keepdims=True)
        o = oi_ref[b]
        g = cg_ref[b].astype(f32)
        gs = g * (1.0 / (1.0 + jnp.exp(-g)))
        for h in range(C_HEADS):
            sl = slice(h * C_DV, (h + 1) * C_DV)
            out_ref[b, :, sl] = (_rms_gain(o[:, sl], og_ref[...]) * gs[:, sl]).astype(bf16)


def _gla(l, zcat3, zs3, w2p, gb, og, emat, t_real):
    bsz, tp, _ = zcat3.shape
    cc = GLA_C
    dk = C_HEADS * C_DK
    return pl.pallas_call(
        functools.partial(_gla_kernel, t_real=t_real),
        grid_spec=pltpu.PrefetchScalarGridSpec(
            num_scalar_prefetch=1, grid=(tp // cc,),
            in_specs=[
                pl.BlockSpec((bsz, cc, dk), lambda c, l: (0, c, T_CQ * LANES // dk)),
                pl.BlockSpec((bsz, cc, dk), lambda c, l: (0, c, T_CK * LANES // dk)),
                pl.BlockSpec((bsz, cc, C_WIDTH), lambda c, l: (0, c, T_CV * LANES // C_WIDTH)),
                pl.BlockSpec((bsz, cc, C_WIDTH), lambda c, l: (0, c, T_CG * LANES // C_WIDTH)),
                pl.BlockSpec((bsz, cc, LANES), lambda c, l: (0, c, 0)),
                pl.BlockSpec((None, LANES, dk), lambda c, l: (l[0], 0, 0)),
                pl.BlockSpec((None, 1, dk), lambda c, l: (l[0], 0, 0)),
                pl.BlockSpec((None, 1, C_DV), lambda c, l: (l[0], 0, 0)),
                pl.BlockSpec((dk, C_WIDTH), lambda c, l: (0, 0)),
            ],
            out_specs=pl.BlockSpec((bsz, cc, C_WIDTH), lambda c, l: (0, c, 0)),
            scratch_shapes=[pltpu.VMEM((bsz, C_DV, dk), f32), pltpu.VMEM((bsz, cc, C_WIDTH), f32),
                            pltpu.VMEM((bsz, cc * cc, dk), bf16), pltpu.VMEM((bsz, cc * cc, C_WIDTH), f32)]),
        out_shape=jax.ShapeDtypeStruct((bsz, tp, C_WIDTH), bf16),
        compiler_params=_cparams(("arbitrary",)),
        name="gla",
    )(l, zcat3, zcat3, zcat3, zcat3, zs3, w2p, gb, og, emat)


def _outproj_kernel(l_ref, oa_ref, ob_ref, oc_ref, h_ref, wo_ref, g_ref, wr_ref, br_ref,
                    hmid_ref, u_ref, eid_ref, ew_ref, *, t_real):
    tm = h_ref.shape[0]
    mix = (_dot(oa_ref[...], wo_ref[0:A_WIDTH, :])
           + _dot(ob_ref[...], wo_ref[A_WIDTH:A_WIDTH + B_WIDTH, :])
           + _dot(oc_ref[...], wo_ref[A_WIDTH + B_WIDTH:, :]))
    pos = pl.program_id(1) * tm + lax.broadcasted_iota(i32, (tm, 1), 0)
    valid = (pos >= FRONT) & (pos < FRONT + t_real)
    hm = h_ref[...] + jnp.where(valid, mix, 0.0)
    hmid_ref[...] = hm
    u = _rms_gain(hm, g_ref[...]).astype(bf16)
    u_ref[...] = u

    logits = _dot(u, wr_ref[...]) + br_ref[...]
    lane = lax.broadcasted_iota(i32, (tm, LANES), 1)
    lanef = lane.astype(f32)
    big = float(4 * LANES)
    first = lambda hit: jnp.min(jnp.where(hit, lanef, big), axis=-1, keepdims=True).astype(i32)
    gl = jnp.where(lane < R_GRP + N_GROUPS, logits, -jnp.inf)
    gmax = jnp.max(gl, axis=-1, keepdims=True)
    g_p = 1.0 / jnp.sum(jnp.exp(gl - gmax), axis=-1, keepdims=True)
    g_i = first(gl == gmax)
    e_lane = lane - R_EXP
    emask = (e_lane >= 0) & (e_lane < N_EXPERTS) & ((e_lane >> 3) == g_i)
    el = jnp.where(emask, logits, -jnp.inf)
    m1 = jnp.max(el, axis=-1, keepdims=True)
    i1 = first(el == m1)
    el2 = jnp.where(lane == i1, -jnp.inf, el)
    m2 = jnp.max(el2, axis=-1, keepdims=True)
    i2 = first(el2 == m2)
    r = jnp.exp(m2 - m1)
    w1 = g_p / (1.0 + r)
    w2 = g_p * r / (1.0 + r)
    eid_ref[...] = jnp.where(lane == 0, i1 - R_EXP, jnp.where(lane == 1, i2 - R_EXP, 0))
    ew_ref[...] = jnp.where(lane == 0, w1, jnp.where(lane == 1, w2, 0.0))


def _seq_tile(tp, cap):
    return max(t for t in range(BLK, cap + 1, BLK) if tp % t == 0)


def _outproj(l, oa, ob, oc, h, wo, gain, wr, br, t_real, tp):
    n, d = h.shape
    tm = _seq_tile(tp, 384)
    nj = tp // tm
    row = lambda w: pl.BlockSpec((tm, w), lambda b, j, l: (b * nj + j, 0))
    return pl.pallas_call(
        functools.partial(_outproj_kernel, t_real=t_real),
        grid_spec=pltpu.PrefetchScalarGridSpec(
            num_scalar_prefetch=1, grid=(n // tp, nj),
            in_specs=[row(A_WIDTH), row(B_WIDTH), row(C_WIDTH), row(d),
                      pl.BlockSpec((None, d, d), lambda b, j, l: (l[0], 0, 0)),
                      pl.BlockSpec((None, 1, d), lambda b, j, l: (l[0], 0, 0)),
                      pl.BlockSpec((None, d, LANES), lambda b, j, l: (l[0], 0, 0)),
                      pl.BlockSpec((None, 1, LANES), lambda b, j, l: (l[0], 0, 0))],
            out_specs=[row(d), row(d), row(LANES), row(LANES)]),
        out_shape=[jax.ShapeDtypeStruct((n, d), f32), jax.ShapeDtypeStruct((n, d), bf16),
                   jax.ShapeDtypeStruct((n, LANES), i32), jax.ShapeDtypeStruct((n, LANES), f32)],
        compiler_params=_cparams(("parallel", "parallel")),
        name="outproj",
    )(l, oa, ob, oc, h, wo, gain, wr, br)


def _moe_kernel(l_ref, te_ref, na_ref, xs_ref, w1_ref, w3_ref, w2_ref, rw_ref, ys_ref,
                w1b_ref, w3b_ref, w2b_ref):
    i = pl.program_id(0)

    @pl.when((i == 0) | (te_ref[i] != te_ref[jnp.maximum(i - 1, 0)]))
    def _():
        w1b_ref[...] = w1_ref[...].astype(bf16)
        w3b_ref[...] = w3_ref[...].astype(bf16)
        w2b_ref[...] = w2_ref[...].astype(bf16)

    @pl.when(i < na_ref[0])
    def _():
        x = xs_ref[...]
        h1 = _dot(x, w1b_ref[...])
        h3 = _dot(x, w3b_ref[...])
        hid = (h1 * (1.0 / (1.0 + jnp.exp(-h1))) * h3).astype(bf16)
        ys_ref[...] = (_dot(hid, w2b_ref[...]) * rw_ref[...]).astype(bf16)

    @pl.when(i >= na_ref[0])
    def _():
        ys_ref[...] = jnp.zeros_like(ys_ref)


def _moe(l, te, na, xs, w1, w3, w2, rw):
    p, d = xs.shape
    tm = MOE_TM
    return pl.pallas_call(
        _moe_kernel,
        grid_spec=pltpu.PrefetchScalarGridSpec(
            num_scalar_prefetch=3, grid=(p // tm,),
            in_specs=[
                pl.BlockSpec((tm, d), lambda i, l, te, na: (i, 0)),
                pl.BlockSpec((None, None, d, D_EXPERT), lambda i, l, te, na: (l[0], te[i], 0, 0)),
                pl.BlockSpec((None, None, d, D_EXPERT), lambda i, l, te, na: (l[0], te[i], 0, 0)),
                pl.BlockSpec((None, None, D_EXPERT, d), lambda i, l, te, na: (l[0], te[i], 0, 0)),
                pl.BlockSpec((tm, 1), lambda i, l, te, na: (i, 0)),
            ],
            out_specs=pl.BlockSpec((tm, d), lambda i, l, te, na: (i, 0)),
            scratch_shapes=[pltpu.VMEM((d, D_EXPERT), bf16), pltpu.VMEM((d, D_EXPERT), bf16),
                            pltpu.VMEM((D_EXPERT, d), bf16)]),
        out_shape=jax.ShapeDtypeStruct((p, d), bf16),
        compiler_params=_cparams(("arbitrary",)),
        name="moe",
    )(l, te, na, xs, w1, w3, w2, rw)


def _route(eid, ew, n_tiles):
    tm = MOE_TM
    p = n_tiles * tm
    n = eid.shape[0]
    e_flat = eid[:, :EXP_TOPK_CONST].T.reshape(-1)
    w_flat = ew[:, :EXP_TOPK_CONST].T.reshape(-1)
    na_all = e_flat.shape[0]
    hp = lax.Precision.HIGHEST
    ex = jnp.arange(N_EXPERTS, dtype=i32)[:, None]
    onehot = (ex == e_flat[None, :]).astype(f32)
    counts = jnp.sum(onehot, axis=1).astype(i32)
    pc = ((counts + tm - 1) // tm) * tm
    pend = jnp.cumsum(pc)
    po = pend - pc
    co = jnp.cumsum(counts) - counts
    order = jnp.argsort(e_flat, stable=True).astype(i32)
    r = jnp.arange(p, dtype=i32)
    step = (r[None, :] >= pend[:, None]).astype(f32)
    dlt = lambda v: jnp.concatenate([v[1:] - v[:-1], jnp.zeros((1,), v.dtype)]).astype(f32)
    tabs = jnp.stack([jnp.ones((N_EXPERTS,), f32), dlt(po), dlt(counts), dlt(co)])
    picked = jnp.dot(tabs, step, precision=hp).astype(i32)
    e_r = jnp.minimum(picked[0], N_EXPERTS - 1)
    local = r - (po[0] + picked[1])
    valid_r = (local < counts[0] + picked[2]) & (r < pend[-1])
    a_r = order[jnp.clip(co[0] + picked[3] + local, 0, na_all - 1)]
    tok_r = jnp.where(valid_r, jnp.where(a_r >= n, a_r - n, a_r), r % n)
    w_r = jnp.where(valid_r, w_flat[a_r], 0.0)
    n_act = (pend[-1] // tm).astype(i32)
    tile_e = e_r[::tm]
    te = jnp.where(jnp.arange(n_tiles, dtype=i32) < n_act, tile_e, tile_e[jnp.maximum(n_act - 1, 0)])
    inv = jnp.argsort(order).astype(i32)
    pos_a = jnp.dot((po - co).astype(f32)[None, :], onehot, precision=hp)[0].astype(i32) + inv
    return tok_r, w_r[:, None], te, n_act.reshape(1), pos_a


def _final_kernel(hmid_ref, ya_ref, yb_ref, out_ref):
    out_ref[...] = hmid_ref[...] + ya_ref[...].astype(f32) + yb_ref[...].astype(f32)


def _final(hmid, y2):
    n, d = hmid.shape
    tm = _row_tile(n, 512)
    nt = n // tm
    return pl.pallas_call(
        _final_kernel, grid=(nt,),
        in_specs=[pl.BlockSpec((tm, d), lambda i: (i, 0)), pl.BlockSpec((tm, d), lambda i: (i, 0)),
                  pl.BlockSpec((tm, d), lambda i: (i + nt, 0))],
        out_specs=pl.BlockSpec((tm, d), lambda i: (i, 0)),
        out_shape=jax.ShapeDtypeStruct((n, d), f32),
        compiler_params=_cparams(("parallel",)),
        name="final",
    )(hmid, y2, y2)


def _rope_tables(tp):
    pos = (jnp.arange(tp, dtype=f32) - FRONT)[:, None]

    def tab(dim, reps):
        inv = 1.0 / (ROPE_THETA ** (jnp.arange(0, dim, 2, dtype=f32) / dim))
        ang = pos * inv[None, :]
        return jnp.tile(jnp.cos(ang), (1, reps)), jnp.tile(jnp.sin(ang), (1, reps))

    cos, sin = tab(HEAD_DIM, 2)
    icos, isin = tab(IDX_DIM, 4)
    return cos, sin, icos, isin


def _split_w_in(w_in):
    parts, off = [], 0
    for s in IN_SPLITS:
        parts.append(w_in[..., off:off + s])
        off += s
    return parts


def kernel(x, meta, norm_mix_g, norm_ffn_g, w_in, a_qn_g, a_kn_g, b_qn_g, b_kn_g, b_f_bias,
           c_gate_w2, c_gate_b, c_on_g, w_out, r_group_w, r_group_b, r_exp_w, r_exp_b,
           e_w1, e_w3, e_w2):
    bsz, n_seq, d = x.shape
    depth = w_in.shape[0]
    t_real = n_seq + N_META
    k_top = min(TOPK_MAX, n_seq // 4)
    tp = -(-(FRONT + t_real) // BLK) * BLK
    n = bsz * tp

    (waq, wak, wav, wiq, wik, wiw, wbq, wbk, wbv, wbf, wcq, wck, wcv, wcr, wcg) = _split_w_in(w_in)
    zc = lambda w: jnp.zeros((depth, d, w), w_in.dtype)
    wcat = jnp.concatenate([waq, wak, wav, wiq, wik, zc(LANES - IDX_DIM), zc(LANES),
                            wbq, wbk, wbv, wcq, wck, wcv, wcg], axis=-1).astype(bf16)
    ws = jnp.concatenate([wiw, wbf, wcr, zc(LANES - S_CR - C_GATE_RANK)], axis=-1).astype(bf16)
    wo = w_out.astype(bf16)
    wr = jnp.concatenate([r_group_w, zc(R_EXP - N_GROUPS), r_exp_w, zc(LANES - R_EXP - N_EXPERTS)],
                         axis=-1).astype(bf16)
    zl = lambda w: jnp.zeros((depth, w), f32)
    br = jnp.concatenate([r_group_b, zl(R_EXP - N_GROUPS), r_exp_b, zl(LANES - R_EXP - N_EXPERTS)],
                         axis=-1)[:, None, :]
    fbias = jnp.concatenate([zl(S_BF), b_f_bias, zl(LANES - S_BF - B_HEADS)], axis=-1)[:, None, :]
    dk = C_HEADS * C_DK
    w2p = jnp.concatenate([jnp.zeros((depth, S_CR, dk), f32), c_gate_w2,
                           jnp.zeros((depth, LANES - S_CR - C_GATE_RANK, dk), f32)], axis=1).astype(bf16)
    gb = c_gate_b[:, None, :]
    og = c_on_g[:, None, :]
    g_mix = norm_mix_g[:, None, :]
    g_ffn = norm_ffn_g[:, None, :]
    gaq, gak, gbq, gbk = (g[:, None, :] for g in (a_qn_g, a_kn_g, b_qn_g, b_kn_g))
    emat = (jnp.arange(dk, dtype=i32)[:, None] // C_DK
            == jnp.arange(C_WIDTH, dtype=i32)[None, :] // C_DV).astype(bf16)
    tabs = _rope_tables(tp)

    h0 = jnp.concatenate([
        jnp.zeros((bsz, FRONT, d), f32),
        jnp.broadcast_to(meta[None].astype(f32), (bsz, N_META, d)),
        x.astype(f32),
        jnp.zeros((bsz, tp - FRONT - t_real, d), f32)], axis=1).reshape(n, d)
    n_tiles = -(-(EXP_TOPK_CONST * n + N_EXPERTS * (MOE_TM - 1)) // MOE_TM)

    def layer(li, carry):
        hmid, y2 = carry
        l = jnp.reshape(jnp.asarray(li, i32), (1,))
        h, zcat, zs = _inproj(l, hmid, y2, g_mix, wcat, ws)
        zcat3 = zcat.reshape(bsz, tp, N_CAT)
        zs3 = zs.reshape(bsz, tp, LANES)
        akr, ikr, bka = _kprep(l, zcat3, zs3, tabs, gak, gbk, fbias, t_real)
        oa = _dsa(l, zcat3, zs3, akr, ikr, tabs, gaq, t_real, k_top)
        ob = _fattn(l, zcat3, bka, gbq)
        oc = _gla(l, zcat3, zs3, w2p, gb, og, emat, t_real)
        hmid2, u, eid, ew = _outproj(l, oa.reshape(n, A_WIDTH), ob.reshape(n, B_WIDTH),
                                     oc.reshape(n, C_WIDTH), h, wo, g_ffn, wr, br, t_real, tp)
        tok_r, w_r, te, n_act, pos_a = _route(eid, ew, n_tiles)
        ys = _moe(l, te, n_act, jnp.take(u, tok_r, axis=0, mode="clip"), e_w1, e_w3, e_w2, w_r)
        return hmid2, jnp.take(ys, pos_a, axis=0, mode="clip")

    y0 = lax.optimization_barrier(jnp.zeros((EXP_TOPK_CONST * n, d), bf16))
    hmid, y2 = lax.fori_loop(0, depth, layer, (h0, y0))
    out = _final(hmid, y2).reshape(bsz, tp, d)
    return out[:, FRONT + N_META:FRONT + t_real].astype(x.dtype)
```

```python
import functools

import jax
import jax.numpy as jnp
from jax import lax
from jax.experimental import pallas as pl
from jax.experimental.pallas import tpu as pltpu

f32 = jnp.float32
bf16 = jnp.bfloat16
i32 = jnp.int32

D_MODEL = 2048
CHUNK = 64
N_META = 16
ROPE_THETA = 10000.0
EPS = 1e-6
HEAD_DIM = 128
A_HEADS = 6
IDX_HEADS = 16
IDX_DIM = 64
TOPK_MAX = 256
B_HEADS = 6
C_HEADS = 4
C_DK = 64
C_DV = 128
C_GATE_RANK = 16
C_TAU = 16.0
N_GROUPS = 4
EXP_PER_GROUP = 8
N_EXPERTS = N_GROUPS * EXP_PER_GROUP
D_EXPERT = 512
A_WIDTH = A_HEADS * HEAD_DIM
B_WIDTH = B_HEADS * HEAD_DIM
C_WIDTH = C_HEADS * C_DV
IN_SPLITS = (A_WIDTH, HEAD_DIM, HEAD_DIM, IDX_HEADS * IDX_DIM, IDX_DIM, IDX_HEADS,
             B_WIDTH, B_WIDTH, B_WIDTH, B_HEADS,
             C_HEADS * C_DK, C_HEADS * C_DK, C_WIDTH, C_GATE_RANK, C_WIDTH)

LANES = 128
SUBLANES = 8
FRONT = (-N_META) % CHUNK
BLK = 128
T_AQ, T_AK, T_AV, T_IQ, T_IK = 0, 6, 7, 8, 16
T_BQ, T_BK, T_BV = 18, 24, 30
T_CQ, T_CK, T_CV, T_CG = 36, 38, 40, 44
N_CAT = 48 * LANES
S_IW, S_BF, S_CR = 0, 16, 22
R_GRP, R_EXP = 0, 32
INPROJ_TN = 1536
GLA_C = 32
DSA_G = 3
MOE_TM = 256
EXP_TOPK_CONST = 2
NEG = -1e30
LOG2E = 1.4426950408889634
INT_MIN = -2 ** 31
VMEM_LIMIT = 56 * 1024 * 1024


def _cparams(sem):
    return pltpu.CompilerParams(dimension_semantics=sem, vmem_limit_bytes=VMEM_LIMIT)


def _row_tile(n, cap):
    t = cap
    while n % t:
        t //= 2
    return t


def _log_sigmoid(x):
    return jnp.minimum(x, 0.0) - jnp.log(1.0 + jnp.exp(-jnp.abs(x)))


def _dot(a, b):
    return jnp.dot(a, b, preferred_element_type=f32)


def _dot_nt(a, b):
    return lax.dot_general(a, b, (((1,), (1,)), ((), ())), preferred_element_type=f32)


def _dot_tn(a, b):
    return lax.dot_general(a, b, (((0,), (0,)), ((), ())), preferred_element_type=f32)


def _split3(x):
    hi = x.astype(bf16)
    r1 = x - hi.astype(f32)
    mid = r1.astype(bf16)
    lo = (r1 - mid.astype(f32)).astype(bf16)
    return hi, mid, lo


def _rope128(x, cos, sin, lane):
    return x * cos + pltpu.roll(x, 64, 1) * jnp.where(lane < 64, -sin, sin)


def _rope64(x, cos, sin, lane):
    low = (lane & 63) < 32
    return (x * cos + pltpu.roll(x, 32, 1) * jnp.where(low, 0.0, sin)
            + pltpu.roll(x, 96, 1) * jnp.where(low, -sin, 0.0))


def _rms_gain(x, g):
    return x * lax.rsqrt(jnp.mean(x * x, axis=-1, keepdims=True) + EPS) * g


def _prenorm_kernel(l_ref, hmid_ref, ya_ref, yb_ref, g_ref, ws_ref, h_ref, xn_ref, zs_ref):
    h = hmid_ref[...] + ya_ref[...].astype(f32) + yb_ref[...].astype(f32)
    h_ref[...] = h
    xn = _rms_gain(h, g_ref[...]).astype(bf16)
    xn_ref[...] = xn
    zs_ref[...] = _dot(xn, ws_ref[...])


def _inproj_kernel(l_ref, xn_ref, w_ref, z_ref):
    z_ref[...] = _dot(xn_ref[...], w_ref[...]).astype(bf16)


def _inproj(l, hmid, y2, gain, wcat, ws):
    n, d = hmid.shape
    tm = _row_tile(n, 512)
    nt = n // tm
    row = lambda w: pl.BlockSpec((tm, w), lambda i, l: (i, 0))
    h, xn, zs = pl.pallas_call(
        _prenorm_kernel,
        grid_spec=pltpu.PrefetchScalarGridSpec(
            num_scalar_prefetch=1, grid=(nt,),
            in_specs=[row(d), row(d),
                      pl.BlockSpec((tm, d), lambda i, l: (i + nt, 0)),
                      pl.BlockSpec((None, 1, d), lambda i, l: (l[0], 0, 0)),
                      pl.BlockSpec((None, d, LANES), lambda i, l: (l[0], 0, 0))],
            out_specs=[row(d), row(d), row(LANES)]),
        out_shape=[jax.ShapeDtypeStruct((n, d), f32), jax.ShapeDtypeStruct((n, d), bf16),
                   jax.ShapeDtypeStruct((n, LANES), f32)],
        compiler_params=_cparams(("parallel",)),
        name="prenorm",
    )(l, hmid, y2, y2, gain, ws)
    tn = INPROJ_TN
    zcat = pl.pallas_call(
        _inproj_kernel,
        grid_spec=pltpu.PrefetchScalarGridSpec(
            num_scalar_prefetch=1, grid=(N_CAT // tn, nt),
            in_specs=[pl.BlockSpec((tm, d), lambda j, i, l: (i, 0)),
                      pl.BlockSpec((None, d, tn), lambda j, i, l: (l[0], 0, j))],
            out_specs=pl.BlockSpec((tm, tn), lambda j, i, l: (i, j))),
        out_shape=jax.ShapeDtypeStruct((n, N_CAT), bf16),
        compiler_params=_cparams(("parallel", "parallel")),
        name="inproj",
    )(l, xn, wcat)
    return h, zcat, zs


def _kprep_kernel(l_ref, ak_ref, ik_ref, bk_ref, zs_ref,
                  cos_ref, sin_ref, icos_ref, isin_ref, gak_ref, gbk_ref, fb_ref,
                  akr_ref, ikr_ref, bka_ref, carry_ref, *, t_real):
    k = pl.program_id(1)
    lane = lax.broadcasted_iota(i32, (BLK, LANES), 1)
    row = lax.broadcasted_iota(i32, (BLK, LANES), 0)
    pos = k * BLK + row
    valid = (pos >= FRONT) & (pos < FRONT + t_real)

    x = ak_ref[...].astype(f32)
    akr_ref[...] = _rope128(_rms_gain(x, gak_ref[...]), cos_ref[...], sin_ref[...], lane).astype(bf16)
    ikr_ref[...] = _rope64(ik_ref[...].astype(f32), icos_ref[...], isin_ref[...], lane).astype(bf16)

    @pl.when(k == 0)
    def _():
        carry_ref[...] = jnp.zeros_like(carry_ref)

    lf = jnp.where(valid, _log_sigmoid(zs_ref[...] + fb_ref[...]), 0.0)
    tri = (row >= lane).astype(bf16)
    hi, mid, lo = _split3(lf)
    fcum = _dot(tri, hi) + _dot(tri, mid) + _dot(tri, lo) + carry_ref[...]
    carry_ref[...] = fcum[BLK - 1:BLK, :]

    fs = jnp.where(valid, fcum * (-(HEAD_DIM ** 0.5)), NEG)
    for h in range(B_HEADS):
        sl = slice(h * LANES, (h + 1) * LANES)
        bka_ref[h, :, 0:LANES] = _rms_gain(bk_ref[:, sl].astype(f32), gbk_ref[...]).astype(bf16)
        p0, p1, p2 = _split3(fs[:, S_BF + h:S_BF + h + 1])
        aug = jnp.where(lane == 0, p0.astype(f32),
                        jnp.where(lane == 1, p1.astype(f32), jnp.where(lane == 2, p2.astype(f32), 0.0)))
        bka_ref[h, :, LANES:2 * LANES] = aug.astype(bf16)


def _kprep(l, zcat3, zs3, tabs, gak, gbk, fbias, t_real):
    bsz, tp, _ = zcat3.shape
    nkb = tp // BLK
    cos, sin, icos, isin = tabs
    tile = lambda c: pl.BlockSpec((None, BLK, LANES), lambda b, k, l, c=c: (b, k, c))
    wide = lambda c: pl.BlockSpec((None, BLK, B_WIDTH), lambda b, k, l, c=c: (b, k, c))
    tab = pl.BlockSpec((BLK, LANES), lambda b, k, l: (k, 0))
    gain = pl.BlockSpec((None, 1, LANES), lambda b, k, l: (l[0], 0, 0))
    return pl.pallas_call(
        functools.partial(_kprep_kernel, t_real=t_real),
        grid_spec=pltpu.PrefetchScalarGridSpec(
            num_scalar_prefetch=1, grid=(bsz, nkb),
            in_specs=[tile(T_AK), tile(T_IK),
                      wide(T_BK * LANES // B_WIDTH),
                      pl.BlockSpec((None, BLK, LANES), lambda b, k, l: (b, k, 0)),
                      tab, tab, tab, tab, gain, gain, gain],
            out_specs=[
                pl.BlockSpec((None, BLK, LANES), lambda b, k, l: (b, k, 0)),
                pl.BlockSpec((None, BLK, LANES), lambda b, k, l: (b, k, 0)),
                pl.BlockSpec((None, B_HEADS, BLK, 2 * LANES), lambda b, k, l: (b, 0, k, 0)),
            ],
            scratch_shapes=[pltpu.VMEM((1, LANES), f32)]),
        out_shape=[jax.ShapeDtypeStruct((bsz, tp, LANES), bf16),
                   jax.ShapeDtypeStruct((bsz, tp, LANES), bf16),
                   jax.ShapeDtypeStruct((bsz, B_HEADS, tp, 2 * LANES), bf16)],
        compiler_params=_cparams(("parallel", "arbitrary")),
        name="kprep",
    )(l, zcat3, zcat3, zcat3, zs3, cos, sin, icos, isin, gak, gbk, fbias)


def _dsa_kernel(l_ref, aq_ref, iq_ref, zs_ref, akr_ref, ikr_ref, av_ref,
                cos_ref, sin_ref, icos_ref, isin_ref, gq_ref, out_ref,
                key_ref, iqs_ref, q6_ref, iwt_ref, m_ref, s_ref, acc_ref, stq_ref, *, t_real, k_top):
    i = pl.program_id(1)
    nk = i + 1
    lane = lax.broadcasted_iota(i32, (BLK, LANES), 1)

    for h in range(A_HEADS):
        x = aq_ref[:, h * LANES:(h + 1) * LANES].astype(f32)
        xr = _rope128(_rms_gain(x, gq_ref[...]), cos_ref[...], sin_ref[...], lane)
        q6_ref[h * BLK:(h + 1) * BLK, :] = xr.astype(bf16)
    for t in range(IDX_HEADS // 2):
        x = iq_ref[:, t * LANES:(t + 1) * LANES].astype(f32)
        xr = _rope64(x, icos_ref[...], isin_ref[...], lane)
        iqs_ref[(2 * t) * BLK:(2 * t + 1) * BLK, :] = xr.astype(bf16)
        iqs_ref[(2 * t + 1) * BLK:(2 * t + 2) * BLK, :] = pltpu.roll(xr, 64, 1).astype(bf16)
    iwt_ref[...] = (zs_ref[...] * (IDX_HEADS ** -0.5 * IDX_DIM ** -0.5)).T

    gb = DSA_G * BLK
    ng = lax.div(nk, jnp.int32(DSA_G))
    nr = nk - ng * DSA_G

    def over_keys(fn, init, merge=()):
        c, g0 = init, 0
        for m in merge:
            nm = lax.div(ng - g0, jnp.int32(m))
            c = lax.fori_loop(0, nm, lambda g, c, g0=g0, m=m: fn(pl.multiple_of((g0 + g * m) * gb, gb), m * gb, c), c)
            g0 = g0 + nm * m
        c = lax.fori_loop(g0, ng, lambda g, c: fn(pl.multiple_of(g * gb, gb), gb, c), c)
        return lax.fori_loop(0, nr, lambda r, c: fn(pl.multiple_of((ng * DSA_G + r) * BLK, BLK), BLK, c), c)

    def score_rows(k0, nrows, carry):
        dt = _dot_nt(ikr_ref[pl.ds(k0, nrows), :], iqs_ref[...])
        s = jnp.zeros((nrows, LANES), f32)
        for h in range(IDX_HEADS):
            s = s + iwt_ref[h:h + 1, :] * jnp.maximum(dt[:, h * LANES:(h + 1) * LANES], 0.0)
        kpos = k0 + lax.broadcasted_iota(i32, (nrows, LANES), 0)
        qpos = i * BLK + lax.broadcasted_iota(i32, (nrows, LANES), 1)
        adm = ((kpos >> 6) <= (qpos >> 6)) & (kpos >= FRONT) & (kpos < FRONT + t_real)
        bits = lax.bitcast_convert_type(s, i32)
        key = bits ^ ((bits >> 31) & 0x7FFFFFFF)
        key_ref[pl.ds(k0, nrows), :] = jnp.where(adm, key, INT_MIN)
        return carry

    over_keys(score_rows, 0, merge=(2,))

    def bit_body(t, thr_u):
        bit = jnp.left_shift(jnp.int32(1), 31 - t)
        cand_u = thr_u | bit
        cand_s = cand_u ^ INT_MIN

        def count_rows(k0, nrows, c):
            hit = (key_ref[pl.ds(k0, nrows), :] >= cand_s).astype(i32)
            for j in range(nrows // BLK):
                c = c + hit[j * BLK:(j + 1) * BLK, :]
            return c

        cnt = over_keys(count_rows, jnp.zeros((BLK, LANES), i32))
        tot = jnp.sum(cnt.astype(f32), axis=0, keepdims=True)
        return jnp.where(tot >= k_top, cand_u, thr_u)

    thr_u = lax.fori_loop(0, 32, bit_body, jnp.zeros((1, LANES), i32))
    thr_s = jnp.maximum(thr_u ^ INT_MIN, INT_MIN + 1)

    m_ref[...] = jnp.full(m_ref.shape, NEG, f32)
    s_ref[...] = jnp.zeros(s_ref.shape, f32)
    acc_ref[...] = jnp.zeros(acc_ref.shape, f32)
    c2 = HEAD_DIM ** -0.5 * LOG2E

    def attn_rows(k0, nrows, carry, st=None):
        if st is None:
            st = _dot_nt(akr_ref[pl.ds(k0, nrows), :], q6_ref[...])
        bias = jnp.where(key_ref[pl.ds(k0, nrows), :] >= thr_s, 0.0, NEG)
        m_news = []
        for h in range(A_HEADS):
            sl = slice(h * LANES, (h + 1) * LANES)
            sh = st[:, sl] + bias
            stq_ref[0, 0:nrows, sl] = sh
            m_news.append(jnp.maximum(m_ref[:, sl], jnp.max(sh, axis=0, keepdims=True)))
        ps, alphas = [], []
        for h in range(A_HEADS):
            sl = slice(h * LANES, (h + 1) * LANES)
            sh = stq_ref[0, 0:nrows, sl]
            m_old = m_ref[:, sl]
            m_new = m_news[h]
            alpha = jnp.exp2((m_old - m_new) * c2)
            p = jnp.exp2((sh - m_new) * c2)
            s_ref[:, sl] = s_ref[:, sl] * alpha + jnp.sum(p, axis=0, keepdims=True)
            m_ref[:, sl] = m_new
            alphas.append(alpha)
            ps.append(p.astype(bf16))
        pv = _dot_tn(av_ref[pl.ds(k0, nrows), :], jnp.concatenate(ps, axis=1))
        acc_ref[...] = acc_ref[...] * jnp.concatenate(alphas, axis=1) + pv
        return carry

    over_keys(attn_rows, 0, merge=(4, 2))

    for h in range(A_HEADS):
        sl = slice(h * LANES, (h + 1) * LANES)
        o = acc_ref[:, sl] / jnp.maximum(s_ref[:, sl], 1e-30)
        out_ref[:, sl] = o.T.astype(bf16)


def _dsa(l, zcat3, zs3, akr, ikr, tabs, gaq, t_real, k_top):
    bsz, tp, _ = zcat3.shape
    nkb = tp // BLK
    cos, sin, icos, isin = tabs
    tab = pl.BlockSpec((BLK, LANES), lambda b, i, l: (i, 0))
    full = pl.BlockSpec((None, tp, LANES), lambda b, i, l: (b, 0, 0))
    return pl.pallas_call(
        functools.partial(_dsa_kernel, t_real=t_real, k_top=k_top),
        grid_spec=pltpu.PrefetchScalarGridSpec(
            num_scalar_prefetch=1, grid=(bsz, nkb),
            in_specs=[
                pl.BlockSpec((None, BLK, A_WIDTH), lambda b, i, l: (b, i, 0)),
                pl.BlockSpec((None, BLK, IDX_HEADS * IDX_DIM), lambda b, i, l: (b, i, T_IQ * LANES // (IDX_HEADS * IDX_DIM))),
                pl.BlockSpec((None, BLK, LANES), lambda b, i, l: (b, i, 0)),
                full, full,
                pl.BlockSpec((None, tp, LANES), lambda b, i, l: (b, 0, T_AV)),
                tab, tab, tab, tab,
                pl.BlockSpec((None, 1, LANES), lambda b, i, l: (l[0], 0, 0)),
            ],
            out_specs=pl.BlockSpec((None, BLK, A_WIDTH), lambda b, i, l: (b, i, 0)),
            scratch_shapes=[
                pltpu.VMEM((tp, LANES), i32),
                pltpu.VMEM((IDX_HEADS * BLK, LANES), bf16),
                pltpu.VMEM((A_HEADS * BLK, LANES), bf16),
                pltpu.VMEM((LANES, BLK), f32),
                pltpu.VMEM((1, A_HEADS * BLK), f32),
                pltpu.VMEM((1, A_HEADS * BLK), f32),
                pltpu.VMEM((HEAD_DIM, A_HEADS * BLK), f32),
                pltpu.VMEM((1, 4 * DSA_G * BLK, A_HEADS * BLK), f32),
            ]),
        out_shape=jax.ShapeDtypeStruct((bsz, tp, A_WIDTH), bf16),
        compiler_params=_cparams(("parallel", "arbitrary")),
        name="dsa",
    )(l, zcat3, zcat3, zs3, akr, ikr, zcat3, cos, sin, icos, isin, gaq)


def _fattn_kernel(l_ref, bq_ref, bka_ref, bv_ref, gq_ref, out_ref, qa_ref, m_ref, s_ref, acc_ref, st_ref):
    i = pl.program_id(1)
    fb = bq_ref.shape[0]
    ones3 = jnp.where(lax.broadcasted_iota(i32, (fb, LANES), 1) < 3, 1.0, 0.0).astype(bf16)
    for h in range(B_HEADS):
        sl = slice(h * LANES, (h + 1) * LANES)
        qa_ref[h, :, 0:LANES] = _rms_gain(bq_ref[:, sl].astype(f32), gq_ref[...]).astype(bf16)
        qa_ref[h, :, LANES:2 * LANES] = ones3
    m_ref[...] = jnp.full(m_ref.shape, NEG, f32)
    s_ref[...] = jnp.zeros(s_ref.shape, f32)
    acc_ref[...] = jnp.zeros(acc_ref.shape, f32)
    c2 = HEAD_DIM ** -0.5 * LOG2E
    causal = lax.broadcasted_iota(i32, (fb, fb), 0) <= lax.broadcasted_iota(i32, (fb, fb), 1)

    def step(kb, diagonal):
        k0 = pl.multiple_of(kb * fb, fb)
        m_news = []
        for h in range(B_HEADS):
            st = _dot_nt(bka_ref[h, pl.ds(k0, fb), :], qa_ref[h])
            if diagonal:
                st = jnp.where(causal, st, NEG)
            st_ref[h] = st
            m_news.append(jnp.maximum(m_ref[h], jnp.max(st, axis=0, keepdims=True)))
        for h in range(B_HEADS):
            sl = slice(h * LANES, (h + 1) * LANES)
            st = st_ref[h]
            m_old = m_ref[h]
            m_new = m_news[h]
            alpha = jnp.exp2((m_old - m_new) * c2)
            p = jnp.exp2((st - m_new) * c2)
            s_ref[h] = s_ref[h] * alpha + jnp.sum(p, axis=0, keepdims=True)
            m_ref[h] = m_new
            acc_ref[h] = acc_ref[h] * alpha + _dot_tn(bv_ref[pl.ds(k0, fb), sl], p.astype(bf16))

    def body(kb, carry):
        step(kb, False)
        return carry

    lax.fori_loop(0, i, body, 0)
    step(i, True)
    for h in range(B_HEADS):
        sl = slice(h * LANES, (h + 1) * LANES)
        out_ref[:, sl] = (acc_ref[h] / s_ref[h]).T.astype(bf16)


def _fattn(l, zcat3, bka, gbq):
    bsz, tp, _ = zcat3.shape
    fb = _seq_tile(tp, 384)
    once = pl.Buffered(1)
    return pl.pallas_call(
        _fattn_kernel,
        grid_spec=pltpu.PrefetchScalarGridSpec(
            num_scalar_prefetch=1, grid=(bsz, tp // fb),
            in_specs=[
                pl.BlockSpec((None, fb, B_WIDTH), lambda b, i, l: (b, i, T_BQ * LANES // B_WIDTH)),
                pl.BlockSpec((None, B_HEADS, tp, 2 * LANES), lambda b, i, l: (b, 0, 0, 0), pipeline_mode=once),
                pl.BlockSpec((None, tp, B_WIDTH), lambda b, i, l: (b, 0, T_BV * LANES // B_WIDTH),
                             pipeline_mode=once),
                pl.BlockSpec((None, 1, LANES), lambda b, i, l: (l[0], 0, 0)),
            ],
            out_specs=pl.BlockSpec((None, fb, B_WIDTH), lambda b, i, l: (b, i, 0)),
            scratch_shapes=[
                pltpu.VMEM((B_HEADS, fb, 2 * LANES), bf16),
                pltpu.VMEM((B_HEADS, 1, fb), f32),
                pltpu.VMEM((B_HEADS, 1, fb), f32),
                pltpu.VMEM((B_HEADS, HEAD_DIM, fb), f32),
                pltpu.VMEM((B_HEADS, fb, fb), f32),
            ]),
        out_shape=jax.ShapeDtypeStruct((bsz, tp, B_WIDTH), bf16),
        compiler_params=_cparams(("parallel", "arbitrary")),
        name="fattn",
    )(l, zcat3, bka, zcat3, gbq)


def _gla_kernel(l_ref, cq_ref, ck_ref, cv_ref, cg_ref, zs_ref, w2_ref, gb_ref, og_ref, e_ref,
                out_ref, st_ref, oi_ref, pp_ref, rr_ref, *, t_real):
    c = pl.program_id(0)
    nb, cc, dk = cq_ref.shape

    @pl.when(c == 0)
    def _():
        st_ref[...] = jnp.zeros_like(st_ref)

    rowc = lax.broadcasted_iota(i32, (cc, 1), 0)
    pos = c * cc + rowc
    valid = (pos >= FRONT) & (pos < FRONT + t_real)
    tri = (lax.broadcasted_iota(i32, (cc, cc), 0) >= lax.broadcasted_iota(i32, (cc, cc), 1)).astype(bf16)
    lane_k = lax.broadcasted_iota(i32, (1, dk), 1)
    head_masks = [(lane_k >= h * C_DK) & (lane_k < (h + 1) * C_DK) for h in range(C_HEADS)]

    used = [(r // SUBLANES + 1) * SUBLANES for r in range(cc)]

    for b in range(nb):
        x = _dot(zs_ref[b].astype(bf16), w2_ref[...]) + gb_ref[...]
        la = jnp.where(valid, _log_sigmoid(x) * (1.0 / C_TAU), 0.0)
        hi, mid, lo = _split3(la)
        bc = _dot(tri, hi) + _dot(tri, mid) + _dot(tri, lo)
        q = cq_ref[b].astype(f32) * (C_DK ** -0.5)
        k = ck_ref[b].astype(f32)
        v = cv_ref[b]
        bc2 = bc * LOG2E
        blast2 = bc2[cc - 1:cc, :]
        qe = q * jnp.exp2(bc2)
        ke = k * jnp.exp2(blast2 - bc2)
        st = st_ref[b]
        stb = st.astype(bf16)
        oi_ref[b] = jnp.concatenate(
            [_dot_nt(jnp.where(head_masks[h], qe, 0.0).astype(bf16), stb) for h in range(C_HEADS)], axis=1)
        new_st = st * jnp.exp2(blast2)
        for h in range(C_HEADS):
            km = jnp.where(head_masks[h], ke, 0.0).astype(bf16)
            new_st = new_st + _dot_tn(v[:, h * C_DV:(h + 1) * C_DV], km)
        st_ref[b] = new_st
        for r in range(cc):
            nu = used[r]
            dec = jnp.exp2(jnp.minimum(bc2[r:r + 1, :] - bc2[0:nu, :], 0.0))
            pr = jnp.where(rowc[0:nu] <= r, q[r:r + 1, :] * k[0:nu, :] * dec, 0.0)
            if nu < cc:
                pr = jnp.concatenate([pr, jnp.zeros((cc - nu, dk), f32)], axis=0)
            pp_ref[b, r * cc:(r + 1) * cc, :] = pr.astype(bf16)

    for b in range(nb):
        rr_ref[b] = _dot(pp_ref[b], e_ref[...])

    for b in range(nb):
        vf = cv_ref[b].astype(f32)
        for r in range(cc):
            nu = used[r]
            oi_ref[b, r:r + 1, :] += jnp.sum(rr_ref[b, r * cc:r * cc + nu, :] * vf[0:nu, :], axis=0, keepdims=True)
        o = oi_ref[b]
        g = cg_ref[b].astype(f32)
        gs = g * (1.0 / (1.0 + jnp.exp(-g)))
        for h in range(C_HEADS):
            sl = slice(h * C_DV, (h + 1) * C_DV)
            out_ref[b, :, sl] = (_rms_gain(o[:, sl], og_ref[...]) * gs[:, sl]).astype(bf16)


def _gla(l, zcat3, zs3, w2p, gb, og, emat, t_real):
    bsz, tp, _ = zcat3.shape
    cc = GLA_C
    dk = C_HEADS * C_DK
    return pl.pallas_call(
        functools.partial(_gla_kernel, t_real=t_real),
        grid_spec=pltpu.PrefetchScalarGridSpec(
            num_scalar_prefetch=1, grid=(tp // cc,),
            in_specs=[
                pl.BlockSpec((bsz, cc, dk), lambda c, l: (0, c, T_CQ * LANES // dk)),
                pl.BlockSpec((bsz, cc, dk), lambda c, l: (0, c, T_CK * LANES // dk)),
                pl.BlockSpec((bsz, cc, C_WIDTH), lambda c, l: (0, c, T_CV * LANES // C_WIDTH)),
                pl.BlockSpec((bsz, cc, C_WIDTH), lambda c, l: (0, c, T_CG * LANES // C_WIDTH)),
                pl.BlockSpec((bsz, cc, LANES), lambda c, l: (0, c, 0)),
                pl.BlockSpec((None, LANES, dk), lambda c, l: (l[0], 0, 0)),
                pl.BlockSpec((None, 1, dk), lambda c, l: (l[0], 0, 0)),
                pl.BlockSpec((None, 1, C_DV), lambda c, l: (l[0], 0, 0)),
                pl.BlockSpec((dk, C_WIDTH), lambda c, l: (0, 0)),
            ],
            out_specs=pl.BlockSpec((bsz, cc, C_WIDTH), lambda c, l: (0, c, 0)),
            scratch_shapes=[pltpu.VMEM((bsz, C_DV, dk), f32), pltpu.VMEM((bsz, cc, C_WIDTH), f32),
                            pltpu.VMEM((bsz, cc * cc, dk), bf16), pltpu.VMEM((bsz, cc * cc, C_WIDTH), f32)]),
        out_shape=jax.ShapeDtypeStruct((bsz, tp, C_WIDTH), bf16),
        compiler_params=_cparams(("arbitrary",)),
        name="gla",
    )(l, zcat3, zcat3, zcat3, zcat3, zs3, w2p, gb, og, emat)


def _outproj_kernel(l_ref, oa_ref, ob_ref, oc_ref, h_ref, wo_ref, g_ref, wr_ref, br_ref,
                    hmid_ref, u_ref, eid_ref, ew_ref, *, t_real):
    tm = h_ref.shape[0]
    mix = (_dot(oa_ref[...], wo_ref[0:A_WIDTH, :])
           + _dot(ob_ref[...], wo_ref[A_WIDTH:A_WIDTH + B_WIDTH, :])
           + _dot(oc_ref[...], wo_ref[A_WIDTH + B_WIDTH:, :]))
    pos = pl.program_id(1) * tm + lax.broadcasted_iota(i32, (tm, 1), 0)
    valid = (pos >= FRONT) & (pos < FRONT + t_real)
    hm = h_ref[...] + jnp.where(valid, mix, 0.0)
    hmid_ref[...] = hm
    u = _rms_gain(hm, g_ref[...]).astype(bf16)
    u_ref[...] = u

    logits = _dot(u, wr_ref[...]) + br_ref[...]
    lane = lax.broadcasted_iota(i32, (tm, LANES), 1)
    lanef = lane.astype(f32)
    big = float(4 * LANES)
    first = lambda hit: jnp.min(jnp.where(hit, lanef, big), axis=-1, keepdims=True).astype(i32)
    gl = jnp.where(lane < R_GRP + N_GROUPS, logits, -jnp.inf)
    gmax = jnp.max(gl, axis=-1, keepdims=True)
    g_p = 1.0 / jnp.sum(jnp.exp(gl - gmax), axis=-1, keepdims=True)
    g_i = first(gl == gmax)
    e_lane = lane - R_EXP
    emask = (e_lane >= 0) & (e_lane < N_EXPERTS) & ((e_lane >> 3) == g_i)
    el = jnp.where(emask, logits, -jnp.inf)
    m1 = jnp.max(el, axis=-1, keepdims=True)
    i1 = first(el == m1)
    el2 = jnp.where(lane == i1, -jnp.inf, el)
    m2 = jnp.max(el2, axis=-1, keepdims=True)
    i2 = first(el2 == m2)
    r = jnp.exp(m2 - m1)
    w1 = g_p / (1.0 + r)
    w2 = g_p * r / (1.0 + r)
    eid_ref[...] = jnp.where(lane == 0, i1 - R_EXP, jnp.where(lane == 1, i2 - R_EXP, 0))
    ew_ref[...] = jnp.where(lane == 0, w1, jnp.where(lane == 1, w2, 0.0))


def _seq_tile(tp, cap):
    return max(t for t in range(BLK, cap + 1, BLK) if tp % t == 0)


def _outproj(l, oa, ob, oc, h, wo, gain, wr, br, t_real, tp):
    n, d = h.shape
    tm = _seq_tile(tp, 384)
    nj = tp // tm
    row = lambda w: pl.BlockSpec((tm, w), lambda b, j, l: (b * nj + j, 0))
    return pl.pallas_call(
        functools.partial(_outproj_kernel, t_real=t_real),
        grid_spec=pltpu.PrefetchScalarGridSpec(
            num_scalar_prefetch=1, grid=(n // tp, nj),
            in_specs=[row(A_WIDTH), row(B_WIDTH), row(C_WIDTH), row(d),
                      pl.BlockSpec((None, d, d), lambda b, j, l: (l[0], 0, 0)),
                      pl.BlockSpec((None, 1, d), lambda b, j, l: (l[0], 0, 0)),
                      pl.BlockSpec((None, d, LANES), lambda b, j, l: (l[0], 0, 0)),
                      pl.BlockSpec((None, 1, LANES), lambda b, j, l: (l[0], 0, 0))],
            out_specs=[row(d), row(d), row(LANES), row(LANES)]),
        out_shape=[jax.ShapeDtypeStruct((n, d), f32), jax.ShapeDtypeStruct((n, d), bf16),
                   jax.ShapeDtypeStruct((n, LANES), i32), jax.ShapeDtypeStruct((n, LANES), f32)],
        compiler_params=_cparams(("parallel", "parallel")),
        name="outproj",
    )(l, oa, ob, oc, h, wo, gain, wr, br)


def _moe_kernel(l_ref, te_ref, na_ref, xs_ref, w1_ref, w3_ref, w2_ref, rw_ref, ys_ref,
                w1b_ref, w3b_ref, w2b_ref):
    i = pl.program_id(0)

    @pl.when((i == 0) | (te_ref[i] != te_ref[jnp.maximum(i - 1, 0)]))
    def _():
        w1b_ref[...] = w1_ref[...].astype(bf16)
        w3b_ref[...] = w3_ref[...].astype(bf16)
        w2b_ref[...] = w2_ref[...].astype(bf16)

    @pl.when(i < na_ref[0])
    def _():
        x = xs_ref[...]
        h1 = _dot(x, w1b_ref[...])
        h3 = _dot(x, w3b_ref[...])
        hid = (h1 * (1.0 / (1.0 + jnp.exp(-h1))) * h3).astype(bf16)
        ys_ref[...] = (_dot(hid, w2b_ref[...]) * rw_ref[...]).astype(bf16)

    @pl.when(i >= na_ref[0])
    def _():
        ys_ref[...] = jnp.zeros_like(ys_ref)


def _moe(l, te, na, xs, w1, w3, w2, rw):
    p, d = xs.shape
    tm = MOE_TM
    return pl.pallas_call(
        _moe_kernel,
        grid_spec=pltpu.PrefetchScalarGridSpec(
            num_scalar_prefetch=3, grid=(p // tm,),
            in_specs=[
                pl.BlockSpec((tm, d), lambda i, l, te, na: (i, 0)),
                pl.BlockSpec((None, None, d, D_EXPERT), lambda i, l, te, na: (l[0], te[i], 0, 0)),
                pl.BlockSpec((None, None, d, D_EXPERT), lambda i, l, te, na: (l[0], te[i], 0, 0)),
                pl.BlockSpec((None, None, D_EXPERT, d), lambda i, l, te, na: (l[0], te[i], 0, 0)),
                pl.BlockSpec((tm, 1), lambda i, l, te, na: (i, 0)),
            ],
            out_specs=pl.BlockSpec((tm, d), lambda i, l, te, na: (i, 0)),
            scratch_shapes=[pltpu.VMEM((d, D_EXPERT), bf16), pltpu.VMEM((d, D_EXPERT), bf16),
                            pltpu.VMEM((D_EXPERT, d), bf16)]),
        out_shape=jax.ShapeDtypeStruct((p, d), bf16),
        compiler_params=_cparams(("arbitrary",)),
        name="moe",
    )(l, te, na, xs, w1, w3, w2, rw)


def _route(eid, ew, n_tiles):
    tm = MOE_TM
    p = n_tiles * tm
    n = eid.shape[0]
    e_flat = eid[:, :EXP_TOPK_CONST].T.reshape(-1)
    w_flat = ew[:, :EXP_TOPK_CONST].T.reshape(-1)
    na_all = e_flat.shape[0]
    hp = lax.Precision.HIGHEST
    ex = jnp.arange(N_EXPERTS, dtype=i32)[:, None]
    onehot = (ex == e_flat[None, :]).astype(f32)
    counts = jnp.sum(onehot, axis=1).astype(i32)
    pc = ((counts + tm - 1) // tm) * tm
    pend = jnp.cumsum(pc)
    po = pend - pc
    co = jnp.cumsum(counts) - counts
    order = jnp.argsort(e_flat, stable=True).astype(i32)
    r = jnp.arange(p, dtype=i32)
    step = (r[None, :] >= pend[:, None]).astype(f32)
    dlt = lambda v: jnp.concatenate([v[1:] - v[:-1], jnp.zeros((1,), v.dtype)]).astype(f32)
    tabs = jnp.stack([jnp.ones((N_EXPERTS,), f32), dlt(po), dlt(counts), dlt(co)])
    picked = jnp.dot(tabs, step, precision=hp).astype(i32)
    e_r = jnp.minimum(picked[0], N_EXPERTS - 1)
    local = r - (po[0] + picked[1])
    valid_r = (local < counts[0] + picked[2]) & (r < pend[-1])
    a_r = order[jnp.clip(co[0] + picked[3] + local, 0, na_all - 1)]
    tok_r = jnp.where(valid_r, jnp.where(a_r >= n, a_r - n, a_r), r % n)
    w_r = jnp.where(valid_r, w_flat[a_r], 0.0)
    n_act = (pend[-1] // tm).astype(i32)
    tile_e = e_r[::tm]
    te = jnp.where(jnp.arange(n_tiles, dtype=i32) < n_act, tile_e, tile_e[jnp.maximum(n_act - 1, 0)])
    inv = jnp.argsort(order).astype(i32)
    pos_a = jnp.dot((po - co).astype(f32)[None, :], onehot, precision=hp)[0].astype(i32) + inv
    return tok_r, w_r[:, None], te, n_act.reshape(1), pos_a


def _final_kernel(hmid_ref, ya_ref, yb_ref, out_ref):
    out_ref[...] = hmid_ref[...] + ya_ref[...].astype(f32) + yb_ref[...].astype(f32)


def _final(hmid, y2):
    n, d = hmid.shape
    tm = _row_tile(n, 512)
    nt = n // tm
    return pl.pallas_call(
        _final_kernel, grid=(nt,),
        in_specs=[pl.BlockSpec((tm, d), lambda i: (i, 0)), pl.BlockSpec((tm, d), lambda i: (i, 0)),
                  pl.BlockSpec((tm, d), lambda i: (i + nt, 0))],
        out_specs=pl.BlockSpec((tm, d), lambda i: (i, 0)),
        out_shape=jax.ShapeDtypeStruct((n, d), f32),
        compiler_params=_cparams(("parallel",)),
        name="final",
    )(hmid, y2, y2)


def _rope_tables(tp):
    pos = (jnp.arange(tp, dtype=f32) - FRONT)[:, None]

    def tab(dim, reps):
        inv = 1.0 / (ROPE_THETA ** (jnp.arange(0, dim, 2, dtype=f32) / dim))
        ang = pos * inv[None, :]
        return jnp.tile(jnp.cos(ang), (1, reps)), jnp.tile(jnp.sin(ang), (1, reps))

    cos, sin = tab(HEAD_DIM, 2)
    icos, isin = tab(IDX_DIM, 4)
    return cos, sin, icos, isin


def _split_w_in(w_in):
    parts, off = [], 0
    for s in IN_SPLITS:
        parts.append(w_in[..., off:off + s])
        off += s
    return parts


def kernel(x, meta, norm_mix_g, norm_ffn_g, w_in, a_qn_g, a_kn_g, b_qn_g, b_kn_g, b_f_bias,
           c_gate_w2, c_gate_b, c_on_g, w_out, r_group_w, r_group_b, r_exp_w, r_exp_b,
           e_w1, e_w3, e_w2):
    bsz, n_seq, d = x.shape
    depth = w_in.shape[0]
    t_real = n_seq + N_META
    k_top = min(TOPK_MAX, n_seq // 4)
    tp = -(-(FRONT + t_real) // BLK) * BLK
    n = bsz * tp

    (waq, wak, wav, wiq, wik, wiw, wbq, wbk, wbv, wbf, wcq, wck, wcv, wcr, wcg) = _split_w_in(w_in)
    zc = lambda w: jnp.zeros((depth, d, w), w_in.dtype)
    wcat = jnp.concatenate([waq, wak, wav, wiq, wik, zc(LANES - IDX_DIM), zc(LANES),
                            wbq, wbk, wbv, wcq, wck, wcv, wcg], axis=-1).astype(bf16)
    ws = jnp.concatenate([wiw, wbf, wcr, zc(LANES - S_CR - C_GATE_RANK)], axis=-1).astype(bf16)
    wo = w_out.astype(bf16)
    wr = jnp.concatenate([r_group_w, zc(R_EXP - N_GROUPS), r_exp_w, zc(LANES - R_EXP - N_EXPERTS)],
                         axis=-1).astype(bf16)
    zl = lambda w: jnp.zeros((depth, w), f32)
    br = jnp.concatenate([r_group_b, zl(R_EXP - N_GROUPS), r_exp_b, zl(LANES - R_EXP - N_EXPERTS)],
                         axis=-1)[:, None, :]
    fbias = jnp.concatenate([zl(S_BF), b_f_bias, zl(LANES - S_BF - B_HEADS)], axis=-1)[:, None, :]
    dk = C_HEADS * C_DK
    w2p = jnp.concatenate([jnp.zeros((depth, S_CR, dk), f32), c_gate_w2,
                           jnp.zeros((depth, LANES - S_CR - C_GATE_RANK, dk), f32)], axis=1).astype(bf16)
    gb = c_gate_b[:, None, :]
    og = c_on_g[:, None, :]
    g_mix = norm_mix_g[:, None, :]
    g_ffn = norm_ffn_g[:, None, :]
    gaq, gak, gbq, gbk = (g[:, None, :] for g in (a_qn_g, a_kn_g, b_qn_g, b_kn_g))
    emat = (jnp.arange(dk, dtype=i32)[:, None] // C_DK
            == jnp.arange(C_WIDTH, dtype=i32)[None, :] // C_DV).astype(bf16)
    tabs = _rope_tables(tp)

    h0 = jnp.concatenate([
        jnp.zeros((bsz, FRONT, d), f32),
        jnp.broadcast_to(meta[None].astype(f32), (bsz, N_META, d)),
        x.astype(f32),
        jnp.zeros((bsz, tp - FRONT - t_real, d), f32)], axis=1).reshape(n, d)
    n_tiles = -(-(EXP_TOPK_CONST * n + N_EXPERTS * (MOE_TM - 1)) // MOE_TM)

    def layer(li, carry):
        hmid, y2 = carry
        l = jnp.reshape(jnp.asarray(li, i32), (1,))
        h, zcat, zs = _inproj(l, hmid, y2, g_mix, wcat, ws)
        zcat3 = zcat.reshape(bsz, tp, N_CAT)
        zs3 = zs.reshape(bsz, tp, LANES)
        akr, ikr, bka = _kprep(l, zcat3, zs3, tabs, gak, gbk, fbias, t_real)
        oa = _dsa(l, zcat3, zs3, akr, ikr, tabs, gaq, t_real, k_top)
        ob = _fattn(l, zcat3, bka, gbq)
        oc = _gla(l, zcat3, zs3, w2p, gb, og, emat, t_real)
        hmid2, u, eid, ew = _outproj(l, oa.reshape(n, A_WIDTH), ob.reshape(n, B_WIDTH),
                                     oc.reshape(n, C_WIDTH), h, wo, g_ffn, wr, br, t_real, tp)
        tok_r, w_r, te, n_act, pos_a = _route(eid, ew, n_tiles)
        ys = _moe(l, te, n_act, jnp.take(u, tok_r, axis=0, mode="clip"), e_w1, e_w3, e_w2, w_r)
        return hmid2, jnp.take(ys, pos_a, axis=0, mode="clip")

    y0 = lax.optimization_barrier(jnp.zeros((EXP_TOPK_CONST * n, d), bf16))
    hmid, y2 = lax.fori_loop(0, depth, layer, (h0, y0))
    out = _final(hmid, y2).reshape(bsz, tp, d)
    return out[:, FRONT + N_META:FRONT + t_real].astype(x.dtype)
```

```python
import functools

import jax
import jax.numpy as jnp
from jax import lax
from jax.experimental import pallas as pl
from jax.experimental.pallas import tpu as pltpu

f32 = jnp.float32
bf16 = jnp.bfloat16
i32 = jnp.int32

D_MODEL = 2048
CHUNK = 64
N_META = 16
ROPE_THETA = 10000.0
EPS = 1e-6
HEAD_DIM = 128
A_HEADS = 6
IDX_HEADS = 16
IDX_DIM = 64
TOPK_MAX = 256
B_HEADS = 6
C_HEADS = 4
C_DK = 64
C_DV = 128
C_GATE_RANK = 16
C_TAU = 16.0
N_GROUPS = 4
EXP_PER_GROUP = 8
N_EXPERTS = N_GROUPS * EXP_PER_GROUP
D_EXPERT = 512
A_WIDTH = A_HEADS * HEAD_DIM
B_WIDTH = B_HEADS * HEAD_DIM
C_WIDTH = C_HEADS * C_DV
IN_SPLITS = (A_WIDTH, HEAD_DIM, HEAD_DIM, IDX_HEADS * IDX_DIM, IDX_DIM, IDX_HEADS,
             B_WIDTH, B_WIDTH, B_WIDTH, B_HEADS,
             C_HEADS * C_DK, C_HEADS * C_DK, C_WIDTH, C_GATE_RANK, C_WIDTH)

LANES = 128
SUBLANES = 8
FRONT = (-N_META) % CHUNK
BLK = 128
T_AQ, T_AK, T_AV, T_IQ, T_IK = 0, 6, 7, 8, 16
T_BQ, T_BK, T_BV = 18, 24, 30
T_CQ, T_CK, T_CV, T_CG = 36, 38, 40, 44
N_CAT = 48 * LANES
S_IW, S_BF, S_CR = 0, 16, 22
R_GRP, R_EXP = 0, 32
INPROJ_TN = 1536
GLA_C = 32
DSA_G = 3
MOE_TM = 256
EXP_TOPK_CONST = 2
NEG = -1e30
LOG2E = 1.4426950408889634
INT_MIN = -2 ** 31
VMEM_LIMIT = 56 * 1024 * 1024


def _cparams(sem):
    return pltpu.CompilerParams(dimension_semantics=sem, vmem_limit_bytes=VMEM_LIMIT)


def _row_tile(n, cap):
    t = cap
    while n % t:
        t //= 2
    return t


def _log_sigmoid(x):
    return jnp.minimum(x, 0.0) - jnp.log(1.0 + jnp.exp(-jnp.abs(x)))


def _dot(a, b):
    return jnp.dot(a, b, preferred_element_type=f32)


def _dot_nt(a, b):
    return lax.dot_general(a, b, (((1,), (1,)), ((), ())), preferred_element_type=f32)


def _dot_tn(a, b):
    return lax.dot_general(a, b, (((0,), (0,)), ((), ())), preferred_element_type=f32)


def _split3(x):
    hi = x.astype(bf16)
    r1 = x - hi.astype(f32)
    mid = r1.astype(bf16)
    lo = (r1 - mid.astype(f32)).astype(bf16)
    return hi, mid, lo


def _rope128(x, cos, sin, lane):
    return x * cos + pltpu.roll(x, 64, 1) * jnp.where(lane < 64, -sin, sin)


def _rope64(x, cos, sin, lane):
    low = (lane & 63) < 32
    return (x * cos + pltpu.roll(x, 32, 1) * jnp.where(low, 0.0, sin)
            + pltpu.roll(x, 96, 1) * jnp.where(low, -sin, 0.0))


def _rms_gain(x, g):
    return x * lax.rsqrt(jnp.mean(x * x, axis=-1, keepdims=True) + EPS) * g


def _prenorm_kernel(l_ref, hmid_ref, ya_ref, yb_ref, g_ref, ws_ref, h_ref, xn_ref, zs_ref):
    h = hmid_ref[...] + ya_ref[...].astype(f32) + yb_ref[...].astype(f32)
    h_ref[...] = h
    xn = _rms_gain(h, g_ref[...]).astype(bf16)
    xn_ref[...] = xn
    zs_ref[...] = _dot(xn, ws_ref[...])


def _inproj_kernel(l_ref, xn_ref, w_ref, z_ref):
    z_ref[...] = _dot(xn_ref[...], w_ref[...]).astype(bf16)


def _inproj(l, hmid, y2, gain, wcat, ws):
    n, d = hmid.shape
    tm = _row_tile(n, 512)
    nt = n // tm
    row = lambda w: pl.BlockSpec((tm, w), lambda i, l: (i, 0))
    h, xn, zs = pl.pallas_call(
        _prenorm_kernel,
        grid_spec=pltpu.PrefetchScalarGridSpec(
            num_scalar_prefetch=1, grid=(nt,),
            in_specs=[row(d), row(d),
                      pl.BlockSpec((tm, d), lambda i, l: (i + nt, 0)),
                      pl.BlockSpec((None, 1, d), lambda i, l: (l[0], 0, 0)),
                      pl.BlockSpec((None, d, LANES), lambda i, l: (l[0], 0, 0))],
            out_specs=[row(d), row(d), row(LANES)]),
        out_shape=[jax.ShapeDtypeStruct((n, d), f32), jax.ShapeDtypeStruct((n, d), bf16),
                   jax.ShapeDtypeStruct((n, LANES), f32)],
        compiler_params=_cparams(("parallel",)),
        name="prenorm",
    )(l, hmid, y2, y2, gain, ws)
    tn = INPROJ_TN
    zcat = pl.pallas_call(
        _inproj_kernel,
        grid_spec=pltpu.PrefetchScalarGridSpec(
            num_scalar_prefetch=1, grid=(N_CAT // tn, nt),
            in_specs=[pl.BlockSpec((tm, d), lambda j, i, l: (i, 0)),
                      pl.BlockSpec((None, d, tn), lambda j, i, l: (l[0], 0, j))],
            out_specs=pl.BlockSpec((tm, tn), lambda j, i, l: (i, j))),
        out_shape=jax.ShapeDtypeStruct((n, N_CAT), bf16),
        compiler_params=_cparams(("parallel", "parallel")),
        name="inproj",
    )(l, xn, wcat)
    return h, zcat, zs


def _kprep_kernel(l_ref, ak_ref, ik_ref, bk_ref, zs_ref,
                  cos_ref, sin_ref, icos_ref, isin_ref, gak_ref, gbk_ref, fb_ref,
                  akr_ref, ikr_ref, bka_ref, carry_ref, *, t_real):
    k = pl.program_id(1)
    lane = lax.broadcasted_iota(i32, (BLK, LANES), 1)
    row = lax.broadcasted_iota(i32, (BLK, LANES), 0)
    pos = k * BLK + row
    valid = (pos >= FRONT) & (pos < FRONT + t_real)

    x = ak_ref[...].astype(f32)
    akr_ref[...] = _rope128(_rms_gain(x, gak_ref[...]), cos_ref[...], sin_ref[...], lane).astype(bf16)
    ikr_ref[...] = _rope64(ik_ref[...].astype(f32), icos_ref[...], isin_ref[...], lane).astype(bf16)

    @pl.when(k == 0)
    def _():
        carry_ref[...] = jnp.zeros_like(carry_ref)

    lf = jnp.where(valid, _log_sigmoid(zs_ref[...] + fb_ref[...]), 0.0)
    tri = (row >= lane).astype(bf16)
    hi, mid, lo = _split3(lf)
    fcum = _dot(tri, hi) + _dot(tri, mid) + _dot(tri, lo) + carry_ref[...]
    carry_ref[...] = fcum[BLK - 1:BLK, :]

    fs = jnp.where(valid, fcum * (-(HEAD_DIM ** 0.5)), NEG)
    for h in range(B_HEADS):
        sl = slice(h * LANES, (h + 1) * LANES)
        bka_ref[h, :, 0:LANES] = _rms_gain(bk_ref[:, sl].astype(f32), gbk_ref[...]).astype(bf16)
        p0, p1, p2 = _split3(fs[:, S_BF + h:S_BF + h + 1])
        aug = jnp.where(lane == 0, p0.astype(f32),
                        jnp.where(lane == 1, p1.astype(f32), jnp.where(lane == 2, p2.astype(f32), 0.0)))
        bka_ref[h, :, LANES:2 * LANES] = aug.astype(bf16)


def _kprep(l, zcat3, zs3, tabs, gak, gbk, fbias, t_real):
    bsz, tp, _ = zcat3.shape
    nkb = tp // BLK
    cos, sin, icos, isin = tabs
    tile = lambda c: pl.BlockSpec((None, BLK, LANES), lambda b, k, l, c=c: (b, k, c))
    wide = lambda c: pl.BlockSpec((None, BLK, B_WIDTH), lambda b, k, l, c=c: (b, k, c))
    tab = pl.BlockSpec((BLK, LANES), lambda b, k, l: (k, 0))
    gain = pl.BlockSpec((None, 1, LANES), lambda b, k, l: (l[0], 0, 0))
    return pl.pallas_call(
        functools.partial(_kprep_kernel, t_real=t_real),
        grid_spec=pltpu.PrefetchScalarGridSpec(
            num_scalar_prefetch=1, grid=(bsz, nkb),
            in_specs=[tile(T_AK), tile(T_IK),
                      wide(T_BK * LANES // B_WIDTH),
                      pl.BlockSpec((None, BLK, LANES), lambda b, k, l: (b, k, 0)),
                      tab, tab, tab, tab, gain, gain, gain],
            out_specs=[
                pl.BlockSpec((None, BLK, LANES), lambda b, k, l: (b, k, 0)),
                pl.BlockSpec((None, BLK, LANES), lambda b, k, l: (b, k, 0)),
                pl.BlockSpec((None, B_HEADS, BLK, 2 * LANES), lambda b, k, l: (b, 0, k, 0)),
            ],
            scratch_shapes=[pltpu.VMEM((1, LANES), f32)]),
        out_shape=[jax.ShapeDtypeStruct((bsz, tp, LANES), bf16),
                   jax.ShapeDtypeStruct((bsz, tp, LANES), bf16),
                   jax.ShapeDtypeStruct((bsz, B_HEADS, tp, 2 * LANES), bf16)],
        compiler_params=_cparams(("parallel", "arbitrary")),
        name="kprep",
    )(l, zcat3, zcat3, zcat3, zs3, cos, sin, icos, isin, gak, gbk, fbias)


def _dsa_kernel(l_ref, aq_ref, iq_ref, zs_ref, akr_ref, ikr_ref, av_ref,
                cos_ref, sin_ref, icos_ref, isin_ref, gq_ref, out_ref,
                key_ref, iqs_ref, q6_ref, iwt_ref, m_ref, s_ref, acc_ref, stq_ref, *, t_real, k_top):
    i = pl.program_id(1)
    nk = i + 1
    lane = lax.broadcasted_iota(i32, (BLK, LANES), 1)

    for h in range(A_HEADS):
        x = aq_ref[:, h * LANES:(h + 1) * LANES].astype(f32)
        xr = _rope128(_rms_gain(x, gq_ref[...]), cos_ref[...], sin_ref[...], lane)
        q6_ref[h * BLK:(h + 1) * BLK, :] = xr.astype(bf16)
    for t in range(IDX_HEADS // 2):
        x = iq_ref[:, t * LANES:(t + 1) * LANES].astype(f32)
        xr = _rope64(x, icos_ref[...], isin_ref[...], lane)
        iqs_ref[(2 * t) * BLK:(2 * t + 1) * BLK, :] = xr.astype(bf16)
        iqs_ref[(2 * t + 1) * BLK:(2 * t + 2) * BLK, :] = pltpu.roll(xr, 64, 1).astype(bf16)
    iwt_ref[...] = (zs_ref[...] * (IDX_HEADS ** -0.5 * IDX_DIM ** -0.5)).T

    gb = DSA_G * BLK
    ng = lax.div(nk, jnp.int32(DSA_G))
    nr = nk - ng * DSA_G

    def over_keys(fn, init, merge=()):
        c, g0 = init, 0
        for m in merge:
            nm = lax.div(ng - g0, jnp.int32(m))
            c = lax.fori_loop(0, nm, lambda g, c, g0=g0, m=m: fn(pl.multiple_of((g0 + g * m) * gb, gb), m * gb, c), c)
            g0 = g0 + nm * m
        c = lax.fori_loop(g0, ng, lambda g, c: fn(pl.multiple_of(g * gb, gb), gb, c), c)
        return lax.fori_loop(0, nr, lambda r, c: fn(pl.multiple_of((ng * DSA_G + r) * BLK, BLK), BLK, c), c)

    def score_rows(k0, nrows, carry):
        dt = _dot_nt(ikr_ref[pl.ds(k0, nrows), :], iqs_ref[...])
        s = jnp.zeros((nrows, LANES), f32)
        for h in range(IDX_HEADS):
            s = s + iwt_ref[h:h + 1, :] * jnp.maximum(dt[:, h * LANES:(h + 1) * LANES], 0.0)
        kpos = k0 + lax.broadcasted_iota(i32, (nrows, LANES), 0)
        qpos = i * BLK + lax.broadcasted_iota(i32, (nrows, LANES), 1)
        adm = ((kpos >> 6) <= (qpos >> 6)) & (kpos >= FRONT) & (kpos < FRONT + t_real)
        bits = lax.bitcast_convert_type(s, i32)
        key = bits ^ ((bits >> 31) & 0x7FFFFFFF)
        key_ref[pl.ds(k0, nrows), :] = jnp.where(adm, key, INT_MIN)
        return carry

    over_keys(score_rows, 0, merge=(2,))

    def bit_body(t, thr_u):
        bit = jnp.left_shift(jnp.int32(1), 31 - t)
        cand_u = thr_u | bit
        cand_s = cand_u ^ INT_MIN

        def count_rows(k0, nrows, c):
            hit = (key_ref[pl.ds(k0, nrows), :] >= cand_s).astype(i32)
            for j in range(nrows // BLK):
                c = c + hit[j * BLK:(j + 1) * BLK, :]
            return c

        cnt = over_keys(count_rows, jnp.zeros((BLK, LANES), i32), merge=(2,))
        tot = jnp.sum(cnt.astype(f32), axis=0, keepdims=True)
        return jnp.where(tot >= k_top, cand_u, thr_u)

    thr_u = lax.fori_loop(0, 32, bit_body, jnp.zeros((1, LANES), i32))
    thr_s = jnp.maximum(thr_u ^ INT_MIN, INT_MIN + 1)

    m_ref[...] = jnp.full(m_ref.shape, NEG, f32)
    s_ref[...] = jnp.zeros(s_ref.shape, f32)
    acc_ref[...] = jnp.zeros(acc_ref.shape, f32)
    c2 = HEAD_DIM ** -0.5 * LOG2E

    def attn_rows(k0, nrows, carry, st=None):
        if st is None:
            st = _dot_nt(akr_ref[pl.ds(k0, nrows), :], q6_ref[...])
        bias = jnp.where(key_ref[pl.ds(k0, nrows), :] >= thr_s, 0.0, NEG)
        m_news = []
        for h in range(A_HEADS):
            sl = slice(h * LANES, (h + 1) * LANES)
            sh = st[:, sl] + bias
            stq_ref[0, 0:nrows, sl] = sh
            m_news.append(jnp.maximum(m_ref[:, sl], jnp.max(sh, axis=0, keepdims=True)))
        ps, alphas = [], []
        for h in range(A_HEADS):
            sl = slice(h * LANES, (h + 1) * LANES)
            sh = stq_ref[0, 0:nrows, sl]
            m_old = m_ref[:, sl]
            m_new = m_news[h]
            alpha = jnp.exp2((m_old - m_new) * c2)
            p = jnp.exp2((sh - m_new) * c2)
            s_ref[:, sl] = s_ref[:, sl] * alpha + jnp.sum(p, axis=0, keepdims=True)
            m_ref[:, sl] = m_new
            alphas.append(alpha)
            ps.append(p.astype(bf16))
        pv = _dot_tn(av_ref[pl.ds(k0, nrows), :], jnp.concatenate(ps, axis=1))
        acc_ref[...] = acc_ref[...] * jnp.concatenate(alphas, axis=1) + pv
        return carry

    over_keys(attn_rows, 0, merge=(4, 2))

    for h in range(A_HEADS):
        sl = slice(h * LANES, (h + 1) * LANES)
        o = acc_ref[:, sl] / jnp.maximum(s_ref[:, sl], 1e-30)
        out_ref[:, sl] = o.T.astype(bf16)


def _dsa(l, zcat3, zs3, akr, ikr, tabs, gaq, t_real, k_top):
    bsz, tp, _ = zcat3.shape
    nkb = tp // BLK
    cos, sin, icos, isin = tabs
    tab = pl.BlockSpec((BLK, LANES), lambda b, i, l: (i, 0))
    full = pl.BlockSpec((None, tp, LANES), lambda b, i, l: (b, 0, 0))
    return pl.pallas_call(
        functools.partial(_dsa_kernel, t_real=t_real, k_top=k_top),
        grid_spec=pltpu.PrefetchScalarGridSpec(
            num_scalar_prefetch=1, grid=(bsz, nkb),
            in_specs=[
                pl.BlockSpec((None, BLK, A_WIDTH), lambda b, i, l: (b, i, 0)),
                pl.BlockSpec((None, BLK, IDX_HEADS * IDX_DIM), lambda b, i, l: (b, i, T_IQ * LANES // (IDX_HEADS * IDX_DIM))),
                pl.BlockSpec((None, BLK, LANES), lambda b, i, l: (b, i, 0)),
                full, full,
                pl.BlockSpec((None, tp, LANES), lambda b, i, l: (b, 0, T_AV)),
                tab, tab, tab, tab,
                pl.BlockSpec((None, 1, LANES), lambda b, i, l: (l[0], 0, 0)),
            ],
            out_specs=pl.BlockSpec((None, BLK, A_WIDTH), lambda b, i, l: (b, i, 0)),
            scratch_shapes=[
                pltpu.VMEM((tp, LANES), i32),
                pltpu.VMEM((IDX_HEADS * BLK, LANES), bf16),
                pltpu.VMEM((A_HEADS * BLK, LANES), bf16),
                pltpu.VMEM((LANES, BLK), f32),
                pltpu.VMEM((1, A_HEADS * BLK), f32),
                pltpu.VMEM((1, A_HEADS * BLK), f32),
                pltpu.VMEM((HEAD_DIM, A_HEADS * BLK), f32),
                pltpu.VMEM((1, 4 * DSA_G * BLK, A_HEADS * BLK), f32),
            ]),
        out_shape=jax.ShapeDtypeStruct((bsz, tp, A_WIDTH), bf16),
        compiler_params=_cparams(("parallel", "arbitrary")),
        name="dsa",
    )(l, zcat3, zcat3, zs3, akr, ikr, zcat3, cos, sin, icos, isin, gaq)


def _fattn_kernel(l_ref, bq_ref, bka_ref, bv_ref, gq_ref, out_ref, qa_ref, m_ref, s_ref, acc_ref, st_ref):
    i = pl.program_id(1)
    fb = bq_ref.shape[0]
    ones3 = jnp.where(lax.broadcasted_iota(i32, (fb, LANES), 1) < 3, 1.0, 0.0).astype(bf16)
    for h in range(B_HEADS):
        sl = slice(h * LANES, (h + 1) * LANES)
        qa_ref[h, :, 0:LANES] = _rms_gain(bq_ref[:, sl].astype(f32), gq_ref[...]).astype(bf16)
        qa_ref[h, :, LANES:2 * LANES] = ones3
    m_ref[...] = jnp.full(m_ref.shape, NEG, f32)
    s_ref[...] = jnp.zeros(s_ref.shape, f32)
    acc_ref[...] = jnp.zeros(acc_ref.shape, f32)
    c2 = HEAD_DIM ** -0.5 * LOG2E
    causal = lax.broadcasted_iota(i32, (fb, fb), 0) <= lax.broadcasted_iota(i32, (fb, fb), 1)

    def step(kb, diagonal):
        k0 = pl.multiple_of(kb * fb, fb)
        m_news = []
        for h in range(B_HEADS):
            st = _dot_nt(bka_ref[h, pl.ds(k0, fb), :], qa_ref[h])
            if diagonal:
                st = jnp.where(causal, st, NEG)
            st_ref[h] = st
            m_news.append(jnp.maximum(m_ref[h], jnp.max(st, axis=0, keepdims=True)))
        for h in range(B_HEADS):
            sl = slice(h * LANES, (h + 1) * LANES)
            st = st_ref[h]
            m_old = m_ref[h]
            m_new = m_news[h]
            alpha = jnp.exp2((m_old - m_new) * c2)
            p = jnp.exp2((st - m_new) * c2)
            s_ref[h] = s_ref[h] * alpha + jnp.sum(p, axis=0, keepdims=True)
            m_ref[h] = m_new
            acc_ref[h] = acc_ref[h] * alpha + _dot_tn(bv_ref[pl.ds(k0, fb), sl], p.astype(bf16))

    def body(kb, carry):
        step(kb, False)
        return carry

    lax.fori_loop(0, i, body, 0)
    step(i, True)
    for h in range(B_HEADS):
        sl = slice(h * LANES, (h + 1) * LANES)
        out_ref[:, sl] = (acc_ref[h] / s_ref[h]).T.astype(bf16)


def _fattn(l, zcat3, bka, gbq):
    bsz, tp, _ = zcat3.shape
    fb = _seq_tile(tp, 384)
    once = pl.Buffered(1)
    return pl.pallas_call(
        _fattn_kernel,
        grid_spec=pltpu.PrefetchScalarGridSpec(
            num_scalar_prefetch=1, grid=(bsz, tp // fb),
            in_specs=[
                pl.BlockSpec((None, fb, B_WIDTH), lambda b, i, l: (b, i, T_BQ * LANES // B_WIDTH)),
                pl.BlockSpec((None, B_HEADS, tp, 2 * LANES), lambda b, i, l: (b, 0, 0, 0), pipeline_mode=once),
                pl.BlockSpec((None, tp, B_WIDTH), lambda b, i, l: (b, 0, T_BV * LANES // B_WIDTH),
                             pipeline_mode=once),
                pl.BlockSpec((None, 1, LANES), lambda b, i, l: (l[0], 0, 0)),
            ],
            out_specs=pl.BlockSpec((None, fb, B_WIDTH), lambda b, i, l: (b, i, 0)),
            scratch_shapes=[
                pltpu.VMEM((B_HEADS, fb, 2 * LANES), bf16),
                pltpu.VMEM((B_HEADS, 1, fb), f32),
                pltpu.VMEM((B_HEADS, 1, fb), f32),
                pltpu.VMEM((B_HEADS, HEAD_DIM, fb), f32),
                pltpu.VMEM((B_HEADS, fb, fb), f32),
            ]),
        out_shape=jax.ShapeDtypeStruct((bsz, tp, B_WIDTH), bf16),
        compiler_params=_cparams(("parallel", "arbitrary")),
        name="fattn",
    )(l, zcat3, bka, zcat3, gbq)


def _gla_kernel(l_ref, cq_ref, ck_ref, cv_ref, cg_ref, zs_ref, w2_ref, gb_ref, og_ref, e_ref,
                out_ref, st_ref, oi_ref, pp_ref, rr_ref, *, t_real):
    c = pl.program_id(0)
    nb, cc, dk = cq_ref.shape

    @pl.when(c == 0)
    def _():
        st_ref[...] = jnp.zeros_like(st_ref)

    rowc = lax.broadcasted_iota(i32, (cc, 1), 0)
    pos = c * cc + rowc
    valid = (pos >= FRONT) & (pos < FRONT + t_real)
    tri = (lax.broadcasted_iota(i32, (cc, cc), 0) >= lax.broadcasted_iota(i32, (cc, cc), 1)).astype(bf16)
    lane_k = lax.broadcasted_iota(i32, (1, dk), 1)
    head_masks = [(lane_k >= h * C_DK) & (lane_k < (h + 1) * C_DK) for h in range(C_HEADS)]

    used = [(r // SUBLANES + 1) * SUBLANES for r in range(cc)]

    for b in range(nb):
        x = _dot(zs_ref[b].astype(bf16), w2_ref[...]) + gb_ref[...]
        la = jnp.where(valid, _log_sigmoid(x) * (1.0 / C_TAU), 0.0)
        hi, mid, lo = _split3(la)
        bc = _dot(tri, hi) + _dot(tri, mid) + _dot(tri, lo)
        q = cq_ref[b].astype(f32) * (C_DK ** -0.5)
        k = ck_ref[b].astype(f32)
        v = cv_ref[b]
        bc2 = bc * LOG2E
        blast2 = bc2[cc - 1:cc, :]
        qe = q * jnp.exp2(bc2)
        ke = k * jnp.exp2(blast2 - bc2)
        st = st_ref[b]
        stb = st.astype(bf16)
        oi_ref[b] = jnp.concatenate(
            [_dot_nt(jnp.where(head_masks[h], qe, 0.0).astype(bf16), stb) for h in range(C_HEADS)], axis=1)
        new_st = st * jnp.exp2(blast2)
        for h in range(C_HEADS):
            km = jnp.where(head_masks[h], ke, 0.0).astype(bf16)
            new_st = new_st + _dot_tn(v[:, h * C_DV:(h + 1) * C_DV], km)
        st_ref[b] = new_st
        for r in range(cc):
            nu = used[r]
            dec = jnp.exp2(jnp.minimum(bc2[r:r + 1, :] - bc2[0:nu, :], 0.0))
            pr = jnp.where(rowc[0:nu] <= r, q[r:r + 1, :] * k[0:nu, :] * dec, 0.0)
            if nu < cc:
                pr = jnp.concatenate([pr, jnp.zeros((cc - nu, dk), f32)], axis=0)
            pp_ref[b, r * cc:(r + 1) * cc, :] = pr.astype(bf16)

    for b in range(nb):
        rr_ref[b] = _dot(pp_ref[b], e_ref[...])

    for b in range(nb):
        vf = cv_ref[b].astype(f32)
        for r in range(cc):
            nu = used[r]
            oi_ref[b, r:r + 1, :] += jnp.sum(rr_ref[b, r * cc:r * cc + nu, :] * vf[0:nu, :], axis=0, keepdims=True)
        o = oi_ref[b]
        g = cg_ref[b].astype(f32)
        gs = g * (1.0 / (1.0 + jnp.exp(-g)))
        for h in range(C_HEADS):
            sl = slice(h * C_DV, (h + 1) * C_DV)
            out_ref[b, :, sl] = (_rms_gain(o[:, sl], og_ref[...]) * gs[:, sl]).astype(bf16)


def _gla(l, zcat3, zs3, w2p, gb, og, emat, t_real):
    bsz, tp, _ = zcat3.shape
    cc = GLA_C
    dk = C_HEADS * C_DK
    return pl.pallas_call(
        functools.partial(_gla_kernel, t_real=t_real),
        grid_spec=pltpu.PrefetchScalarGridSpec(
            num_scalar_prefetch=1, grid=(tp // cc,),
            in_specs=[
                pl.BlockSpec((bsz, cc, dk), lambda c, l: (0, c, T_CQ * LANES // dk)),
                pl.BlockSpec((bsz, cc, dk), lambda c, l: (0, c, T_CK * LANES // dk)),
                pl.BlockSpec((bsz, cc, C_WIDTH), lambda c, l: (0, c, T_CV * LANES // C_WIDTH)),
                pl.BlockSpec((bsz, cc, C_WIDTH), lambda c, l: (0, c, T_CG * LANES // C_WIDTH)),
                pl.BlockSpec((bsz, cc, LANES), lambda c, l: (0, c, 0)),
                pl.BlockSpec((None, LANES, dk), lambda c, l: (l[0], 0, 0)),
                pl.BlockSpec((None, 1, dk), lambda c, l: (l[0], 0, 0)),
                pl.BlockSpec((None, 1, C_DV), lambda c, l: (l[0], 0, 0)),
                pl.BlockSpec((dk, C_WIDTH), lambda c, l: (0, 0)),
            ],
            out_specs=pl.BlockSpec((bsz, cc, C_WIDTH), lambda c, l: (0, c, 0)),
            scratch_shapes=[pltpu.VMEM((bsz, C_DV, dk), f32), pltpu.VMEM((bsz, cc, C_WIDTH), f32),
                            pltpu.VMEM((bsz, cc * cc, dk), bf16), pltpu.VMEM((bsz, cc * cc, C_WIDTH), f32)]),
        out_shape=jax.ShapeDtypeStruct((bsz, tp, C_WIDTH), bf16),
        compiler_params=_cparams(("arbitrary",)),
        name="gla",
    )(l, zcat3, zcat3, zcat3, zcat3, zs3, w2p, gb, og, emat)


def _outproj_kernel(l_ref, oa_ref, ob_ref, oc_ref, h_ref, wo_ref, g_ref, wr_ref, br_ref,
                    hmid_ref, u_ref, eid_ref, ew_ref, *, t_real):
    tm = h_ref.shape[0]
    mix = (_dot(oa_ref[...], wo_ref[0:A_WIDTH, :])
           + _dot(ob_ref[...], wo_ref[A_WIDTH:A_WIDTH + B_WIDTH, :])
           + _dot(oc_ref[...], wo_ref[A_WIDTH + B_WIDTH:, :]))
    pos = pl.program_id(1) * tm + lax.broadcasted_iota(i32, (tm, 1), 0)
    valid = (pos >= FRONT) & (pos < FRONT + t_real)
    hm = h_ref[...] + jnp.where(valid, mix, 0.0)
    hmid_ref[...] = hm
    u = _rms_gain(hm, g_ref[...]).astype(bf16)
    u_ref[...] = u

    logits = _dot(u, wr_ref[...]) + br_ref[...]
    lane = lax.broadcasted_iota(i32, (tm, LANES), 1)
    lanef = lane.astype(f32)
    big = float(4 * LANES)
    first = lambda hit: jnp.min(jnp.where(hit, lanef, big), axis=-1, keepdims=True).astype(i32)
    gl = jnp.where(lane < R_GRP + N_GROUPS, logits, -jnp.inf)
    gmax = jnp.max(gl, axis=-1, keepdims=True)
    g_p = 1.0 / jnp.sum(jnp.exp(gl - gmax), axis=-1, keepdims=True)
    g_i = first(gl == gmax)
    e_lane = lane - R_EXP
    emask = (e_lane >= 0) & (e_lane < N_EXPERTS) & ((e_lane >> 3) == g_i)
    el = jnp.where(emask, logits, -jnp.inf)
    m1 = jnp.max(el, axis=-1, keepdims=True)
    i1 = first(el == m1)
    el2 = jnp.where(lane == i1, -jnp.inf, el)
    m2 = jnp.max(el2, axis=-1, keepdims=True)
    i2 = first(el2 == m2)
    r = jnp.exp(m2 - m1)
    w1 = g_p / (1.0 + r)
    w2 = g_p * r / (1.0 + r)
    eid_ref[...] = jnp.where(lane == 0, i1 - R_EXP, jnp.where(lane == 1, i2 - R_EXP, 0))
    ew_ref[...] = jnp.where(lane == 0, w1, jnp.where(lane == 1, w2, 0.0))


def _seq_tile(tp, cap):
    return max(t for t in range(BLK, cap + 1, BLK) if tp % t == 0)


def _outproj(l, oa, ob, oc, h, wo, gain, wr, br, t_real, tp):
    n, d = h.shape
    tm = _seq_tile(tp, 384)
    nj = tp // tm
    row = lambda w: pl.BlockSpec((tm, w), lambda b, j, l: (b * nj + j, 0))
    return pl.pallas_call(
        functools.partial(_outproj_kernel, t_real=t_real),
        grid_spec=pltpu.PrefetchScalarGridSpec(
            num_scalar_prefetch=1, grid=(n // tp, nj),
            in_specs=[row(A_WIDTH), row(B_WIDTH), row(C_WIDTH), row(d),
                      pl.BlockSpec((None, d, d), lambda b, j, l: (l[0], 0, 0)),
                      pl.BlockSpec((None, 1, d), lambda b, j, l: (l[0], 0, 0)),
                      pl.BlockSpec((None, d, LANES), lambda b, j, l: (l[0], 0, 0)),
                      pl.BlockSpec((None, 1, LANES), lambda b, j, l: (l[0], 0, 0))],
            out_specs=[row(d), row(d), row(LANES), row(LANES)]),
        out_shape=[jax.ShapeDtypeStruct((n, d), f32), jax.ShapeDtypeStruct((n, d), bf16),
                   jax.ShapeDtypeStruct((n, LANES), i32), jax.ShapeDtypeStruct((n, LANES), f32)],
        compiler_params=_cparams(("parallel", "parallel")),
        name="outproj",
    )(l, oa, ob, oc, h, wo, gain, wr, br)


def _moe_kernel(l_ref, te_ref, na_ref, xs_ref, w1_ref, w3_ref, w2_ref, rw_ref, ys_ref,
                w1b_ref, w3b_ref, w2b_ref):
    i = pl.program_id(0)

    @pl.when((i == 0) | (te_ref[i] != te_ref[jnp.maximum(i - 1, 0)]))
    def _():
        w1b_ref[...] = w1_ref[...].astype(bf16)
        w3b_ref[...] = w3_ref[...].astype(bf16)
        w2b_ref[...] = w2_ref[...].astype(bf16)

    @pl.when(i < na_ref[0])
    def _():
        x = xs_ref[...]
        h1 = _dot(x, w1b_ref[...])
        h3 = _dot(x, w3b_ref[...])
        hid = (h1 * (1.0 / (1.0 + jnp.exp(-h1))) * h3).astype(bf16)
        ys_ref[...] = (_dot(hid, w2b_ref[...]) * rw_ref[...]).astype(bf16)

    @pl.when(i >= na_ref[0])
    def _():
        ys_ref[...] = jnp.zeros_like(ys_ref)


def _moe(l, te, na, xs, w1, w3, w2, rw):
    p, d = xs.shape
    tm = MOE_TM
    return pl.pallas_call(
        _moe_kernel,
        grid_spec=pltpu.PrefetchScalarGridSpec(
            num_scalar_prefetch=3, grid=(p // tm,),
            in_specs=[
                pl.BlockSpec((tm, d), lambda i, l, te, na: (i, 0)),
                pl.BlockSpec((None, None, d, D_EXPERT), lambda i, l, te, na: (l[0], te[i], 0, 0)),
                pl.BlockSpec((None, None, d, D_EXPERT), lambda i, l, te, na: (l[0], te[i], 0, 0)),
                pl.BlockSpec((None, None, D_EXPERT, d), lambda i, l, te, na: (l[0], te[i], 0, 0)),
                pl.BlockSpec((tm, 1), lambda i, l, te, na: (i, 0)),
            ],
            out_specs=pl.BlockSpec((tm, d), lambda i, l, te, na: (i, 0)),
            scratch_shapes=[pltpu.VMEM((d, D_EXPERT), bf16), pltpu.VMEM((d, D_EXPERT), bf16),
                            pltpu.VMEM((D_EXPERT, d), bf16)]),
        out_shape=jax.ShapeDtypeStruct((p, d), bf16),
        compiler_params=_cparams(("arbitrary",)),
        name="moe",
    )(l, te, na, xs, w1, w3, w2, rw)


def _route(eid, ew, n_tiles):
    tm = MOE_TM
    p = n_tiles * tm
    n = eid.shape[0]
    e_flat = eid[:, :EXP_TOPK_CONST].T.reshape(-1)
    w_flat = ew[:, :EXP_TOPK_CONST].T.reshape(-1)
    na_all = e_flat.shape[0]
    hp = lax.Precision.HIGHEST
    ex = jnp.arange(N_EXPERTS, dtype=i32)[:, None]
    onehot = (ex == e_flat[None, :]).astype(f32)
    counts = jnp.sum(onehot, axis=1).astype(i32)
    pc = ((counts + tm - 1) // tm) * tm
    pend = jnp.cumsum(pc)
    po = pend - pc
    co = jnp.cumsum(counts) - counts
    order = jnp.argsort(e_flat, stable=True).astype(i32)
    r = jnp.arange(p, dtype=i32)
    step = (r[None, :] >= pend[:, None]).astype(f32)
    dlt = lambda v: jnp.concatenate([v[1:] - v[:-1], jnp.zeros((1,), v.dtype)]).astype(f32)
    tabs = jnp.stack([jnp.ones((N_EXPERTS,), f32), dlt(po), dlt(counts), dlt(co)])
    picked = jnp.dot(tabs, step, precision=hp).astype(i32)
    e_r = jnp.minimum(picked[0], N_EXPERTS - 1)
    local = r - (po[0] + picked[1])
    valid_r = (local < counts[0] + picked[2]) & (r < pend[-1])
    a_r = order[jnp.clip(co[0] + picked[3] + local, 0, na_all - 1)]
    tok_r = jnp.where(valid_r, jnp.where(a_r >= n, a_r - n, a_r), r % n)
    w_r = jnp.where(valid_r, w_flat[a_r], 0.0)
    n_act = (pend[-1] // tm).astype(i32)
    tile_e = e_r[::tm]
    te = jnp.where(jnp.arange(n_tiles, dtype=i32) < n_act, tile_e, tile_e[jnp.maximum(n_act - 1, 0)])
    inv = jnp.argsort(order).astype(i32)
    pos_a = jnp.dot((po - co).astype(f32)[None, :], onehot, precision=hp)[0].astype(i32) + inv
    return tok_r, w_r[:, None], te, n_act.reshape(1), pos_a


def _final_kernel(hmid_ref, ya_ref, yb_ref, out_ref):
    out_ref[...] = hmid_ref[...] + ya_ref[...].astype(f32) + yb_ref[...].astype(f32)


def _final(hmid, y2):
    n, d = hmid.shape
    tm = _row_tile(n, 512)
    nt = n // tm
    return pl.pallas_call(
        _final_kernel, grid=(nt,),
        in_specs=[pl.BlockSpec((tm, d), lambda i: (i, 0)), pl.BlockSpec((tm, d), lambda i: (i, 0)),
                  pl.BlockSpec((tm, d), lambda i: (i + nt, 0))],
        out_specs=pl.BlockSpec((tm, d), lambda i: (i, 0)),
        out_shape=jax.ShapeDtypeStruct((n, d), f32),
        compiler_params=_cparams(("parallel",)),
        name="final",
    )(hmid, y2, y2)


def _rope_tables(tp):
    pos = (jnp.arange(tp, dtype=f32) - FRONT)[:, None]

    def tab(dim, reps):
        inv = 1.0 / (ROPE_THETA ** (jnp.arange(0, dim, 2, dtype=f32) / dim))
        ang = pos * inv[None, :]
        return jnp.tile(jnp.cos(ang), (1, reps)), jnp.tile(jnp.sin(ang), (1, reps))

    cos, sin = tab(HEAD_DIM, 2)
    icos, isin = tab(IDX_DIM, 4)
    return cos, sin, icos, isin


def _split_w_in(w_in):
    parts, off = [], 0
    for s in IN_SPLITS:
        parts.append(w_in[..., off:off + s])
        off += s
    return parts


def kernel(x, meta, norm_mix_g, norm_ffn_g, w_in, a_qn_g, a_kn_g, b_qn_g, b_kn_g, b_f_bias,
           c_gate_w2, c_gate_b, c_on_g, w_out, r_group_w, r_group_b, r_exp_w, r_exp_b,
           e_w1, e_w3, e_w2):
    bsz, n_seq, d = x.shape
    depth = w_in.shape[0]
    t_real = n_seq + N_META
    k_top = min(TOPK_MAX, n_seq // 4)
    tp = -(-(FRONT + t_real) // BLK) * BLK
    n = bsz * tp

    (waq, wak, wav, wiq, wik, wiw, wbq, wbk, wbv, wbf, wcq, wck, wcv, wcr, wcg) = _split_w_in(w_in)
    zc = lambda w: jnp.zeros((depth, d, w), w_in.dtype)
    wcat = jnp.concatenate([waq, wak, wav, wiq, wik, zc(LANES - IDX_DIM), zc(LANES),
                            wbq, wbk, wbv, wcq, wck, wcv, wcg], axis=-1).astype(bf16)
    ws = jnp.concatenate([wiw, wbf, wcr, zc(LANES - S_CR - C_GATE_RANK)], axis=-1).astype(bf16)
    wo = w_out.astype(bf16)
    wr = jnp.concatenate([r_group_w, zc(R_EXP - N_GROUPS), r_exp_w, zc(LANES - R_EXP - N_EXPERTS)],
                         axis=-1).astype(bf16)
    zl = lambda w: jnp.zeros((depth, w), f32)
    br = jnp.concatenate([r_group_b, zl(R_EXP - N_GROUPS), r_exp_b, zl(LANES - R_EXP - N_EXPERTS)],
                         axis=-1)[:, None, :]
    fbias = jnp.concatenate([zl(S_BF), b_f_bias, zl(LANES - S_BF - B_HEADS)], axis=-1)[:, None, :]
    dk = C_HEADS * C_DK
    w2p = jnp.concatenate([jnp.zeros((depth, S_CR, dk), f32), c_gate_w2,
                           jnp.zeros((depth, LANES - S_CR - C_GATE_RANK, dk), f32)], axis=1).astype(bf16)
    gb = c_gate_b[:, None, :]
    og = c_on_g[:, None, :]
    g_mix = norm_mix_g[:, None, :]
    g_ffn = norm_ffn_g[:, None, :]
    gaq, gak, gbq, gbk = (g[:, None, :] for g in (a_qn_g, a_kn_g, b_qn_g, b_kn_g))
    emat = (jnp.arange(dk, dtype=i32)[:, None] // C_DK
            == jnp.arange(C_WIDTH, dtype=i32)[None, :] // C_DV).astype(bf16)
    tabs = _rope_tables(tp)

    h0 = jnp.concatenate([
        jnp.zeros((bsz, FRONT, d), f32),
        jnp.broadcast_to(meta[None].astype(f32), (bsz, N_META, d)),
        x.astype(f32),
        jnp.zeros((bsz, tp - FRONT - t_real, d), f32)], axis=1).reshape(n, d)
    n_tiles = -(-(EXP_TOPK_CONST * n + N_EXPERTS * (MOE_TM - 1)) // MOE_TM)

    def layer(li, carry):
        hmid, y2 = carry
        l = jnp.reshape(jnp.asarray(li, i32), (1,))
        h, zcat, zs = _inproj(l, hmid, y2, g_mix, wcat, ws)
        zcat3 = zcat.reshape(bsz, tp, N_CAT)
        zs3 = zs.reshape(bsz, tp, LANES)
        akr, ikr, bka = _kprep(l, zcat3, zs3, tabs, gak, gbk, fbias, t_real)
        oa = _dsa(l, zcat3, zs3, akr, ikr, tabs, gaq, t_real, k_top)
        ob = _fattn(l, zcat3, bka, gbq)
        oc = _gla(l, zcat3, zs3, w2p, gb, og, emat, t_real)
        hmid2, u, eid, ew = _outproj(l, oa.reshape(n, A_WIDTH), ob.reshape(n, B_WIDTH),
                                     oc.reshape(n, C_WIDTH), h, wo, g_ffn, wr, br, t_real, tp)
        tok_r, w_r, te, n_act, pos_a = _route(eid, ew, n_tiles)
        ys = _moe(l, te, n_act, jnp.take(u, tok_r, axis=0, mode="clip"), e_w1, e_w3, e_w2, w_r)
        return hmid2, jnp.take(ys, pos_a, axis=0, mode="clip")

    y0 = lax.optimization_barrier(jnp.zeros((EXP_TOPK_CONST * n, d), bf16))
    hmid, y2 = lax.fori_loop(0, depth, layer, (h0, y0))
    out = _final(hmid, y2).reshape(bsz, tp, d)
    return out[:, FRONT + N_META:FRONT + t_real].astype(x.dtype)
```

```python
import functools

import jax
import jax.numpy as jnp
from jax import lax
from jax.experimental import pallas as pl
from jax.experimental.pallas import tpu as pltpu

f32 = jnp.float32
bf16 = jnp.bfloat16
i32 = jnp.int32

D_MODEL = 2048
CHUNK = 64
N_META = 16
ROPE_THETA = 10000.0
EPS = 1e-6
HEAD_DIM = 128
A_HEADS = 6
IDX_HEADS = 16
IDX_DIM = 64
TOPK_MAX = 256
B_HEADS = 6
C_HEADS = 4
C_DK = 64
C_DV = 128
C_GATE_RANK = 16
C_TAU = 16.0
N_GROUPS = 4
EXP_PER_GROUP = 8
N_EXPERTS = N_GROUPS * EXP_PER_GROUP
D_EXPERT = 512
A_WIDTH = A_HEADS * HEAD_DIM
B_WIDTH = B_HEADS * HEAD_DIM
C_WIDTH = C_HEADS * C_DV
IN_SPLITS = (A_WIDTH, HEAD_DIM, HEAD_DIM, IDX_HEADS * IDX_DIM, IDX_DIM, IDX_HEADS,
             B_WIDTH, B_WIDTH, B_WIDTH, B_HEADS,
             C_HEADS * C_DK, C_HEADS * C_DK, C_WIDTH, C_GATE_RANK, C_WIDTH)

LANES = 128
SUBLANES = 8
FRONT = (-N_META) % CHUNK
BLK = 128
T_AQ, T_AK, T_AV, T_IQ, T_IK = 0, 6, 7, 8, 16
T_BQ, T_BK, T_BV = 18, 24, 30
T_CQ, T_CK, T_CV, T_CG = 36, 38, 40, 44
N_CAT = 48 * LANES
S_IW, S_BF, S_CR = 0, 16, 22
R_GRP, R_EXP = 0, 32
INPROJ_TN = 1536
GLA_C = 32
DSA_G = 3
MOE_TM = 512
EXP_TOPK_CONST = 2
NEG = -1e30
LOG2E = 1.4426950408889634
INT_MIN = -2 ** 31
VMEM_LIMIT = 56 * 1024 * 1024


def _cparams(sem):
    return pltpu.CompilerParams(dimension_semantics=sem, vmem_limit_bytes=VMEM_LIMIT)


def _row_tile(n, cap):
    t = cap
    while n % t:
        t //= 2
    return t


def _log_sigmoid(x):
    return jnp.minimum(x, 0.0) - jnp.log(1.0 + jnp.exp(-jnp.abs(x)))


def _dot(a, b):
    return jnp.dot(a, b, preferred_element_type=f32)


def _dot_nt(a, b):
    return lax.dot_general(a, b, (((1,), (1,)), ((), ())), preferred_element_type=f32)


def _dot_tn(a, b):
    return lax.dot_general(a, b, (((0,), (0,)), ((), ())), preferred_element_type=f32)


def _split3(x):
    hi = x.astype(bf16)
    r1 = x - hi.astype(f32)
    mid = r1.astype(bf16)
    lo = (r1 - mid.astype(f32)).astype(bf16)
    return hi, mid, lo


def _rope128(x, cos, sin, lane):
    return x * cos + pltpu.roll(x, 64, 1) * jnp.where(lane < 64, -sin, sin)


def _rope64(x, cos, sin, lane):
    low = (lane & 63) < 32
    return (x * cos + pltpu.roll(x, 32, 1) * jnp.where(low, 0.0, sin)
            + pltpu.roll(x, 96, 1) * jnp.where(low, -sin, 0.0))


def _rms_gain(x, g):
    return x * lax.rsqrt(jnp.mean(x * x, axis=-1, keepdims=True) + EPS) * g


def _prenorm_kernel(l_ref, hmid_ref, ya_ref, yb_ref, g_ref, ws_ref, h_ref, xn_ref, zs_ref):
    h = hmid_ref[...] + ya_ref[...].astype(f32) + yb_ref[...].astype(f32)
    h_ref[...] = h
    xn = _rms_gain(h, g_ref[...]).astype(bf16)
    xn_ref[...] = xn
    zs_ref[...] = _dot(xn, ws_ref[...])


def _inproj_kernel(l_ref, xn_ref, w_ref, z_ref):
    z_ref[...] = _dot(xn_ref[...], w_ref[...]).astype(bf16)


def _inproj(l, hmid, y2, gain, wcat, ws):
    n, d = hmid.shape
    tm = _row_tile(n, 512)
    nt = n // tm
    row = lambda w: pl.BlockSpec((tm, w), lambda i, l: (i, 0))
    h, xn, zs = pl.pallas_call(
        _prenorm_kernel,
        grid_spec=pltpu.PrefetchScalarGridSpec(
            num_scalar_prefetch=1, grid=(nt,),
            in_specs=[row(d), row(d),
                      pl.BlockSpec((tm, d), lambda i, l: (i + nt, 0)),
                      pl.BlockSpec((None, 1, d), lambda i, l: (l[0], 0, 0)),
                      pl.BlockSpec((None, d, LANES), lambda i, l: (l[0], 0, 0))],
            out_specs=[row(d), row(d), row(LANES)]),
        out_shape=[jax.ShapeDtypeStruct((n, d), f32), jax.ShapeDtypeStruct((n, d), bf16),
                   jax.ShapeDtypeStruct((n, LANES), f32)],
        compiler_params=_cparams(("parallel",)),
        name="prenorm",
    )(l, hmid, y2, y2, gain, ws)
    tn = INPROJ_TN
    zcat = pl.pallas_call(
        _inproj_kernel,
        grid_spec=pltpu.PrefetchScalarGridSpec(
            num_scalar_prefetch=1, grid=(N_CAT // tn, nt),
            in_specs=[pl.BlockSpec((tm, d), lambda j, i, l: (i, 0)),
                      pl.BlockSpec((None, d, tn), lambda j, i, l: (l[0], 0, j))],
            out_specs=pl.BlockSpec((tm, tn), lambda j, i, l: (i, j))),
        out_shape=jax.ShapeDtypeStruct((n, N_CAT), bf16),
        compiler_params=_cparams(("parallel", "parallel")),
        name="inproj",
    )(l, xn, wcat)
    return h, zcat, zs


def _kprep_kernel(l_ref, ak_ref, ik_ref, bk_ref, zs_ref,
                  cos_ref, sin_ref, icos_ref, isin_ref, gak_ref, gbk_ref, fb_ref,
                  akr_ref, ikr_ref, bka_ref, carry_ref, *, t_real):
    k = pl.program_id(1)
    lane = lax.broadcasted_iota(i32, (BLK, LANES), 1)
    row = lax.broadcasted_iota(i32, (BLK, LANES), 0)
    pos = k * BLK + row
    valid = (pos >= FRONT) & (pos < FRONT + t_real)

    x = ak_ref[...].astype(f32)
    akr_ref[...] = _rope128(_rms_gain(x, gak_ref[...]), cos_ref[...], sin_ref[...], lane).astype(bf16)
    ikr_ref[...] = _rope64(ik_ref[...].astype(f32), icos_ref[...], isin_ref[...], lane).astype(bf16)

    @pl.when(k == 0)
    def _():
        carry_ref[...] = jnp.zeros_like(carry_ref)

    lf = jnp.where(valid, _log_sigmoid(zs_ref[...] + fb_ref[...]), 0.0)
    tri = (row >= lane).astype(bf16)
    hi, mid, lo = _split3(lf)
    fcum = _dot(tri, hi) + _dot(tri, mid) + _dot(tri, lo) + carry_ref[...]
    carry_ref[...] = fcum[BLK - 1:BLK, :]

    fs = jnp.where(valid, fcum * (-(HEAD_DIM ** 0.5)), NEG)
    for h in range(B_HEADS):
        sl = slice(h * LANES, (h + 1) * LANES)
        bka_ref[h, :, 0:LANES] = _rms_gain(bk_ref[:, sl].astype(f32), gbk_ref[...]).astype(bf16)
        p0, p1, p2 = _split3(fs[:, S_BF + h:S_BF + h + 1])
        aug = jnp.where(lane == 0, p0.astype(f32),
                        jnp.where(lane == 1, p1.astype(f32), jnp.where(lane == 2, p2.astype(f32), 0.0)))
        bka_ref[h, :, LANES:2 * LANES] = aug.astype(bf16)


def _kprep(l, zcat3, zs3, tabs, gak, gbk, fbias, t_real):
    bsz, tp, _ = zcat3.shape
    nkb = tp // BLK
    cos, sin, icos, isin = tabs
    tile = lambda c: pl.BlockSpec((None, BLK, LANES), lambda b, k, l, c=c: (b, k, c))
    wide = lambda c: pl.BlockSpec((None, BLK, B_WIDTH), lambda b, k, l, c=c: (b, k, c))
    tab = pl.BlockSpec((BLK, LANES), lambda b, k, l: (k, 0))
    gain = pl.BlockSpec((None, 1, LANES), lambda b, k, l: (l[0], 0, 0))
    return pl.pallas_call(
        functools.partial(_kprep_kernel, t_real=t_real),
        grid_spec=pltpu.PrefetchScalarGridSpec(
            num_scalar_prefetch=1, grid=(bsz, nkb),
            in_specs=[tile(T_AK), tile(T_IK),
                      wide(T_BK * LANES // B_WIDTH),
                      pl.BlockSpec((None, BLK, LANES), lambda b, k, l: (b, k, 0)),
                      tab, tab, tab, tab, gain, gain, gain],
            out_specs=[
                pl.BlockSpec((None, BLK, LANES), lambda b, k, l: (b, k, 0)),
                pl.BlockSpec((None, BLK, LANES), lambda b, k, l: (b, k, 0)),
                pl.BlockSpec((None, B_HEADS, BLK, 2 * LANES), lambda b, k, l: (b, 0, k, 0)),
            ],
            scratch_shapes=[pltpu.VMEM((1, LANES), f32)]),
        out_shape=[jax.ShapeDtypeStruct((bsz, tp, LANES), bf16),
                   jax.ShapeDtypeStruct((bsz, tp, LANES), bf16),
                   jax.ShapeDtypeStruct((bsz, B_HEADS, tp, 2 * LANES), bf16)],
        compiler_params=_cparams(("parallel", "arbitrary")),
        name="kprep",
    )(l, zcat3, zcat3, zcat3, zs3, cos, sin, icos, isin, gak, gbk, fbias)


def _dsa_kernel(l_ref, aq_ref, iq_ref, zs_ref, akr_ref, ikr_ref, av_ref,
                cos_ref, sin_ref, icos_ref, isin_ref, gq_ref, out_ref,
                key_ref, iqs_ref, q6_ref, iwt_ref, m_ref, s_ref, acc_ref, stq_ref, *, t_real, k_top):
    i = pl.program_id(1)
    nk = i + 1
    lane = lax.broadcasted_iota(i32, (BLK, LANES), 1)

    for h in range(A_HEADS):
        x = aq_ref[:, h * LANES:(h + 1) * LANES].astype(f32)
        xr = _rope128(_rms_gain(x, gq_ref[...]), cos_ref[...], sin_ref[...], lane)
        q6_ref[h * BLK:(h + 1) * BLK, :] = xr.astype(bf16)
    for t in range(IDX_HEADS // 2):
        x = iq_ref[:, t * LANES:(t + 1) * LANES].astype(f32)
        xr = _rope64(x, icos_ref[...], isin_ref[...], lane)
        iqs_ref[(2 * t) * BLK:(2 * t + 1) * BLK, :] = xr.astype(bf16)
        iqs_ref[(2 * t + 1) * BLK:(2 * t + 2) * BLK, :] = pltpu.roll(xr, 64, 1).astype(bf16)
    iwt_ref[...] = (zs_ref[...] * (IDX_HEADS ** -0.5 * IDX_DIM ** -0.5)).T

    gb = DSA_G * BLK
    ng = lax.div(nk, jnp.int32(DSA_G))
    nr = nk - ng * DSA_G

    def over_keys(fn, init, merge=()):
        c, g0 = init, 0
        for m in merge:
            nm = lax.div(ng - g0, jnp.int32(m))
            c = lax.fori_loop(0, nm, lambda g, c, g0=g0, m=m: fn(pl.multiple_of((g0 + g * m) * gb, gb), m * gb, c), c)
            g0 = g0 + nm * m
        c = lax.fori_loop(g0, ng, lambda g, c: fn(pl.multiple_of(g * gb, gb), gb, c), c)
        return lax.fori_loop(0, nr, lambda r, c: fn(pl.multiple_of((ng * DSA_G + r) * BLK, BLK), BLK, c), c)

    def score_rows(k0, nrows, carry):
        dt = _dot_nt(ikr_ref[pl.ds(k0, nrows), :], iqs_ref[...])
        s = jnp.zeros((nrows, LANES), f32)
        for h in range(IDX_HEADS):
            s = s + iwt_ref[h:h + 1, :] * jnp.maximum(dt[:, h * LANES:(h + 1) * LANES], 0.0)
        kpos = k0 + lax.broadcasted_iota(i32, (nrows, LANES), 0)
        qpos = i * BLK + lax.broadcasted_iota(i32, (nrows, LANES), 1)
        adm = ((kpos >> 6) <= (qpos >> 6)) & (kpos >= FRONT) & (kpos < FRONT + t_real)
        bits = lax.bitcast_convert_type(s, i32)
        key = bits ^ ((bits >> 31) & 0x7FFFFFFF)
        key_ref[pl.ds(k0, nrows), :] = jnp.where(adm, key, INT_MIN)
        return carry

    over_keys(score_rows, 0, merge=(2,))

    def bit_body(t, thr_u):
        bit = jnp.left_shift(jnp.int32(1), 31 - t)
        cand_u = thr_u | bit
        cand_s = cand_u ^ INT_MIN

        def count_rows(k0, nrows, c):
            hit = (key_ref[pl.ds(k0, nrows), :] >= cand_s).astype(i32)
            for j in range(nrows // BLK):
                c = c + hit[j * BLK:(j + 1) * BLK, :]
            return c

        cnt = over_keys(count_rows, jnp.zeros((BLK, LANES), i32))
        tot = jnp.sum(cnt.astype(f32), axis=0, keepdims=True)
        return jnp.where(tot >= k_top, cand_u, thr_u)

    thr_u = lax.fori_loop(0, 32, bit_body, jnp.zeros((1, LANES), i32))
    thr_s = jnp.maximum(thr_u ^ INT_MIN, INT_MIN + 1)

    m_ref[...] = jnp.full(m_ref.shape, NEG, f32)
    s_ref[...] = jnp.zeros(s_ref.shape, f32)
    acc_ref[...] = jnp.zeros(acc_ref.shape, f32)
    c2 = HEAD_DIM ** -0.5 * LOG2E

    def attn_rows(k0, nrows, carry, st=None):
        if st is None:
            st = _dot_nt(akr_ref[pl.ds(k0, nrows), :], q6_ref[...])
        bias = jnp.where(key_ref[pl.ds(k0, nrows), :] >= thr_s, 0.0, NEG)
        m_news = []
        for h in range(A_HEADS):
            sl = slice(h * LANES, (h + 1) * LANES)
            sh = st[:, sl] + bias
            stq_ref[0, 0:nrows, sl] = sh
            m_news.append(jnp.maximum(m_ref[:, sl], jnp.max(sh, axis=0, keepdims=True)))
        ps, alphas = [], []
        for h in range(A_HEADS):
            sl = slice(h * LANES, (h + 1) * LANES)
            sh = stq_ref[0, 0:nrows, sl]
            m_old = m_ref[:, sl]
            m_new = m_news[h]
            alpha = jnp.exp2((m_old - m_new) * c2)
            p = jnp.exp2((sh - m_new) * c2)
            s_ref[:, sl] = s_ref[:, sl] * alpha + jnp.sum(p, axis=0, keepdims=True)
            m_ref[:, sl] = m_new
            alphas.append(alpha)
            ps.append(p.astype(bf16))
        pv = _dot_tn(av_ref[pl.ds(k0, nrows), :], jnp.concatenate(ps, axis=1))
        acc_ref[...] = acc_ref[...] * jnp.concatenate(alphas, axis=1) + pv
        return carry

    over_keys(attn_rows, 0, merge=(4, 2))

    for h in range(A_HEADS):
        sl = slice(h * LANES, (h + 1) * LANES)
        o = acc_ref[:, sl] / jnp.maximum(s_ref[:, sl], 1e-30)
        out_ref[:, sl] = o.T.astype(bf16)


def _dsa(l, zcat3, zs3, akr, ikr, tabs, gaq, t_real, k_top):
    bsz, tp, _ = zcat3.shape
    nkb = tp // BLK
    cos, sin, icos, isin = tabs
    tab = pl.BlockSpec((BLK, LANES), lambda b, i, l: (i, 0))
    full = pl.BlockSpec((None, tp, LANES), lambda b, i, l: (b, 0, 0))
    return pl.pallas_call(
        functools.partial(_dsa_kernel, t_real=t_real, k_top=k_top),
        grid_spec=pltpu.PrefetchScalarGridSpec(
            num_scalar_prefetch=1, grid=(bsz, nkb),
            in_specs=[
                pl.BlockSpec((None, BLK, A_WIDTH), lambda b, i, l: (b, i, 0)),
                pl.BlockSpec((None, BLK, IDX_HEADS * IDX_DIM), lambda b, i, l: (b, i, T_IQ * LANES // (IDX_HEADS * IDX_DIM))),
                pl.BlockSpec((None, BLK, LANES), lambda b, i, l: (b, i, 0)),
                full, full,
                pl.BlockSpec((None, tp, LANES), lambda b, i, l: (b, 0, T_AV)),
                tab, tab, tab, tab,
                pl.BlockSpec((None, 1, LANES), lambda b, i, l: (l[0], 0, 0)),
            ],
            out_specs=pl.BlockSpec((None, BLK, A_WIDTH), lambda b, i, l: (b, i, 0)),
            scratch_shapes=[
                pltpu.VMEM((tp, LANES), i32),
                pltpu.VMEM((IDX_HEADS * BLK, LANES), bf16),
                pltpu.VMEM((A_HEADS * BLK, LANES), bf16),
                pltpu.VMEM((LANES, BLK), f32),
                pltpu.VMEM((1, A_HEADS * BLK), f32),
                pltpu.VMEM((1, A_HEADS * BLK), f32),
                pltpu.VMEM((HEAD_DIM, A_HEADS * BLK), f32),
                pltpu.VMEM((1, 4 * DSA_G * BLK, A_HEADS * BLK), f32),
            ]),
        out_shape=jax.ShapeDtypeStruct((bsz, tp, A_WIDTH), bf16),
        compiler_params=_cparams(("parallel", "arbitrary")),
        name="dsa",
    )(l, zcat3, zcat3, zs3, akr, ikr, zcat3, cos, sin, icos, isin, gaq)


def _fattn_kernel(l_ref, bq_ref, bka_ref, bv_ref, gq_ref, out_ref, qa_ref, m_ref, s_ref, acc_ref, st_ref):
    i = pl.program_id(1)
    fb = bq_ref.shape[0]
    ones3 = jnp.where(lax.broadcasted_iota(i32, (fb, LANES), 1) < 3, 1.0, 0.0).astype(bf16)
    for h in range(B_HEADS):
        sl = slice(h * LANES, (h + 1) * LANES)
        qa_ref[h, :, 0:LANES] = _rms_gain(bq_ref[:, sl].astype(f32), gq_ref[...]).astype(bf16)
        qa_ref[h, :, LANES:2 * LANES] = ones3
    m_ref[...] = jnp.full(m_ref.shape, NEG, f32)
    s_ref[...] = jnp.zeros(s_ref.shape, f32)
    acc_ref[...] = jnp.zeros(acc_ref.shape, f32)
    c2 = HEAD_DIM ** -0.5 * LOG2E
    causal = lax.broadcasted_iota(i32, (fb, fb), 0) <= lax.broadcasted_iota(i32, (fb, fb), 1)

    def step(kb, diagonal):
        k0 = pl.multiple_of(kb * fb, fb)
        m_news = []
        for h in range(B_HEADS):
            st = _dot_nt(bka_ref[h, pl.ds(k0, fb), :], qa_ref[h])
            if diagonal:
                st = jnp.where(causal, st, NEG)
            st_ref[h] = st
            m_news.append(jnp.maximum(m_ref[h], jnp.max(st, axis=0, keepdims=True)))
        for h in range(B_HEADS):
            sl = slice(h * LANES, (h + 1) * LANES)
            st = st_ref[h]
            m_old = m_ref[h]
            m_new = m_news[h]
            alpha = jnp.exp2((m_old - m_new) * c2)
            p = jnp.exp2((st - m_new) * c2)
            s_ref[h] = s_ref[h] * alpha + jnp.sum(p, axis=0, keepdims=True)
            m_ref[h] = m_new
            acc_ref[h] = acc_ref[h] * alpha + _dot_tn(bv_ref[pl.ds(k0, fb), sl], p.astype(bf16))

    def body(kb, carry):
        step(kb, False)
        return carry

    lax.fori_loop(0, i, body, 0)
    step(i, True)
    for h in range(B_HEADS):
        sl = slice(h * LANES, (h + 1) * LANES)
        out_ref[:, sl] = (acc_ref[h] / s_ref[h]).T.astype(bf16)


def _fattn(l, zcat3, bka, gbq):
    bsz, tp, _ = zcat3.shape
    fb = _seq_tile(tp, 384)
    once = pl.Buffered(1)
    return pl.pallas_call(
        _fattn_kernel,
        grid_spec=pltpu.PrefetchScalarGridSpec(
            num_scalar_prefetch=1, grid=(bsz, tp // fb),
            in_specs=[
                pl.BlockSpec((None, fb, B_WIDTH), lambda b, i, l: (b, i, T_BQ * LANES // B_WIDTH)),
                pl.BlockSpec((None, B_HEADS, tp, 2 * LANES), lambda b, i, l: (b, 0, 0, 0), pipeline_mode=once),
                pl.BlockSpec((None, tp, B_WIDTH), lambda b, i, l: (b, 0, T_BV * LANES // B_WIDTH),
                             pipeline_mode=once),
                pl.BlockSpec((None, 1, LANES), lambda b, i, l: (l[0], 0, 0)),
            ],
            out_specs=pl.BlockSpec((None, fb, B_WIDTH), lambda b, i, l: (b, i, 0)),
            scratch_shapes=[
                pltpu.VMEM((B_HEADS, fb, 2 * LANES), bf16),
                pltpu.VMEM((B_HEADS, 1, fb), f32),
                pltpu.VMEM((B_HEADS, 1, fb), f32),
                pltpu.VMEM((B_HEADS, HEAD_DIM, fb), f32),
                pltpu.VMEM((B_HEADS, fb, fb), f32),
            ]),
        out_shape=jax.ShapeDtypeStruct((bsz, tp, B_WIDTH), bf16),
        compiler_params=_cparams(("parallel", "arbitrary")),
        name="fattn",
    )(l, zcat3, bka, zcat3, gbq)


def _gla_kernel(l_ref, cq_ref, ck_ref, cv_ref, cg_ref, zs_ref, w2_ref, gb_ref, og_ref, e_ref,
                out_ref, st_ref, oi_ref, pp_ref, rr_ref, *, t_real):
    c = pl.program_id(0)
    nb, cc, dk = cq_ref.shape

    @pl.when(c == 0)
    def _():
        st_ref[...] = jnp.zeros_like(st_ref)

    rowc = lax.broadcasted_iota(i32, (cc, 1), 0)
    pos = c * cc + rowc
    valid = (pos >= FRONT) & (pos < FRONT + t_real)
    tri = (lax.broadcasted_iota(i32, (cc, cc), 0) >= lax.broadcasted_iota(i32, (cc, cc), 1)).astype(bf16)
    lane_k = lax.broadcasted_iota(i32, (1, dk), 1)
    head_masks = [(lane_k >= h * C_DK) & (lane_k < (h + 1) * C_DK) for h in range(C_HEADS)]

    used = [(r // SUBLANES + 1) * SUBLANES for r in range(cc)]

    for b in range(nb):
        x = _dot(zs_ref[b].astype(bf16), w2_ref[...]) + gb_ref[...]
        la = jnp.where(valid, _log_sigmoid(x) * (1.0 / C_TAU), 0.0)
        hi, mid, lo = _split3(la)
        bc = _dot(tri, hi) + _dot(tri, mid) + _dot(tri, lo)
        q = cq_ref[b].astype(f32) * (C_DK ** -0.5)
        k = ck_ref[b].astype(f32)
        v = cv_ref[b]
        bc2 = bc * LOG2E
        blast2 = bc2[cc - 1:cc, :]
        qe = q * jnp.exp2(bc2)
        ke = k * jnp.exp2(blast2 - bc2)
        st = st_ref[b]
        stb = st.astype(bf16)
        oi_ref[b] = jnp.concatenate(
            [_dot_nt(jnp.where(head_masks[h], qe, 0.0).astype(bf16), stb) for h in range(C_HEADS)], axis=1)
        new_st = st * jnp.exp2(blast2)
        for h in range(C_HEADS):
            km = jnp.where(head_masks[h], ke, 0.0).astype(bf16)
            new_st = new_st + _dot_tn(v[:, h * C_DV:(h + 1) * C_DV], km)
        st_ref[b] = new_st
        for r in range(cc):
            nu = used[r]
            dec = jnp.exp2(jnp.minimum(bc2[r:r + 1, :] - bc2[0:nu, :], 0.0))
            pr = jnp.where(rowc[0:nu] <= r, q[r:r + 1, :] * k[0:nu, :] * dec, 0.0)
            if nu < cc:
                pr = jnp.concatenate([pr, jnp.zeros((cc - nu, dk), f32)], axis=0)
            pp_ref[b, r * cc:(r + 1) * cc, :] = pr.astype(bf16)

    for b in range(nb):
        rr_ref[b] = _dot(pp_ref[b], e_ref[...])

    for b in range(nb):
        vf = cv_ref[b].astype(f32)
        for r in range(cc):
            nu = used[r]
            oi_ref[b, r:r + 1, :] += jnp.sum(rr_ref[b, r * cc:r * cc + nu, :] * vf[0:nu, :], axis=0, keepdims=True)
        o = oi_ref[b]
        g = cg_ref[b].astype(f32)
        gs = g * (1.0 / (1.0 + jnp.exp(-g)))
        for h in range(C_HEADS):
            sl = slice(h * C_DV, (h + 1) * C_DV)
            out_ref[b, :, sl] = (_rms_gain(o[:, sl], og_ref[...]) * gs[:, sl]).astype(bf16)


def _gla(l, zcat3, zs3, w2p, gb, og, emat, t_real):
    bsz, tp, _ = zcat3.shape
    cc = GLA_C
    dk = C_HEADS * C_DK
    return pl.pallas_call(
        functools.partial(_gla_kernel, t_real=t_real),
        grid_spec=pltpu.PrefetchScalarGridSpec(
            num_scalar_prefetch=1, grid=(tp // cc,),
            in_specs=[
                pl.BlockSpec((bsz, cc, dk), lambda c, l: (0, c, T_CQ * LANES // dk)),
                pl.BlockSpec((bsz, cc, dk), lambda c, l: (0, c, T_CK * LANES // dk)),
                pl.BlockSpec((bsz, cc, C_WIDTH), lambda c, l: (0, c, T_CV * LANES // C_WIDTH)),
                pl.BlockSpec((bsz, cc, C_WIDTH), lambda c, l: (0, c, T_CG * LANES // C_WIDTH)),
                pl.BlockSpec((bsz, cc, LANES), lambda c, l: (0, c, 0)),
                pl.BlockSpec((None, LANES, dk), lambda c, l: (l[0], 0, 0)),
                pl.BlockSpec((None, 1, dk), lambda c, l: (l[0], 0, 0)),
                pl.BlockSpec((None, 1, C_DV), lambda c, l: (l[0], 0, 0)),
                pl.BlockSpec((dk, C_WIDTH), lambda c, l: (0, 0)),
            ],
            out_specs=pl.BlockSpec((bsz, cc, C_WIDTH), lambda c, l: (0, c, 0)),
            scratch_shapes=[pltpu.VMEM((bsz, C_DV, dk), f32), pltpu.VMEM((bsz, cc, C_WIDTH), f32),
                            pltpu.VMEM((bsz, cc * cc, dk), bf16), pltpu.VMEM((bsz, cc * cc, C_WIDTH), f32)]),
        out_shape=jax.ShapeDtypeStruct((bsz, tp, C_WIDTH), bf16),
        compiler_params=_cparams(("arbitrary",)),
        name="gla",
    )(l, zcat3, zcat3, zcat3, zcat3, zs3, w2p, gb, og, emat)


def _outproj_kernel(l_ref, oa_ref, ob_ref, oc_ref, h_ref, wo_ref, g_ref, wr_ref, br_ref,
                    hmid_ref, u_ref, eid_ref, ew_ref, *, t_real):
    tm = h_ref.shape[0]
    mix = (_dot(oa_ref[...], wo_ref[0:A_WIDTH, :])
           + _dot(ob_ref[...], wo_ref[A_WIDTH:A_WIDTH + B_WIDTH, :])
           + _dot(oc_ref[...], wo_ref[A_WIDTH + B_WIDTH:, :]))
    pos = pl.program_id(1) * tm + lax.broadcasted_iota(i32, (tm, 1), 0)
    valid = (pos >= FRONT) & (pos < FRONT + t_real)
    hm = h_ref[...] + jnp.where(valid, mix, 0.0)
    hmid_ref[...] = hm
    u = _rms_gain(hm, g_ref[...]).astype(bf16)
    u_ref[...] = u

    logits = _dot(u, wr_ref[...]) + br_ref[...]
    lane = lax.broadcasted_iota(i32, (tm, LANES), 1)
    lanef = lane.astype(f32)
    big = float(4 * LANES)
    first = lambda hit: jnp.min(jnp.where(hit, lanef, big), axis=-1, keepdims=True).astype(i32)
    gl = jnp.where(lane < R_GRP + N_GROUPS, logits, -jnp.inf)
    gmax = jnp.max(gl, axis=-1, keepdims=True)
    g_p = 1.0 / jnp.sum(jnp.exp(gl - gmax), axis=-1, keepdims=True)
    g_i = first(gl == gmax)
    e_lane = lane - R_EXP
    emask = (e_lane >= 0) & (e_lane < N_EXPERTS) & ((e_lane >> 3) == g_i)
    el = jnp.where(emask, logits, -jnp.inf)
    m1 = jnp.max(el, axis=-1, keepdims=True)
    i1 = first(el == m1)
    el2 = jnp.where(lane == i1, -jnp.inf, el)
    m2 = jnp.max(el2, axis=-1, keepdims=True)
    i2 = first(el2 == m2)
    r = jnp.exp(m2 - m1)
    w1 = g_p / (1.0 + r)
    w2 = g_p * r / (1.0 + r)
    eid_ref[...] = jnp.where(lane == 0, i1 - R_EXP, jnp.where(lane == 1, i2 - R_EXP, 0))
    ew_ref[...] = jnp.where(lane == 0, w1, jnp.where(lane == 1, w2, 0.0))


def _seq_tile(tp, cap):
    return max(t for t in range(BLK, cap + 1, BLK) if tp % t == 0)


def _outproj(l, oa, ob, oc, h, wo, gain, wr, br, t_real, tp):
    n, d = h.shape
    tm = _seq_tile(tp, 384)
    nj = tp // tm
    row = lambda w: pl.BlockSpec((tm, w), lambda b, j, l: (b * nj + j, 0))
    return pl.pallas_call(
        functools.partial(_outproj_kernel, t_real=t_real),
        grid_spec=pltpu.PrefetchScalarGridSpec(
            num_scalar_prefetch=1, grid=(n // tp, nj),
            in_specs=[row(A_WIDTH), row(B_WIDTH), row(C_WIDTH), row(d),
                      pl.BlockSpec((None, d, d), lambda b, j, l: (l[0], 0, 0)),
                      pl.BlockSpec((None, 1, d), lambda b, j, l: (l[0], 0, 0)),
                      pl.BlockSpec((None, d, LANES), lambda b, j, l: (l[0], 0, 0)),
                      pl.BlockSpec((None, 1, LANES), lambda b, j, l: (l[0], 0, 0))],
            out_specs=[row(d), row(d), row(LANES), row(LANES)]),
        out_shape=[jax.ShapeDtypeStruct((n, d), f32), jax.ShapeDtypeStruct((n, d), bf16),
                   jax.ShapeDtypeStruct((n, LANES), i32), jax.ShapeDtypeStruct((n, LANES), f32)],
        compiler_params=_cparams(("parallel", "parallel")),
        name="outproj",
    )(l, oa, ob, oc, h, wo, gain, wr, br)


def _moe_kernel(l_ref, te_ref, na_ref, xs_ref, w1_ref, w3_ref, w2_ref, rw_ref, ys_ref,
                w1b_ref, w3b_ref, w2b_ref):
    i = pl.program_id(0)

    @pl.when((i == 0) | (te_ref[i] != te_ref[jnp.maximum(i - 1, 0)]))
    def _():
        w1b_ref[...] = w1_ref[...].astype(bf16)
        w3b_ref[...] = w3_ref[...].astype(bf16)
        w2b_ref[...] = w2_ref[...].astype(bf16)

    @pl.when(i < na_ref[0])
    def _():
        x = xs_ref[...]
        h1 = _dot(x, w1b_ref[...])
        h3 = _dot(x, w3b_ref[...])
        hid = (h1 * (1.0 / (1.0 + jnp.exp(-h1))) * h3).astype(bf16)
        ys_ref[...] = (_dot(hid, w2b_ref[...]) * rw_ref[...]).astype(bf16)

    @pl.when(i >= na_ref[0])
    def _():
        ys_ref[...] = jnp.zeros_like(ys_ref)


def _moe(l, te, na, xs, w1, w3, w2, rw):
    p, d = xs.shape
    tm = MOE_TM
    return pl.pallas_call(
        _moe_kernel,
        grid_spec=pltpu.PrefetchScalarGridSpec(
            num_scalar_prefetch=3, grid=(p // tm,),
            in_specs=[
                pl.BlockSpec((tm, d), lambda i, l, te, na: (i, 0)),
                pl.BlockSpec((None, None, d, D_EXPERT), lambda i, l, te, na: (l[0], te[i], 0, 0)),
                pl.BlockSpec((None, None, d, D_EXPERT), lambda i, l, te, na: (l[0], te[i], 0, 0)),
                pl.BlockSpec((None, None, D_EXPERT, d), lambda i, l, te, na: (l[0], te[i], 0, 0)),
                pl.BlockSpec((tm, 1), lambda i, l, te, na: (i, 0)),
            ],
            out_specs=pl.BlockSpec((tm, d), lambda i, l, te, na: (i, 0)),
            scratch_shapes=[pltpu.VMEM((d, D_EXPERT), bf16), pltpu.VMEM((d, D_EXPERT), bf16),
                            pltpu.VMEM((D_EXPERT, d), bf16)]),
        out_shape=jax.ShapeDtypeStruct((p, d), bf16),
        compiler_params=_cparams(("arbitrary",)),
        name="moe",
    )(l, te, na, xs, w1, w3, w2, rw)


def _route(eid, ew, n_tiles):
    tm = MOE_TM
    p = n_tiles * tm
    n = eid.shape[0]
    e_flat = eid[:, :EXP_TOPK_CONST].T.reshape(-1)
    w_flat = ew[:, :EXP_TOPK_CONST].T.reshape(-1)
    na_all = e_flat.shape[0]
    hp = lax.Precision.HIGHEST
    ex = jnp.arange(N_EXPERTS, dtype=i32)[:, None]
    onehot = (ex == e_flat[None, :]).astype(f32)
    counts = jnp.sum(onehot, axis=1).astype(i32)
    pc = ((counts + tm - 1) // tm) * tm
    pend = jnp.cumsum(pc)
    po = pend - pc
    co = jnp.cumsum(counts) - counts
    order = jnp.argsort(e_flat, stable=True).astype(i32)
    r = jnp.arange(p, dtype=i32)
    step = (r[None, :] >= pend[:, None]).astype(f32)
    dlt = lambda v: jnp.concatenate([v[1:] - v[:-1], jnp.zeros((1,), v.dtype)]).astype(f32)
    tabs = jnp.stack([jnp.ones((N_EXPERTS,), f32), dlt(po), dlt(counts), dlt(co)])
    picked = jnp.dot(tabs, step, precision=hp).astype(i32)
    e_r = jnp.minimum(picked[0], N_EXPERTS - 1)
    local = r - (po[0] + picked[1])
    valid_r = (local < counts[0] + picked[2]) & (r < pend[-1])
    a_r = order[jnp.clip(co[0] + picked[3] + local, 0, na_all - 1)]
    tok_r = jnp.where(valid_r, jnp.where(a_r >= n, a_r - n, a_r), r % n)
    w_r = jnp.where(valid_r, w_flat[a_r], 0.0)
    n_act = (pend[-1] // tm).astype(i32)
    tile_e = e_r[::tm]
    te = jnp.where(jnp.arange(n_tiles, dtype=i32) < n_act, tile_e, tile_e[jnp.maximum(n_act - 1, 0)])
    inv = jnp.argsort(order).astype(i32)
    pos_a = jnp.dot((po - co).astype(f32)[None, :], onehot, precision=hp)[0].astype(i32) + inv
    return tok_r, w_r[:, None], te, n_act.reshape(1), pos_a


def _final_kernel(hmid_ref, ya_ref, yb_ref, out_ref):
    out_ref[...] = hmid_ref[...] + ya_ref[...].astype(f32) + yb_ref[...].astype(f32)


def _final(hmid, y2):
    n, d = hmid.shape
    tm = _row_tile(n, 512)
    nt = n // tm
    return pl.pallas_call(
        _final_kernel, grid=(nt,),
        in_specs=[pl.BlockSpec((tm, d), lambda i: (i, 0)), pl.BlockSpec((tm, d), lambda i: (i, 0)),
                  pl.BlockSpec((tm, d), lambda i: (i + nt, 0))],
        out_specs=pl.BlockSpec((tm, d), lambda i: (i, 0)),
        out_shape=jax.ShapeDtypeStruct((n, d), f32),
        compiler_params=_cparams(("parallel",)),
        name="final",
    )(hmid, y2, y2)


def _rope_tables(tp):
    pos = (jnp.arange(tp, dtype=f32) - FRONT)[:, None]

    def tab(dim, reps):
        inv = 1.0 / (ROPE_THETA ** (jnp.arange(0, dim, 2, dtype=f32) / dim))
        ang = pos * inv[None, :]
        return jnp.tile(jnp.cos(ang), (1, reps)), jnp.tile(jnp.sin(ang), (1, reps))

    cos, sin = tab(HEAD_DIM, 2)
    icos, isin = tab(IDX_DIM, 4)
    return cos, sin, icos, isin


def _split_w_in(w_in):
    parts, off = [], 0
    for s in IN_SPLITS:
        parts.append(w_in[..., off:off + s])
        off += s
    return parts


def kernel(x, meta, norm_mix_g, norm_ffn_g, w_in, a_qn_g, a_kn_g, b_qn_g, b_kn_g, b_f_bias,
           c_gate_w2, c_gate_b, c_on_g, w_out, r_group_w, r_group_b, r_exp_w, r_exp_b,
           e_w1, e_w3, e_w2):
    bsz, n_seq, d = x.shape
    depth = w_in.shape[0]
    t_real = n_seq + N_META
    k_top = min(TOPK_MAX, n_seq // 4)
    tp = -(-(FRONT + t_real) // BLK) * BLK
    n = bsz * tp

    (waq, wak, wav, wiq, wik, wiw, wbq, wbk, wbv, wbf, wcq, wck, wcv, wcr, wcg) = _split_w_in(w_in)
    zc = lambda w: jnp.zeros((depth, d, w), w_in.dtype)
    wcat = jnp.concatenate([waq, wak, wav, wiq, wik, zc(LANES - IDX_DIM), zc(LANES),
                            wbq, wbk, wbv, wcq, wck, wcv, wcg], axis=-1).astype(bf16)
    ws = jnp.concatenate([wiw, wbf, wcr, zc(LANES - S_CR - C_GATE_RANK)], axis=-1).astype(bf16)
    wo = w_out.astype(bf16)
    wr = jnp.concatenate([r_group_w, zc(R_EXP - N_GROUPS), r_exp_w, zc(LANES - R_EXP - N_EXPERTS)],
                         axis=-1).astype(bf16)
    zl = lambda w: jnp.zeros((depth, w), f32)
    br = jnp.concatenate([r_group_b, zl(R_EXP - N_GROUPS), r_exp_b, zl(LANES - R_EXP - N_EXPERTS)],
                         axis=-1)[:, None, :]
    fbias = jnp.concatenate([zl(S_BF), b_f_bias, zl(LANES - S_BF - B_HEADS)], axis=-1)[:, None, :]
    dk = C_HEADS * C_DK
    w2p = jnp.concatenate([jnp.zeros((depth, S_CR, dk), f32), c_gate_w2,
                           jnp.zeros((depth, LANES - S_CR - C_GATE_RANK, dk), f32)], axis=1).astype(bf16)
    gb = c_gate_b[:, None, :]
    og = c_on_g[:, None, :]
    g_mix = norm_mix_g[:, None, :]
    g_ffn = norm_ffn_g[:, None, :]
    gaq, gak, gbq, gbk = (g[:, None, :] for g in (a_qn_g, a_kn_g, b_qn_g, b_kn_g))
    emat = (jnp.arange(dk, dtype=i32)[:, None] // C_DK
            == jnp.arange(C_WIDTH, dtype=i32)[None, :] // C_DV).astype(bf16)
    tabs = _rope_tables(tp)

    h0 = jnp.concatenate([
        jnp.zeros((bsz, FRONT, d), f32),
        jnp.broadcast_to(meta[None].astype(f32), (bsz, N_META, d)),
        x.astype(f32),
        jnp.zeros((bsz, tp - FRONT - t_real, d), f32)], axis=1).reshape(n, d)
    n_tiles = -(-(EXP_TOPK_CONST * n + N_EXPERTS * (MOE_TM - 1)) // MOE_TM)

    def layer(li, carry):
        hmid, y2 = carry
        l = jnp.reshape(jnp.asarray(li, i32), (1,))
        h, zcat, zs = _inproj(l, hmid, y2, g_mix, wcat, ws)
        zcat3 = zcat.reshape(bsz, tp, N_CAT)
        zs3 = zs.reshape(bsz, tp, LANES)
        akr, ikr, bka = _kprep(l, zcat3, zs3, tabs, gak, gbk, fbias, t_real)
        oa = _dsa(l, zcat3, zs3, akr, ikr, tabs, gaq, t_real, k_top)
        ob = _fattn(l, zcat3, bka, gbq)
        oc = _gla(l, zcat3, zs3, w2p, gb, og, emat, t_real)
        hmid2, u, eid, ew = _outproj(l, oa.reshape(n, A_WIDTH), ob.reshape(n, B_WIDTH),
                                     oc.reshape(n, C_WIDTH), h, wo, g_ffn, wr, br, t_real, tp)
        tok_r, w_r, te, n_act, pos_a = _route(eid, ew, n_tiles)
        ys = _moe(l, te, n_act, jnp.take(u, tok_r, axis=0, mode="clip"), e_w1, e_w3, e_w2, w_r)
        return hmid2, jnp.take(ys, pos_a, axis=0, mode="clip")

    y0 = lax.optimization_barrier(jnp.zeros((EXP_TOPK_CONST * n, d), bf16))
    hmid, y2 = lax.fori_loop(0, depth, layer, (h0, y0))
    out = _final(hmid, y2).reshape(bsz, tp, d)
    return out[:, FRONT + N_META:FRONT + t_real].astype(x.dtype)
```

```python
import functools

import jax
import jax.numpy as jnp
from jax import lax
from jax.experimental import pallas as pl
from jax.experimental.pallas import tpu as pltpu

f32 = jnp.float32
bf16 = jnp.bfloat16
i32 = jnp.int32

D_MODEL = 2048
CHUNK = 64
N_META = 16
ROPE_THETA = 10000.0
EPS = 1e-6
HEAD_DIM = 128
A_HEADS = 6
IDX_HEADS = 16
IDX_DIM = 64
TOPK_MAX = 256
B_HEADS = 6
C_HEADS = 4
C_DK = 64
C_DV = 128
C_GATE_RANK = 16
C_TAU = 16.0
N_GROUPS = 4
EXP_PER_GROUP = 8
N_EXPERTS = N_GROUPS * EXP_PER_GROUP
D_EXPERT = 512
A_WIDTH = A_HEADS * HEAD_DIM
B_WIDTH = B_HEADS * HEAD_DIM
C_WIDTH = C_HEADS * C_DV
IN_SPLITS = (A_WIDTH, HEAD_DIM, HEAD_DIM, IDX_HEADS * IDX_DIM, IDX_DIM, IDX_HEADS,
             B_WIDTH, B_WIDTH, B_WIDTH, B_HEADS,
             C_HEADS * C_DK, C_HEADS * C_DK, C_WIDTH, C_GATE_RANK, C_WIDTH)

LANES = 128
SUBLANES = 8
FRONT = (-N_META) % CHUNK
BLK = 128
T_AQ, T_AK, T_AV, T_IQ, T_IK = 0, 6, 7, 8, 16
T_BQ, T_BK, T_BV = 18, 24, 30
T_CQ, T_CK, T_CV, T_CG = 36, 38, 40, 44
N_CAT = 48 * LANES
S_IW, S_BF, S_CR = 0, 16, 22
R_GRP, R_EXP = 0, 32
INPROJ_TN = 1536
GLA_C = 32
DSA_G = 3
MOE_TM = 256
EXP_TOPK_CONST = 2
NEG = -1e30
LOG2E = 1.4426950408889634
INT_MIN = -2 ** 31
VMEM_LIMIT = 56 * 1024 * 1024


def _cparams(sem):
    return pltpu.CompilerParams(dimension_semantics=sem, vmem_limit_bytes=VMEM_LIMIT)


def _row_tile(n, cap):
    t = cap
    while n % t:
        t //= 2
    return t


def _log_sigmoid(x):
    return jnp.minimum(x, 0.0) - jnp.log(1.0 + jnp.exp(-jnp.abs(x)))


def _dot(a, b):
    return jnp.dot(a, b, preferred_element_type=f32)


def _dot_nt(a, b):
    return lax.dot_general(a, b, (((1,), (1,)), ((), ())), preferred_element_type=f32)


def _dot_tn(a, b):
    return lax.dot_general(a, b, (((0,), (0,)), ((), ())), preferred_element_type=f32)


def _split3(x):
    hi = x.astype(bf16)
    r1 = x - hi.astype(f32)
    mid = r1.astype(bf16)
    lo = (r1 - mid.astype(f32)).astype(bf16)
    return hi, mid, lo


def _rope128(x, cos, sin, lane):
    return x * cos + pltpu.roll(x, 64, 1) * jnp.where(lane < 64, -sin, sin)


def _rope64(x, cos, sin, lane):
    low = (lane & 63) < 32
    return (x * cos + pltpu.roll(x, 32, 1) * jnp.where(low, 0.0, sin)
            + pltpu.roll(x, 96, 1) * jnp.where(low, -sin, 0.0))


def _rms_gain(x, g):
    return x * lax.rsqrt(jnp.mean(x * x, axis=-1, keepdims=True) + EPS) * g


def _prenorm_kernel(l_ref, hmid_ref, ya_ref, yb_ref, g_ref, ws_ref, h_ref, xn_ref, zs_ref):
    h = hmid_ref[...] + ya_ref[...].astype(f32) + yb_ref[...].astype(f32)
    h_ref[...] = h
    xn = _rms_gain(h, g_ref[...]).astype(bf16)
    xn_ref[...] = xn
    zs_ref[...] = _dot(xn, ws_ref[...])


def _inproj_kernel(l_ref, xn_ref, w_ref, z_ref):
    z_ref[...] = _dot(xn_ref[...], w_ref[...]).astype(bf16)


def _inproj(l, hmid, y2, gain, wcat, ws):
    n, d = hmid.shape
    tm = _row_tile(n, 512)
    nt = n // tm
    row = lambda w: pl.BlockSpec((tm, w), lambda i, l: (i, 0))
    h, xn, zs = pl.pallas_call(
        _prenorm_kernel,
        grid_spec=pltpu.PrefetchScalarGridSpec(
            num_scalar_prefetch=1, grid=(nt,),
            in_specs=[row(d), row(d),
                      pl.BlockSpec((tm, d), lambda i, l: (i + nt, 0)),
                      pl.BlockSpec((None, 1, d), lambda i, l: (l[0], 0, 0)),
                      pl.BlockSpec((None, d, LANES), lambda i, l: (l[0], 0, 0))],
            out_specs=[row(d), row(d), row(LANES)]),
        out_shape=[jax.ShapeDtypeStruct((n, d), f32), jax.ShapeDtypeStruct((n, d), bf16),
                   jax.ShapeDtypeStruct((n, LANES), f32)],
        compiler_params=_cparams(("parallel",)),
        name="prenorm",
    )(l, hmid, y2, y2, gain, ws)
    tn = INPROJ_TN
    zcat = pl.pallas_call(
        _inproj_kernel,
        grid_spec=pltpu.PrefetchScalarGridSpec(
            num_scalar_prefetch=1, grid=(N_CAT // tn, nt),
            in_specs=[pl.BlockSpec((tm, d), lambda j, i, l: (i, 0)),
                      pl.BlockSpec((None, d, tn), lambda j, i, l: (l[0], 0, j))],
            out_specs=pl.BlockSpec((tm, tn), lambda j, i, l: (i, j))),
        out_shape=jax.ShapeDtypeStruct((n, N_CAT), bf16),
        compiler_params=_cparams(("parallel", "parallel")),
        name="inproj",
    )(l, xn, wcat)
    return h, zcat, zs


def _kprep_kernel(l_ref, ak_ref, ik_ref, bk_ref, zs_ref,
                  cos_ref, sin_ref, icos_ref, isin_ref, gak_ref, gbk_ref, fb_ref,
                  akr_ref, ikr_ref, bka_ref, carry_ref, *, t_real):
    k = pl.program_id(1)
    lane = lax.broadcasted_iota(i32, (BLK, LANES), 1)
    row = lax.broadcasted_iota(i32, (BLK, LANES), 0)
    pos = k * BLK + row
    valid = (pos >= FRONT) & (pos < FRONT + t_real)

    x = ak_ref[...].astype(f32)
    akr_ref[...] = _rope128(_rms_gain(x, gak_ref[...]), cos_ref[...], sin_ref[...], lane).astype(bf16)
    ikr_ref[...] = _rope64(ik_ref[...].astype(f32), icos_ref[...], isin_ref[...], lane).astype(bf16)

    @pl.when(k == 0)
    def _():
        carry_ref[...] = jnp.zeros_like(carry_ref)

    lf = jnp.where(valid, _log_sigmoid(zs_ref[...] + fb_ref[...]), 0.0)
    tri = (row >= lane).astype(bf16)
    hi, mid, lo = _split3(lf)
    fcum = _dot(tri, hi) + _dot(tri, mid) + _dot(tri, lo) + carry_ref[...]
    carry_ref[...] = fcum[BLK - 1:BLK, :]

    fs = jnp.where(valid, fcum * (-(HEAD_DIM ** 0.5)), NEG)
    for h in range(B_HEADS):
        sl = slice(h * LANES, (h + 1) * LANES)
        bka_ref[h, :, 0:LANES] = _rms_gain(bk_ref[:, sl].astype(f32), gbk_ref[...]).astype(bf16)
        p0, p1, p2 = _split3(fs[:, S_BF + h:S_BF + h + 1])
        aug = jnp.where(lane == 0, p0.astype(f32),
                        jnp.where(lane == 1, p1.astype(f32), jnp.where(lane == 2, p2.astype(f32), 0.0)))
        bka_ref[h, :, LANES:2 * LANES] = aug.astype(bf16)


def _kprep(l, zcat3, zs3, tabs, gak, gbk, fbias, t_real):
    bsz, tp, _ = zcat3.shape
    nkb = tp // BLK
    cos, sin, icos, isin = tabs
    tile = lambda c: pl.BlockSpec((None, BLK, LANES), lambda b, k, l, c=c: (b, k, c))
    wide = lambda c: pl.BlockSpec((None, BLK, B_WIDTH), lambda b, k, l, c=c: (b, k, c))
    tab = pl.BlockSpec((BLK, LANES), lambda b, k, l: (k, 0))
    gain = pl.BlockSpec((None, 1, LANES), lambda b, k, l: (l[0], 0, 0))
    return pl.pallas_call(
        functools.partial(_kprep_kernel, t_real=t_real),
        grid_spec=pltpu.PrefetchScalarGridSpec(
            num_scalar_prefetch=1, grid=(bsz, nkb),
            in_specs=[tile(T_AK), tile(T_IK),
                      wide(T_BK * LANES // B_WIDTH),
                      pl.BlockSpec((None, BLK, LANES), lambda b, k, l: (b, k, 0)),
                      tab, tab, tab, tab, gain, gain, gain],
            out_specs=[
                pl.BlockSpec((None, BLK, LANES), lambda b, k, l: (b, k, 0)),
                pl.BlockSpec((None, BLK, LANES), lambda b, k, l: (b, k, 0)),
                pl.BlockSpec((None, B_HEADS, BLK, 2 * LANES), lambda b, k, l: (b, 0, k, 0)),
            ],
            scratch_shapes=[pltpu.VMEM((1, LANES), f32)]),
        out_shape=[jax.ShapeDtypeStruct((bsz, tp, LANES), bf16),
                   jax.ShapeDtypeStruct((bsz, tp, LANES), bf16),
                   jax.ShapeDtypeStruct((bsz, B_HEADS, tp, 2 * LANES), bf16)],
        compiler_params=_cparams(("parallel", "arbitrary")),
        name="kprep",
    )(l, zcat3, zcat3, zcat3, zs3, cos, sin, icos, isin, gak, gbk, fbias)


def _dsa_kernel(l_ref, aq_ref, iq_ref, zs_ref, akr_ref, ikr_ref, av_ref,
                cos_ref, sin_ref, icos_ref, isin_ref, gq_ref, out_ref,
                key_ref, iqs_ref, q6_ref, iwt_ref, m_ref, s_ref, acc_ref, stq_ref, *, t_real, k_top):
    i = pl.program_id(1)
    nk = i + 1
    lane = lax.broadcasted_iota(i32, (BLK, LANES), 1)

    for h in range(A_HEADS):
        x = aq_ref[:, h * LANES:(h + 1) * LANES].astype(f32)
        xr = _rope128(_rms_gain(x, gq_ref[...]), cos_ref[...], sin_ref[...], lane)
        q6_ref[h * BLK:(h + 1) * BLK, :] = xr.astype(bf16)
    for t in range(IDX_HEADS // 2):
        x = iq_ref[:, t * LANES:(t + 1) * LANES].astype(f32)
        xr = _rope64(x, icos_ref[...], isin_ref[...], lane)
        iqs_ref[(2 * t) * BLK:(2 * t + 1) * BLK, :] = xr.astype(bf16)
        iqs_ref[(2 * t + 1) * BLK:(2 * t + 2) * BLK, :] = pltpu.roll(xr, 64, 1).astype(bf16)
    iwt_ref[...] = (zs_ref[...] * (IDX_HEADS ** -0.5 * IDX_DIM ** -0.5)).T

    gb = DSA_G * BLK
    ng = lax.div(nk, jnp.int32(DSA_G))
    nr = nk - ng * DSA_G

    def over_keys(fn, init, merge=()):
        c, g0 = init, 0
        for m in merge:
            nm = lax.div(ng - g0, jnp.int32(m))
            c = lax.fori_loop(0, nm, lambda g, c, g0=g0, m=m: fn(pl.multiple_of((g0 + g * m) * gb, gb), m * gb, c), c)
            g0 = g0 + nm * m
        c = lax.fori_loop(g0, ng, lambda g, c: fn(pl.multiple_of(g * gb, gb), gb, c), c)
        return lax.fori_loop(0, nr, lambda r, c: fn(pl.multiple_of((ng * DSA_G + r) * BLK, BLK), BLK, c), c)

    def score_rows(k0, nrows, carry):
        dt = _dot_nt(ikr_ref[pl.ds(k0, nrows), :], iqs_ref[...])
        s = jnp.zeros((nrows, LANES), f32)
        for h in range(IDX_HEADS):
            s = s + iwt_ref[h:h + 1, :] * jnp.maximum(dt[:, h * LANES:(h + 1) * LANES], 0.0)
        kpos = k0 + lax.broadcasted_iota(i32, (nrows, LANES), 0)
        qpos = i * BLK + lax.broadcasted_iota(i32, (nrows, LANES), 1)
        adm = ((kpos >> 6) <= (qpos >> 6)) & (kpos >= FRONT) & (kpos < FRONT + t_real)
        bits = lax.bitcast_convert_type(s, i32)
        key = bits ^ ((bits >> 31) & 0x7FFFFFFF)
        key_ref[pl.ds(k0, nrows), :] = jnp.where(adm, key, INT_MIN)
        return carry

    over_keys(score_rows, 0, merge=(2,))

    def bit_body(t, thr_u):
        bit = jnp.left_shift(jnp.int32(1), 31 - t)
        cand_u = thr_u | bit
        cand_s = cand_u ^ INT_MIN

        def count_rows(k0, nrows, c):
            hit = (key_ref[pl.ds(k0, nrows), :] >= cand_s).astype(i32)
            for j in range(nrows // BLK):
                c = c + hit[j * BLK:(j + 1) * BLK, :]
            return c

        cnt = over_keys(count_rows, jnp.zeros((BLK, LANES), i32))
        tot = jnp.sum(cnt.astype(f32), axis=0, keepdims=True)
        return jnp.where(tot >= k_top, cand_u, thr_u)

    thr_u = lax.fori_loop(0, 32, bit_body, jnp.zeros((1, LANES), i32))
    thr_s = jnp.maximum(thr_u ^ INT_MIN, INT_MIN + 1)

    m_ref[...] = jnp.full(m_ref.shape, NEG, f32)
    s_ref[...] = jnp.zeros(s_ref.shape, f32)
    acc_ref[...] = jnp.zeros(acc_ref.shape, f32)
    c2 = HEAD_DIM ** -0.5 * LOG2E

    def attn_rows(k0, nrows, carry, st=None):
        if st is None:
            st = _dot_nt(akr_ref[pl.ds(k0, nrows), :], q6_ref[...])
        bias = jnp.where(key_ref[pl.ds(k0, nrows), :] >= thr_s, 0.0, NEG)
        m_news = []
        for h in range(A_HEADS):
            sl = slice(h * LANES, (h + 1) * LANES)
            sh = st[:, sl] + bias
            stq_ref[0, 0:nrows, sl] = sh
            m_news.append(jnp.maximum(m_ref[:, sl], jnp.max(sh, axis=0, keepdims=True)))
        ps, alphas = [], []
        for h in range(A_HEADS):
            sl = slice(h * LANES, (h + 1) * LANES)
            sh = stq_ref[0, 0:nrows, sl]
            m_old = m_ref[:, sl]
            m_new = m_news[h]
            alpha = jnp.exp2((m_old - m_new) * c2)
            p = jnp.exp2((sh - m_new) * c2)
            s_ref[:, sl] = s_ref[:, sl] * alpha + jnp.sum(p, axis=0, keepdims=True)
            m_ref[:, sl] = m_new
            alphas.append(alpha)
            ps.append(p.astype(bf16))
        pv = _dot_tn(av_ref[pl.ds(k0, nrows), :], jnp.concatenate(ps, axis=1))
        acc_ref[...] = acc_ref[...] * jnp.concatenate(alphas, axis=1) + pv
        return carry

    over_keys(attn_rows, 0, merge=(4, 2))

    for h in range(A_HEADS):
        sl = slice(h * LANES, (h + 1) * LANES)
        o = acc_ref[:, sl] / jnp.maximum(s_ref[:, sl], 1e-30)
        out_ref[:, sl] = o.T.astype(bf16)


def _dsa(l, zcat3, zs3, akr, ikr, tabs, gaq, t_real, k_top):
    bsz, tp, _ = zcat3.shape
    nkb = tp // BLK
    cos, sin, icos, isin = tabs
    tab = pl.BlockSpec((BLK, LANES), lambda b, i, l: (i, 0))
    full = pl.BlockSpec((None, tp, LANES), lambda b, i, l: (b, 0, 0))
    return pl.pallas_call(
        functools.partial(_dsa_kernel, t_real=t_real, k_top=k_top),
        grid_spec=pltpu.PrefetchScalarGridSpec(
            num_scalar_prefetch=1, grid=(bsz, nkb),
            in_specs=[
                pl.BlockSpec((None, BLK, A_WIDTH), lambda b, i, l: (b, i, 0)),
                pl.BlockSpec((None, BLK, IDX_HEADS * IDX_DIM), lambda b, i, l: (b, i, T_IQ * LANES // (IDX_HEADS * IDX_DIM))),
                pl.BlockSpec((None, BLK, LANES), lambda b, i, l: (b, i, 0)),
                full, full,
                pl.BlockSpec((None, tp, LANES), lambda b, i, l: (b, 0, T_AV)),
                tab, tab, tab, tab,
                pl.BlockSpec((None, 1, LANES), lambda b, i, l: (l[0], 0, 0)),
            ],
            out_specs=pl.BlockSpec((None, BLK, A_WIDTH), lambda b, i, l: (b, i, 0)),
            scratch_shapes=[
                pltpu.VMEM((tp, LANES), i32),
                pltpu.VMEM((IDX_HEADS * BLK, LANES), bf16),
                pltpu.VMEM((A_HEADS * BLK, LANES), bf16),
                pltpu.VMEM((LANES, BLK), f32),
                pltpu.VMEM((1, A_HEADS * BLK), f32),
                pltpu.VMEM((1, A_HEADS * BLK), f32),
                pltpu.VMEM((HEAD_DIM, A_HEADS * BLK), f32),
                pltpu.VMEM((1, 4 * DSA_G * BLK, A_HEADS * BLK), f32),
            ]),
        out_shape=jax.ShapeDtypeStruct((bsz, tp, A_WIDTH), bf16),
        compiler_params=_cparams(("parallel", "arbitrary")),
        name="dsa",
    )(l, zcat3, zcat3, zs3, akr, ikr, zcat3, cos, sin, icos, isin, gaq)


def _fattn_kernel(l_ref, bq_ref, bka_ref, bv_ref, gq_ref, out_ref, qa_ref, m_ref, s_ref, acc_ref, st_ref):
    i = pl.program_id(1)
    fb = bq_ref.shape[0]
    ones3 = jnp.where(lax.broadcasted_iota(i32, (fb, LANES), 1) < 3, 1.0, 0.0).astype(bf16)
    for h in range(B_HEADS):
        sl = slice(h * LANES, (h + 1) * LANES)
        qa_ref[h, :, 0:LANES] = _rms_gain(bq_ref[:, sl].astype(f32), gq_ref[...]).astype(bf16)
        qa_ref[h, :, LANES:2 * LANES] = ones3
    m_ref[...] = jnp.full(m_ref.shape, NEG, f32)
    s_ref[...] = jnp.zeros(s_ref.shape, f32)
    acc_ref[...] = jnp.zeros(acc_ref.shape, f32)
    c2 = HEAD_DIM ** -0.5 * LOG2E
    causal = lax.broadcasted_iota(i32, (fb, fb), 0) <= lax.broadcasted_iota(i32, (fb, fb), 1)

    def step(kb, diagonal):
        k0 = pl.multiple_of(kb * fb, fb)
        m_news = []
        for h in range(B_HEADS):
            st = _dot_nt(bka_ref[h, pl.ds(k0, fb), :], qa_ref[h])
            if diagonal:
                st = jnp.where(causal, st, NEG)
            st_ref[h] = st
            m_news.append(jnp.maximum(m_ref[h], jnp.max(st, axis=0, keepdims=True)))
        for h in range(B_HEADS):
            sl = slice(h * LANES, (h + 1) * LANES)
            st = st_ref[h]
            m_old = m_ref[h]
            m_new = m_news[h]
            alpha = jnp.exp2((m_old - m_new) * c2)
            p = jnp.exp2((st - m_new) * c2)
            s_ref[h] = s_ref[h] * alpha + jnp.sum(p, axis=0, keepdims=True)
            m_ref[h] = m_new
            acc_ref[h] = acc_ref[h] * alpha + _dot_tn(bv_ref[pl.ds(k0, fb), sl], p.astype(bf16))

    def body(kb, carry):
        step(kb, False)
        return carry

    lax.fori_loop(0, i, body, 0)
    step(i, True)
    for h in range(B_HEADS):
        sl = slice(h * LANES, (h + 1) * LANES)
        out_ref[:, sl] = (acc_ref[h] / s_ref[h]).T.astype(bf16)


def _fattn(l, zcat3, bka, gbq):
    bsz, tp, _ = zcat3.shape
    fb = _seq_tile(tp, 384)
    once = pl.Buffered(1)
    return pl.pallas_call(
        _fattn_kernel,
        grid_spec=pltpu.PrefetchScalarGridSpec(
            num_scalar_prefetch=1, grid=(bsz, tp // fb),
            in_specs=[
                pl.BlockSpec((None, fb, B_WIDTH), lambda b, i, l: (b, i, T_BQ * LANES // B_WIDTH)),
                pl.BlockSpec((None, B_HEADS, tp, 2 * LANES), lambda b, i, l: (b, 0, 0, 0), pipeline_mode=once),
                pl.BlockSpec((None, tp, B_WIDTH), lambda b, i, l: (b, 0, T_BV * LANES // B_WIDTH),
                             pipeline_mode=once),
                pl.BlockSpec((None, 1, LANES), lambda b, i, l: (l[0], 0, 0)),
            ],
            out_specs=pl.BlockSpec((None, fb, B_WIDTH), lambda b, i, l: (b, i, 0)),
            scratch_shapes=[
                pltpu.VMEM((B_HEADS, fb, 2 * LANES), bf16),
                pltpu.VMEM((B_HEADS, 1, fb), f32),
                pltpu.VMEM((B_HEADS, 1, fb), f32),
                pltpu.VMEM((B_HEADS, HEAD_DIM, fb), f32),
                pltpu.VMEM((B_HEADS, fb, fb), f32),
            ]),
        out_shape=jax.ShapeDtypeStruct((bsz, tp, B_WIDTH), bf16),
        compiler_params=_cparams(("parallel", "arbitrary")),
        name="fattn",
    )(l, zcat3, bka, zcat3, gbq)


def _gla_kernel(l_ref, cq_ref, ck_ref, cv_ref, cg_ref, zs_ref, w2_ref, gb_ref, og_ref, e_ref,
                out_ref, st_ref, oi_ref, pp_ref, rr_ref, *, t_real):
    c = pl.program_id(0)
    nb, cc, dk = cq_ref.shape

    @pl.when(c == 0)
    def _():
        st_ref[...] = jnp.zeros_like(st_ref)

    rowc = lax.broadcasted_iota(i32, (cc, 1), 0)
    pos = c * cc + rowc
    valid = (pos >= FRONT) & (pos < FRONT + t_real)
    tri = (lax.broadcasted_iota(i32, (cc, cc), 0) >= lax.broadcasted_iota(i32, (cc, cc), 1)).astype(bf16)
    lane_k = lax.broadcasted_iota(i32, (1, dk), 1)
    head_masks = [(lane_k >= h * C_DK) & (lane_k < (h + 1) * C_DK) for h in range(C_HEADS)]

    used = [(r // SUBLANES + 1) * SUBLANES for r in range(cc)]

    for b in range(nb):
        x = _dot(zs_ref[b].astype(bf16), w2_ref[...]) + gb_ref[...]
        la = jnp.where(valid, _log_sigmoid(x) * (1.0 / C_TAU), 0.0)
        hi, mid, lo = _split3(la)
        bc = _dot(tri, hi) + _dot(tri, mid) + _dot(tri, lo)
        q = cq_ref[b].astype(f32) * (C_DK ** -0.5)
        k = ck_ref[b].astype(f32)
        v = cv_ref[b]
        bc2 = bc * LOG2E
        blast2 = bc2[cc - 1:cc, :]
        qe = q * jnp.exp2(bc2)
        ke = k * jnp.exp2(blast2 - bc2)
        st = st_ref[b]
        stb = st.astype(bf16)
        oi_ref[b] = jnp.concatenate(
            [_dot_nt(jnp.where(head_masks[h], qe, 0.0).astype(bf16), stb) for h in range(C_HEADS)], axis=1)
        new_st = st * jnp.exp2(blast2)
        for h in range(C_HEADS):
            km = jnp.where(head_masks[h], ke, 0.0).astype(bf16)
            new_st = new_st + _dot_tn(v[:, h * C_DV:(h + 1) * C_DV], km)
        st_ref[b] = new_st
        for r in range(cc):
            nu = used[r]
            dec = jnp.exp2(jnp.minimum(bc2[r:r + 1, :] - bc2[0:nu, :], 0.0))
            pr = jnp.where(rowc[0:nu] <= r, q[r:r + 1, :] * k[0:nu, :] * dec, 0.0)
            if nu < cc:
                pr = jnp.concatenate([pr, jnp.zeros((cc - nu, dk), f32)], axis=0)
            pp_ref[b, r * cc:(r + 1) * cc, :] = pr.astype(bf16)

    for b in range(nb):
        rr_ref[b] = _dot(pp_ref[b], e_ref[...])

    for b in range(nb):
        vf = cv_ref[b].astype(f32)
        for r in range(cc):
            nu = used[r]
            oi_ref[b, r:r + 1, :] += jnp.sum(rr_ref[b, r * cc:r * cc + nu, :] * vf[0:nu, :], axis=0, keepdims=True)
        o = oi_ref[b]
        g = cg_ref[b].astype(f32)
        gs = g * (1.0 / (1.0 + jnp.exp(-g)))
        for h in range(C_HEADS):
            sl = slice(h * C_DV, (h + 1) * C_DV)
            out_ref[b, :, sl] = (_rms_gain(o[:, sl], og_ref[...]) * gs[:, sl]).astype(bf16)


def _gla(l, zcat3, zs3, w2p, gb, og, emat, t_real):
    bsz, tp, _ = zcat3.shape
    cc = GLA_C
    dk = C_HEADS * C_DK
    return pl.pallas_call(
        functools.partial(_gla_kernel, t_real=t_real),
        grid_spec=pltpu.PrefetchScalarGridSpec(
            num_scalar_prefetch=1, grid=(tp // cc,),
            in_specs=[
                pl.BlockSpec((bsz, cc, dk), lambda c, l: (0, c, T_CQ * LANES // dk)),
                pl.BlockSpec((bsz, cc, dk), lambda c, l: (0, c, T_CK * LANES // dk)),
                pl.BlockSpec((bsz, cc, C_WIDTH), lambda c, l: (0, c, T_CV * LANES // C_WIDTH)),
                pl.BlockSpec((bsz, cc, C_WIDTH), lambda c, l: (0, c, T_CG * LANES // C_WIDTH)),
                pl.BlockSpec((bsz, cc, LANES), lambda c, l: (0, c, 0)),
                pl.BlockSpec((None, LANES, dk), lambda c, l: (l[0], 0, 0)),
                pl.BlockSpec((None, 1, dk), lambda c, l: (l[0], 0, 0)),
                pl.BlockSpec((None, 1, C_DV), lambda c, l: (l[0], 0, 0)),
                pl.BlockSpec((dk, C_WIDTH), lambda c, l: (0, 0)),
            ],
            out_specs=pl.BlockSpec((bsz, cc, C_WIDTH), lambda c, l: (0, c, 0)),
            scratch_shapes=[pltpu.VMEM((bsz, C_DV, dk), f32), pltpu.VMEM((bsz, cc, C_WIDTH), f32),
                            pltpu.VMEM((bsz, cc * cc, dk), bf16), pltpu.VMEM((bsz, cc * cc, C_WIDTH), f32)]),
        out_shape=jax.ShapeDtypeStruct((bsz, tp, C_WIDTH), bf16),
        compiler_params=_cparams(("arbitrary",)),
        name="gla",
    )(l, zcat3, zcat3, zcat3, zcat3, zs3, w2p, gb, og, emat)


def _outproj_kernel(l_ref, oa_ref, ob_ref, oc_ref, h_ref, wo_ref, g_ref, wr_ref, br_ref,
                    hmid_ref, u_ref, eid_ref, ew_ref, *, t_real):
    tm = h_ref.shape[0]
    mix = (_dot(oa_ref[...], wo_ref[0:A_WIDTH, :])
           + _dot(ob_ref[...], wo_ref[A_WIDTH:A_WIDTH + B_WIDTH, :])
           + _dot(oc_ref[...], wo_ref[A_WIDTH + B_WIDTH:, :]))
    pos = pl.program_id(1) * tm + lax.broadcasted_iota(i32, (tm, 1), 0)
    valid = (pos >= FRONT) & (pos < FRONT + t_real)
    hm = h_ref[...] + jnp.where(valid, mix, 0.0)
    hmid_ref[...] = hm
    u = _rms_gain(hm, g_ref[...]).astype(bf16)
    u_ref[...] = u

    logits = _dot(u, wr_ref[...]) + br_ref[...]
    lane = lax.broadcasted_iota(i32, (tm, LANES), 1)
    lanef = lane.astype(f32)
    big = float(4 * LANES)
    first = lambda hit: jnp.min(jnp.where(hit, lanef, big), axis=-1, keepdims=True).astype(i32)
    gl = jnp.where(lane < R_GRP + N_GROUPS, logits, -jnp.inf)
    gmax = jnp.max(gl, axis=-1, keepdims=True)
    g_p = 1.0 / jnp.sum(jnp.exp(gl - gmax), axis=-1, keepdims=True)
    g_i = first(gl == gmax)
    e_lane = lane - R_EXP
    emask = (e_lane >= 0) & (e_lane < N_EXPERTS) & ((e_lane >> 3) == g_i)
    el = jnp.where(emask, logits, -jnp.inf)
    m1 = jnp.max(el, axis=-1, keepdims=True)
    i1 = first(el == m1)
    el2 = jnp.where(lane == i1, -jnp.inf, el)
    m2 = jnp.max(el2, axis=-1, keepdims=True)
    i2 = first(el2 == m2)
    r = jnp.exp(m2 - m1)
    w1 = g_p / (1.0 + r)
    w2 = g_p * r / (1.0 + r)
    eid_ref[...] = jnp.where(lane == 0, i1 - R_EXP, jnp.where(lane == 1, i2 - R_EXP, 0))
    ew_ref[...] = jnp.where(lane == 0, w1, jnp.where(lane == 1, w2, 0.0))


def _seq_tile(tp, cap):
    return max(t for t in range(BLK, cap + 1, BLK) if tp % t == 0)


def _outproj(l, oa, ob, oc, h, wo, gain, wr, br, t_real, tp):
    n, d = h.shape
    tm = _seq_tile(tp, 384)
    nj = tp // tm
    row = lambda w: pl.BlockSpec((tm, w), lambda b, j, l: (b * nj + j, 0))
    return pl.pallas_call(
        functools.partial(_outproj_kernel, t_real=t_real),
        grid_spec=pltpu.PrefetchScalarGridSpec(
            num_scalar_prefetch=1, grid=(n // tp, nj),
            in_specs=[row(A_WIDTH), row(B_WIDTH), row(C_WIDTH), row(d),
                      pl.BlockSpec((None, d, d), lambda b, j, l: (l[0], 0, 0)),
                      pl.BlockSpec((None, 1, d), lambda b, j, l: (l[0], 0, 0)),
                      pl.BlockSpec((None, d, LANES), lambda b, j, l: (l[0], 0, 0)),
                      pl.BlockSpec((None, 1, LANES), lambda b, j, l: (l[0], 0, 0))],
            out_specs=[row(d), row(d), row(LANES), row(LANES)]),
        out_shape=[jax.ShapeDtypeStruct((n, d), f32), jax.ShapeDtypeStruct((n, d), bf16),
                   jax.ShapeDtypeStruct((n, LANES), i32), jax.ShapeDtypeStruct((n, LANES), f32)],
        compiler_params=_cparams(("parallel", "parallel")),
        name="outproj",
    )(l, oa, ob, oc, h, wo, gain, wr, br)


def _moe_kernel(l_ref, te_ref, na_ref, xs_ref, w1_ref, w3_ref, w2_ref, rw_ref, ys_ref,
                w1b_ref, w3b_ref, w2b_ref):
    i = pl.program_id(0)

    @pl.when((i == 0) | (te_ref[i] != te_ref[jnp.maximum(i - 1, 0)]))
    def _():
        w1b_ref[...] = w1_ref[...].astype(bf16)
        w3b_ref[...] = w3_ref[...].astype(bf16)
        w2b_ref[...] = w2_ref[...].astype(bf16)

    @pl.when(i < na_ref[0])
    def _():
        x = xs_ref[...]
        h1 = _dot(x, w1b_ref[...])
        h3 = _dot(x, w3b_ref[...])
        hid = (h1 * (1.0 / (1.0 + jnp.exp(-h1))) * h3).astype(bf16)
        ys_ref[...] = (_dot(hid, w2b_ref[...]) * rw_ref[...]).astype(bf16)

    @pl.when(i >= na_ref[0])
    def _():
        ys_ref[...] = jnp.zeros_like(ys_ref)


def _moe_part_kernel(l_ref, te_ref, na_ref, xs_ref, w1_ref, w3_ref, w2_ref, rw_ref, base_ref, ys_ref, *scratch):
    del base_ref
    _moe_kernel(l_ref, te_ref, na_ref, xs_ref, w1_ref, w3_ref, w2_ref, rw_ref, ys_ref, *scratch)


def _moe(l, te, na, xs, w1, w3, w2, rw, p_total=None, tile_off=0, base=None):
    p, d = xs.shape
    tm = MOE_TM
    p_total = p if p_total is None else p_total
    in_specs = [
        pl.BlockSpec((tm, d), lambda i, l, te, na: (i, 0)),
        pl.BlockSpec((None, None, d, D_EXPERT), lambda i, l, te, na: (l[0], te[i], 0, 0)),
        pl.BlockSpec((None, None, d, D_EXPERT), lambda i, l, te, na: (l[0], te[i], 0, 0)),
        pl.BlockSpec((None, None, D_EXPERT, d), lambda i, l, te, na: (l[0], te[i], 0, 0)),
        pl.BlockSpec((tm, 1), lambda i, l, te, na: (i, 0)),
    ]
    args = [l, te, na, xs, w1, w3, w2, rw]
    aliases = {}
    if base is not None:
        in_specs.append(pl.BlockSpec(memory_space=pl.ANY))
        args.append(base)
        aliases = {len(args) - 1: 0}
    return pl.pallas_call(
        _moe_kernel if base is None else _moe_part_kernel,
        grid_spec=pltpu.PrefetchScalarGridSpec(
            num_scalar_prefetch=3, grid=(p // tm,),
            in_specs=in_specs,
            out_specs=pl.BlockSpec((tm, d), lambda i, l, te, na: (i + tile_off, 0)),
            scratch_shapes=[pltpu.VMEM((d, D_EXPERT), bf16), pltpu.VMEM((d, D_EXPERT), bf16),
                            pltpu.VMEM((D_EXPERT, d), bf16)]),
        out_shape=jax.ShapeDtypeStruct((p_total, d), bf16),
        input_output_aliases=aliases,
        compiler_params=_cparams(("arbitrary",)),
        name="moe",
    )(*args)


def _route(eid, ew, n_tiles):
    tm = MOE_TM
    p = n_tiles * tm
    n = eid.shape[0]
    e_flat = eid[:, :EXP_TOPK_CONST].T.reshape(-1)
    w_flat = ew[:, :EXP_TOPK_CONST].T.reshape(-1)
    na_all = e_flat.shape[0]
    hp = lax.Precision.HIGHEST
    ex = jnp.arange(N_EXPERTS, dtype=i32)[:, None]
    onehot = (ex == e_flat[None, :]).astype(f32)
    counts = jnp.sum(onehot, axis=1).astype(i32)
    pc = ((counts + tm - 1) // tm) * tm
    pend = jnp.cumsum(pc)
    po = pend - pc
    co = jnp.cumsum(counts) - counts
    order = jnp.argsort(e_flat, stable=True).astype(i32)
    r = jnp.arange(p, dtype=i32)
    step = (r[None, :] >= pend[:, None]).astype(f32)
    dlt = lambda v: jnp.concatenate([v[1:] - v[:-1], jnp.zeros((1,), v.dtype)]).astype(f32)
    tabs = jnp.stack([jnp.ones((N_EXPERTS,), f32), dlt(po), dlt(counts), dlt(co)])
    picked = jnp.dot(tabs, step, precision=hp).astype(i32)
    e_r = jnp.minimum(picked[0], N_EXPERTS - 1)
    local = r - (po[0] + picked[1])
    valid_r = (local < counts[0] + picked[2]) & (r < pend[-1])
    a_r = order[jnp.clip(co[0] + picked[3] + local, 0, na_all - 1)]
    tok_r = jnp.where(valid_r, jnp.where(a_r >= n, a_r - n, a_r), r % n)
    w_r = jnp.where(valid_r, w_flat[a_r], 0.0)
    n_act = (pend[-1] // tm).astype(i32)
    tile_e = e_r[::tm]
    te = jnp.where(jnp.arange(n_tiles, dtype=i32) < n_act, tile_e, tile_e[jnp.maximum(n_act - 1, 0)])
    inv = jnp.argsort(order).astype(i32)
    pos_a = jnp.dot((po - co).astype(f32)[None, :], onehot, precision=hp)[0].astype(i32) + inv
    return tok_r, w_r[:, None], te, n_act.reshape(1), pos_a


def _final_kernel(hmid_ref, ya_ref, yb_ref, out_ref):
    out_ref[...] = hmid_ref[...] + ya_ref[...].astype(f32) + yb_ref[...].astype(f32)


def _final(hmid, y2):
    n, d = hmid.shape
    tm = _row_tile(n, 512)
    nt = n // tm
    return pl.pallas_call(
        _final_kernel, grid=(nt,),
        in_specs=[pl.BlockSpec((tm, d), lambda i: (i, 0)), pl.BlockSpec((tm, d), lambda i: (i, 0)),
                  pl.BlockSpec((tm, d), lambda i: (i + nt, 0))],
        out_specs=pl.BlockSpec((tm, d), lambda i: (i, 0)),
        out_shape=jax.ShapeDtypeStruct((n, d), f32),
        compiler_params=_cparams(("parallel",)),
        name="final",
    )(hmid, y2, y2)


def _rope_tables(tp):
    pos = (jnp.arange(tp, dtype=f32) - FRONT)[:, None]

    def tab(dim, reps):
        inv = 1.0 / (ROPE_THETA ** (jnp.arange(0, dim, 2, dtype=f32) / dim))
        ang = pos * inv[None, :]
        return jnp.tile(jnp.cos(ang), (1, reps)), jnp.tile(jnp.sin(ang), (1, reps))

    cos, sin = tab(HEAD_DIM, 2)
    icos, isin = tab(IDX_DIM, 4)
    return cos, sin, icos, isin


def _split_w_in(w_in):
    parts, off = [], 0
    for s in IN_SPLITS:
        parts.append(w_in[..., off:off + s])
        off += s
    return parts


def kernel(x, meta, norm_mix_g, norm_ffn_g, w_in, a_qn_g, a_kn_g, b_qn_g, b_kn_g, b_f_bias,
           c_gate_w2, c_gate_b, c_on_g, w_out, r_group_w, r_group_b, r_exp_w, r_exp_b,
           e_w1, e_w3, e_w2):
    bsz, n_seq, d = x.shape
    depth = w_in.shape[0]
    t_real = n_seq + N_META
    k_top = min(TOPK_MAX, n_seq // 4)
    tp = -(-(FRONT + t_real) // BLK) * BLK
    n = bsz * tp

    (waq, wak, wav, wiq, wik, wiw, wbq, wbk, wbv, wbf, wcq, wck, wcv, wcr, wcg) = _split_w_in(w_in)
    zc = lambda w: jnp.zeros((depth, d, w), w_in.dtype)
    wcat = jnp.concatenate([waq, wak, wav, wiq, wik, zc(LANES - IDX_DIM), zc(LANES),
                            wbq, wbk, wbv, wcq, wck, wcv, wcg], axis=-1).astype(bf16)
    ws = jnp.concatenate([wiw, wbf, wcr, zc(LANES - S_CR - C_GATE_RANK)], axis=-1).astype(bf16)
    wo = w_out.astype(bf16)
    wr = jnp.concatenate([r_group_w, zc(R_EXP - N_GROUPS), r_exp_w, zc(LANES - R_EXP - N_EXPERTS)],
                         axis=-1).astype(bf16)
    zl = lambda w: jnp.zeros((depth, w), f32)
    br = jnp.concatenate([r_group_b, zl(R_EXP - N_GROUPS), r_exp_b, zl(LANES - R_EXP - N_EXPERTS)],
                         axis=-1)[:, None, :]
    fbias = jnp.concatenate([zl(S_BF), b_f_bias, zl(LANES - S_BF - B_HEADS)], axis=-1)[:, None, :]
    dk = C_HEADS * C_DK
    w2p = jnp.concatenate([jnp.zeros((depth, S_CR, dk), f32), c_gate_w2,
                           jnp.zeros((depth, LANES - S_CR - C_GATE_RANK, dk), f32)], axis=1).astype(bf16)
    gb = c_gate_b[:, None, :]
    og = c_on_g[:, None, :]
    g_mix = norm_mix_g[:, None, :]
    g_ffn = norm_ffn_g[:, None, :]
    gaq, gak, gbq, gbk = (g[:, None, :] for g in (a_qn_g, a_kn_g, b_qn_g, b_kn_g))
    emat = (jnp.arange(dk, dtype=i32)[:, None] // C_DK
            == jnp.arange(C_WIDTH, dtype=i32)[None, :] // C_DV).astype(bf16)
    tabs = _rope_tables(tp)

    h0 = jnp.concatenate([
        jnp.zeros((bsz, FRONT, d), f32),
        jnp.broadcast_to(meta[None].astype(f32), (bsz, N_META, d)),
        x.astype(f32),
        jnp.zeros((bsz, tp - FRONT - t_real, d), f32)], axis=1).reshape(n, d)
    n_tiles = -(-(EXP_TOPK_CONST * n + N_EXPERTS * (MOE_TM - 1)) // MOE_TM)

    def layer(li, carry):
        hmid, y2 = carry
        l = jnp.reshape(jnp.asarray(li, i32), (1,))
        h, zcat, zs = _inproj(l, hmid, y2, g_mix, wcat, ws)
        zcat3 = zcat.reshape(bsz, tp, N_CAT)
        zs3 = zs.reshape(bsz, tp, LANES)
        akr, ikr, bka = _kprep(l, zcat3, zs3, tabs, gak, gbk, fbias, t_real)
        oa = _dsa(l, zcat3, zs3, akr, ikr, tabs, gaq, t_real, k_top)
        ob = _fattn(l, zcat3, bka, gbq)
        oc = _gla(l, zcat3, zs3, w2p, gb, og, emat, t_real)
        hmid2, u, eid, ew = _outproj(l, oa.reshape(n, A_WIDTH), ob.reshape(n, B_WIDTH),
                                     oc.reshape(n, C_WIDTH), h, wo, g_ffn, wr, br, t_real, tp)
        tok_r, w_r, te, n_act, pos_a = _route(eid, ew, n_tiles)
        th = n_tiles // 2
        rh = th * MOE_TM
        xa = jnp.take(u, tok_r[:rh], axis=0, mode="clip")
        xb = jnp.take(u, tok_r[rh:], axis=0, mode="clip")
        ys = _moe(l, te[:th], jnp.minimum(n_act, th), xa, e_w1, e_w3, e_w2, w_r[:rh], p_total=n_tiles * MOE_TM)
        ys = _moe(l, te[th:], jnp.maximum(n_act - th, 0), xb, e_w1, e_w3, e_w2, w_r[rh:],
                  p_total=n_tiles * MOE_TM, tile_off=th, base=ys)
        return hmid2, jnp.take(ys, pos_a, axis=0, mode="clip")

    y0 = lax.optimization_barrier(jnp.zeros((EXP_TOPK_CONST * n, d), bf16))
    hmid, y2 = lax.fori_loop(0, depth, layer, (h0, y0))
    out = _final(hmid, y2).reshape(bsz, tp, d)
    return out[:, FRONT + N_META:FRONT + t_real].astype(x.dtype)
```
